```python
import math
import jax
import jax.numpy as jnp
from jax import lax
import numpy as np

D_MODEL = 1024
BATCH = 8
SEQ = 2048
DEPTH = 2
DEC_BATCH = 128
DEC_SEQ = 4
PAST_LEN = 16384
PAGE_SIZE = 128

RET_HEADS = 4
RET_DK = 64
RET_DV = 128
ML_HEADS = 4
ML_DH = 128
CONV_W = 4
RW_HEADS = 8
RW_DH = 64
RW_LORA_W = 64
RW_LORA_A = 64
RW_LORA_G = 128
RET_W = RET_HEADS * RET_DV
ML_W = ML_HEADS * ML_DH
RW_W = RW_HEADS * RW_DH
BRANCH_W = RET_W
N_BRANCH = 3
RW_COLS = 3 * RW_W + RW_LORA_W + RW_LORA_A + RW_LORA_G
IN_SPLITS = (RET_HEADS * RET_DK, RET_HEADS * RET_DK, RET_W, RET_W,
             ML_W, ML_W, ML_HEADS, ML_HEADS, ML_W,
             RW_COLS, N_BRANCH * D_MODEL)
RW_SPLITS = (RW_W, RW_W, RW_W, RW_LORA_W, RW_LORA_A, RW_LORA_G)
N_IN = sum(IN_SPLITS)
D_FF = 2816
N_EXPERTS = 8
TOP_K = 2
D_FF_EXPERT = 1408
N_DENSE = (DEPTH + 1) // 2
N_MOE = DEPTH // 2
CHUNK = 128
NORM_EPS = 1e-6
GN_EPS = 1e-5
RW_GN_EPS = 64e-5
ROPE_BASE = 10000.0
RW_DECAY_SCALE = 0.606531

kernel_name = 'hybrid_retention_mlstm_rwkv7_moe_step'


def _split(z, sizes):
    out, off = [], 0
    for s in sizes:
        out.append(z[..., off:off + s])
        off += s
    return out


def _rmsnorm(x, g):
    xf = x.astype(jnp.float32)
    y = xf * lax.rsqrt(jnp.mean(xf * xf, axis=-1, keepdims=True) + NORM_EPS)
    return (y * g.astype(jnp.float32)).astype(x.dtype)


def _head_norm(x, gain, bias, eps):
    xf = x.astype(jnp.float32)
    xc = xf - jnp.mean(xf, axis=-1, keepdims=True)
    y = xc * lax.rsqrt(jnp.mean(xc * xc, axis=-1, keepdims=True) + eps)
    y = y.reshape(x.shape[:-2] + (-1,)) * gain.astype(jnp.float32)
    if bias is not None:
        y = y + bias.astype(jnp.float32)
    return y


def _rotary(x, pos):
    half = x.shape[-1] // 2
    inv = ROPE_BASE ** (-jnp.arange(half, dtype=jnp.float32) / half)
    ang = pos.astype(jnp.float32)[:, None] * inv[None, :]
    cos = jnp.cos(ang)[None, :, None, :]
    sin = jnp.sin(ang)[None, :, None, :]
    xf = x.astype(jnp.float32)
    x1, x2 = xf[..., :half], xf[..., half:]
    return jnp.concatenate([x1 * cos - x2 * sin, x1 * sin + x2 * cos], axis=-1)


def _to_chunks(x, L):
    B, T = x.shape[0], x.shape[1]
    return jnp.moveaxis(x.reshape((B, T // L, L) + x.shape[2:]), 1, 0)


def _from_chunks(x):
    x = jnp.moveaxis(x, 0, 1)
    return x.reshape((x.shape[0], -1) + x.shape[3:])


def _retention(q, k, v, S0):
    L = math.gcd(q.shape[1], CHUNK)
    log_gamma = jnp.log1p(-(2.0 ** (-5.0 - jnp.arange(RET_HEADS, dtype=jnp.float32))))
    idx = jnp.arange(L, dtype=jnp.float32)
    diff = idx[:, None] - idx[None, :]
    causal = diff >= 0
    decay_intra = jnp.where(causal[None], jnp.exp(jnp.where(causal, diff, 0.0)[None] * log_gamma[:, None, None]), 0.0)
    decay_q = jnp.exp((idx[:, None] + 1.0) * log_gamma[None, :])
    decay_k = jnp.exp((L - 1.0 - idx)[:, None] * log_gamma[None, :])
    decay_chunk = jnp.exp(L * log_gamma)

    def step(S, blk):
        qc, kc, vc = blk
        s = jnp.einsum('blhd,bshd->bhls', qc, kc) * decay_intra[None]
        inner = jnp.einsum('bhls,bshv->blhv', s, vc)
        cross = jnp.einsum('blhd,bhdv->blhv', qc, S) * decay_q[None, :, :, None]
        S_new = S * decay_chunk[None, :, None, None] + jnp.einsum('blhd,blhv->bhdv', kc * decay_k[None, :, :, None], vc)
        return S_new, inner + cross

    S, o = lax.scan(step, S0.astype(jnp.float32),
                    (_to_chunks(q, L), _to_chunks(k, L), _to_chunks(v.astype(jnp.float32), L)))
    return _from_chunks(o), S


def _mlstm(q, k, v, ig, lf, C0, n0, m0):
    L = math.gcd(q.shape[1], CHUNK)
    tri = jnp.tril(jnp.ones((L, L), dtype=bool))

    def step(carry, blk):
        C, n, m = carry
        qc, kc, vc, ic, fc = blk
        F = jnp.cumsum(fc, axis=1)
        Dlog = F[:, :, None, :] - F[:, None, :, :] + ic[:, None, :, :]
        Dlog = jnp.where(tri[None, :, :, None], Dlog, -jnp.inf)
        inter = F + m[:, None, :]
        m_t = jnp.maximum(jnp.max(Dlog, axis=2), inter)
        W = jnp.exp(Dlog - m_t[:, :, None, :])
        s = jnp.einsum('blhd,bshd->blsh', qc, kc) * W
        a_inter = jnp.exp(inter - m_t)
        num = jnp.einsum('blsh,bshv->blhv', s, vc) + a_inter[..., None] * jnp.einsum('blhd,bhdv->blhv', qc, C)
        den = jnp.sum(s, axis=2) + a_inter * jnp.einsum('blhd,bhd->blh', qc, n)
        h = num / jnp.maximum(jnp.abs(den), jnp.exp(-m_t))[..., None]
        FL = F[:, -1]
        log_end = FL[:, None, :] - F + ic
        m_new = jnp.maximum(FL + m, jnp.max(log_end, axis=1))
        w_end = jnp.exp(log_end - m_new[:, None, :])
        a_old = jnp.exp(FL + m - m_new)
        C_new = a_old[..., None, None] * C + jnp.einsum('blh,blhd,blhv->bhdv', w_end, kc, vc)
        n_new = a_old[..., None] * n + jnp.einsum('blh,blhd->bhd', w_end, kc)
        return (C_new, n_new, m_new), h

    f32 = jnp.float32
    (C, n, m), h = lax.scan(step, (C0.astype(f32), n0.astype(f32), m0.astype(f32)),
                            (_to_chunks(q.astype(f32), L), _to_chunks(k.astype(f32), L), _to_chunks(v.astype(f32), L),
                             _to_chunks(ig, L), _to_chunks(lf, L)))
    return _from_chunks(h), C, n, m


def _rwkv_scan(r, w, k, v, kk, a, S0):
    def step(S, inp):
        rt, wt, kt, vt, kkt, at = inp
        Skk = jnp.einsum('bhvk,bhk->bhv', S, kkt)
        S = S * wt[:, :, None, :] - Skk[..., None] * (kkt * at)[:, :, None, :] + vt[..., None] * kt[:, :, None, :]
        return S, jnp.einsum('bhvk,bhk->bhv', S, rt)

    xs = tuple(jnp.moveaxis(t, 1, 0) for t in (r, w, k, v, kk, a))
    S, o = lax.scan(step, S0.astype(jnp.float32), xs)
    return jnp.moveaxis(o, 0, 1), S


def _causal_conv(x, buf, w, b):
    T = x.shape[1]
    xp = jnp.concatenate([buf.astype(x.dtype), x], axis=1)
    y = b + sum(xp[:, j:j + T] * w[j] for j in range(CONV_W))
    return y, xp[:, xp.shape[1] - (CONV_W - 1):]


def _token_shift(z, prev, mu):
    zp = jnp.concatenate([prev.astype(z.dtype)[:, None], z[:, :-1]], axis=1)
    return z + (zp - z) * mu, z[:, -1]


def _swiglu(h, wg, wu, wd):
    return (jax.nn.silu(h @ wg) * (h @ wu)) @ wd


def _moe(h, router, wg, wu, wd):
    logits = jnp.einsum('btd,de->bte', h, router).astype(jnp.float32)
    top_v, top_i = lax.top_k(logits, TOP_K)
    probs = jax.nn.softmax(top_v, axis=-1)
    combine = jnp.sum(jax.nn.one_hot(top_i, N_EXPERTS, dtype=jnp.float32) * probs[..., None], axis=-2)
    y = jnp.zeros(h.shape[:-1] + (wd.shape[-1],), h.dtype)
    for e in range(N_EXPERTS):
        y = y + combine[..., e:e + 1].astype(h.dtype) * _swiglu(h, wg[e], wu[e], wd[e])
    return y


def _mixer(x, pos, states, p):
    ret_S, ml_C, ml_n, ml_m, ml_buf, rw_S, rw_prev = states
    B, T = x.shape[0], x.shape[1]
    dt = x.dtype
    f32 = jnp.float32
    h = _rmsnorm(x, p['norm_mix'])
    z = h @ p['w_in']
    rq, rk, rv, rg, mx, mv, mi, mf, mo, zrw, zgate = _split(z, IN_SPLITS)

    q = _rotary(rq.reshape(B, T, RET_HEADS, RET_DK), pos)
    k = _rotary(rk.reshape(B, T, RET_HEADS, RET_DK), pos) * (RET_DK ** -0.5)
    v = rv.reshape(B, T, RET_HEADS, RET_DV)
    o_ret, ret_S_new = _retention(q, k, v, ret_S)
    y_ret = jax.nn.silu(rg.astype(f32)) * _head_norm(o_ret, p['ret_gn'], None, GN_EPS)

    xc, buf_new = _causal_conv(mx, ml_buf, p['ml_conv_w'], p['ml_conv_b'])
    xc = jax.nn.silu(xc)
    xch = xc.reshape(B, T, ML_HEADS, ML_DH)
    mq = jnp.einsum('bthd,hde->bthe', xch, p['ml_wq'])
    mk = jnp.einsum('bthd,hde->bthe', xch, p['ml_wk']) * (ML_DH ** -0.5)
    mvh = mv.reshape(B, T, ML_HEADS, ML_DH)
    ig = (mi + p['ml_bi']).astype(f32)
    lf = jax.nn.log_sigmoid((mf + p['ml_bf']).astype(f32))
    o_ml, C_new, n_new, m_new = _mlstm(mq, mk, mvh, ig, lf, ml_C, ml_n, ml_m)
    y_ml = jax.nn.sigmoid(mo.astype(f32)) * (_head_norm(o_ml, p['ml_gn'], None, GN_EPS)
                                              + p['ml_skip'].astype(f32) * xc.astype(f32))

    zs, prev_new = _token_shift(zrw, rw_prev, p['rw_mu'])
    r, kr, vr, wl, al, gl = _split(zs.astype(f32), RW_SPLITS)
    w = jnp.exp(-RW_DECAY_SCALE * jax.nn.sigmoid(p['rw_w0'] + jnp.tanh(wl) @ p['rw_w2']))
    a = jax.nn.sigmoid(p['rw_a0'] + al @ p['rw_a2'])
    g = jax.nn.sigmoid(gl) @ p['rw_g2']
    hd = lambda t: t.reshape(B, T, RW_HEADS, RW_DH)
    kk = hd(kr * p['rw_kk'])
    kk = kk / jnp.maximum(jnp.sqrt(jnp.sum(kk * kk, axis=-1, keepdims=True)), 1e-12)
    kr = kr * (1.0 + (a - 1.0) * p['rw_ka'])
    o_rw, rw_S_new = _rwkv_scan(hd(r), hd(w), hd(kr), hd(vr), kk, hd(a), rw_S)
    bonus = jnp.sum(hd(r * kr * p['rw_rk']), axis=-1, keepdims=True) * hd(vr)
    y_rw = (_head_norm(o_rw, p['rw_gn_w'], p['rw_gn_b'], RW_GN_EPS) + bonus.reshape(B, T, RW_W)) * g

    branches = jnp.stack([y_ret, y_ml, y_rw], axis=2).astype(dt)
    proj = jnp.einsum('btnc,ncd->btnd', branches, p['w_branch'])
    gates = jax.nn.sigmoid(zgate.reshape(B, T, N_BRANCH, D_MODEL))
    out = jnp.sum(gates * proj, axis=2) @ p['w_out']
    new = (ret_S_new, C_new, n_new, m_new, buf_new, rw_S_new, prev_new)
    return x + out.astype(dt), tuple(ns.astype(s.dtype) for ns, s in zip(new, states))


def setup_inputs(seed: int = 0) -> dict:
    key = jax.random.key(seed)
    ks = iter(jax.random.split(key, 64))
    f32 = jnp.float32

    def nrm(shape, s):
        return s * jax.random.normal(next(ks), shape, f32)

    def gain(shape):
        return 1.0 + nrm(shape, 0.02)

    inp = {}
    inp['x_prompt'] = nrm((BATCH, SEQ, D_MODEL), 1.0)
    inp['x_sample'] = nrm((DEC_BATCH, DEC_SEQ, D_MODEL), 1.0)
    inp['state_ret'] = nrm((DEPTH, DEC_BATCH, RET_HEADS, RET_DK, RET_DV), 0.3)
    inp['state_mlstm_C'] = nrm((DEPTH, DEC_BATCH, ML_HEADS, ML_DH, ML_DH), 0.1)
    inp['state_mlstm_n'] = jnp.abs(nrm((DEPTH, DEC_BATCH, ML_HEADS, ML_DH), 0.3))
    inp['state_mlstm_m'] = nrm((DEPTH, DEC_BATCH, ML_HEADS), 1.0)
    inp['state_mlstm_conv'] = nrm((DEPTH, DEC_BATCH, CONV_W - 1, ML_W), 1.0)
    inp['state_rwkv'] = nrm((DEPTH, DEC_BATCH, RW_HEADS, RW_DH, RW_DH), 0.1)
    inp['state_rwkv_shift'] = nrm((DEPTH, DEC_BATCH, RW_COLS), 1.0)
    inp['norm_mix'] = gain((DEPTH, D_MODEL))
    inp['w_in'] = nrm((DEPTH, D_MODEL, N_IN), D_MODEL ** -0.5)
    inp['ret_gn'] = gain((DEPTH, RET_W))
    inp['ml_conv_w'] = nrm((DEPTH, CONV_W, ML_W), CONV_W ** -0.5)
    inp['ml_conv_b'] = nrm((DEPTH, ML_W), 0.01)
    inp['ml_wq'] = nrm((DEPTH, ML_HEADS, ML_DH, ML_DH), ML_DH ** -0.5)
    inp['ml_wk'] = nrm((DEPTH, ML_HEADS, ML_DH, ML_DH), ML_DH ** -0.5)
    inp['ml_bi'] = nrm((DEPTH, ML_HEADS), 0.1)
    inp['ml_bf'] = jnp.linspace(3.0, 6.0, ML_HEADS, dtype=f32)[None, :] + nrm((DEPTH, ML_HEADS), 0.1)
    inp['ml_gn'] = gain((DEPTH, ML_W))
    inp['ml_skip'] = gain((DEPTH, ML_W))
    inp['rw_mu'] = jax.random.uniform(next(ks), (DEPTH, RW_COLS), f32)
    inp['rw_w0'] = nrm((DEPTH, RW_W), 1.0)
    inp['rw_w2'] = nrm((DEPTH, RW_LORA_W, RW_W), RW_LORA_W ** -0.5)
    inp['rw_a0'] = nrm((DEPTH, RW_W), 0.5)
    inp['rw_a2'] = nrm((DEPTH, RW_LORA_A, RW_W), RW_LORA_A ** -0.5)
    inp['rw_g2'] = nrm((DEPTH, RW_LORA_G, RW_W), RW_LORA_G ** -0.5)
    inp['rw_kk'] = 0.85 + nrm((DEPTH, RW_W), 0.02)
    inp['rw_ka'] = gain((DEPTH, RW_W))
    inp['rw_rk'] = nrm((DEPTH, RW_W), 0.1)
    inp['rw_gn_w'] = gain((DEPTH, RW_W))
    inp['rw_gn_b'] = nrm((DEPTH, RW_W), 0.01)
    inp['w_branch'] = nrm((DEPTH, N_BRANCH, BRANCH_W, D_MODEL), BRANCH_W ** -0.5)
    inp['w_out'] = nrm((DEPTH, D_MODEL, D_MODEL), D_MODEL ** -0.5)
    inp['norm_ffn'] = gain((DEPTH, D_MODEL))
    inp['ffn_w_gate'] = nrm((N_DENSE, D_MODEL, D_FF), D_MODEL ** -0.5)
    inp['ffn_w_up'] = nrm((N_DENSE, D_MODEL, D_FF), D_MODEL ** -0.5)
    inp['ffn_w_down'] = nrm((N_DENSE, D_FF, D_MODEL), D_FF ** -0.5)
    inp['moe_router'] = nrm((N_MOE, D_MODEL, N_EXPERTS), D_MODEL ** -0.5)
    inp['moe_w_gate'] = nrm((N_MOE, N_EXPERTS, D_MODEL, D_FF_EXPERT), D_MODEL ** -0.5)
    inp['moe_w_up'] = nrm((N_MOE, N_EXPERTS, D_MODEL, D_FF_EXPERT), D_MODEL ** -0.5)
    inp['moe_w_down'] = nrm((N_MOE, N_EXPERTS, D_FF_EXPERT, D_MODEL), D_FF_EXPERT ** -0.5)
    inp['final_norm'] = gain((D_MODEL,))
    return inp


def reference(x_prompt, x_sample, state_ret, state_mlstm_C, state_mlstm_n, state_mlstm_m, state_mlstm_conv,
              state_rwkv, state_rwkv_shift, norm_mix, w_in, ret_gn, ml_conv_w, ml_conv_b, ml_wq, ml_wk, ml_bi,
              ml_bf, ml_gn, ml_skip, rw_mu, rw_w0, rw_w2, rw_a0, rw_a2, rw_g2, rw_kk, rw_ka, rw_rk, rw_gn_w,
              rw_gn_b, w_branch, w_out, norm_ffn, ffn_w_gate, ffn_w_up, ffn_w_down, moe_router, moe_w_gate,
              moe_w_up, moe_w_down, final_norm):
    Bp = x_prompt.shape[0]
    pos_p = jnp.arange(x_prompt.shape[1], dtype=jnp.int32)
    pos_s = PAST_LEN + jnp.arange(x_sample.shape[1], dtype=jnp.int32)
    sample_states = (state_ret, state_mlstm_C, state_mlstm_n, state_mlstm_m, state_mlstm_conv,
                     state_rwkv, state_rwkv_shift)
    yp, ys = x_prompt, x_sample
    new_p, new_s = [], []
    for l in range(DEPTH):
        p = {'norm_mix': norm_mix[l], 'w_in': w_in[l], 'ret_gn': ret_gn[l],
             'ml_conv_w': ml_conv_w[l], 'ml_conv_b': ml_conv_b[l], 'ml_wq': ml_wq[l], 'ml_wk': ml_wk[l],
             'ml_bi': ml_bi[l], 'ml_bf': ml_bf[l], 'ml_gn': ml_gn[l], 'ml_skip': ml_skip[l],
             'rw_mu': rw_mu[l], 'rw_w0': rw_w0[l], 'rw_w2': rw_w2[l], 'rw_a0': rw_a0[l], 'rw_a2': rw_a2[l],
             'rw_g2': rw_g2[l], 'rw_kk': rw_kk[l], 'rw_ka': rw_ka[l], 'rw_rk': rw_rk[l],
             'rw_gn_w': rw_gn_w[l], 'rw_gn_b': rw_gn_b[l], 'w_branch': w_branch[l], 'w_out': w_out[l]}
        zero_states = tuple(jnp.zeros((Bp,) + s.shape[2:], s.dtype) for s in sample_states)
        yp, st_p = _mixer(yp, pos_p, zero_states, p)
        ys, st_s = _mixer(ys, pos_s, tuple(s[l] for s in sample_states), p)
        i = l // 2
        if l % 2 == 0:
            yp = yp + _swiglu(_rmsnorm(yp, norm_ffn[l]), ffn_w_gate[i], ffn_w_up[i], ffn_w_down[i])
            ys = ys + _swiglu(_rmsnorm(ys, norm_ffn[l]), ffn_w_gate[i], ffn_w_up[i], ffn_w_down[i])
        else:
            yp = yp + _moe(_rmsnorm(yp, norm_ffn[l]), moe_router[i], moe_w_gate[i], moe_w_up[i], moe_w_down[i])
            ys = ys + _moe(_rmsnorm(ys, norm_ffn[l]), moe_router[i], moe_w_gate[i], moe_w_up[i], moe_w_down[i])
        new_p.append(st_p)
        new_s.append(st_s)
    y_prompt = _rmsnorm(yp, final_norm)
    y_sample = _rmsnorm(ys, final_norm)
    ret_p, mlstm_C_p, mlstm_n_p, mlstm_m_p, mlstm_conv_p, rwkv_p, rwkv_shift_p = (
        jnp.stack([st[j] for st in new_p]) for j in range(7))
    ret_s, mlstm_C_s, mlstm_n_s, mlstm_m_s, mlstm_conv_s, rwkv_s, rwkv_shift_s = (
        jnp.stack([st[j] for st in new_s]) for j in range(7))
    return (y_prompt, y_sample, ret_p, mlstm_C_p, mlstm_n_p, mlstm_m_p, mlstm_conv_p, rwkv_p, rwkv_shift_p,
            ret_s, mlstm_C_s, mlstm_n_s, mlstm_m_s, mlstm_conv_s, rwkv_s, rwkv_shift_s)
```

```python
import functools
import math

import jax
import jax.numpy as jnp
from jax import lax
from jax.experimental import pallas as pl
from jax.experimental.pallas import tpu as pltpu

D_MODEL = 1024
BATCH = 8
SEQ = 2048
DEPTH = 2
DEC_BATCH = 128
DEC_SEQ = 4
PAST_LEN = 16384
RET_HEADS = 4
RET_DK = 64
RET_DV = 128
ML_HEADS = 4
ML_DH = 128
CONV_W = 4
RW_HEADS = 8
RW_DH = 64
RW_LORA_W = 64
RW_LORA_A = 64
RW_LORA_G = 128
RET_W = RET_HEADS * RET_DV
ML_W = ML_HEADS * ML_DH
RW_W = RW_HEADS * RW_DH
N_BRANCH = 3
RW_COLS = 3 * RW_W + RW_LORA_W + RW_LORA_A + RW_LORA_G
D_FF = 2816
N_EXPERTS = 8
D_FF_EXPERT = 1408
CHUNK = 128
NORM_EPS = 1e-6
GN_EPS = 1e-5
RW_GN_EPS = 64e-5
ROPE_BASE = 10000.0
RW_DECAY_SCALE = 0.606531

LANES = 128
S_PAD = 16
N_PROMPT = BATCH * SEQ
N_SAMPLE = DEC_BATCH * S_PAD
N_TOK = N_PROMPT + N_SAMPLE
ML_GATE_W = LANES
ML_COLS = 3 * ML_W + ML_GATE_W
RET_COLS = 2 * RET_HEADS * RET_DK + 2 * RET_W
NEG_BIG = -1e30
VMEM_LIMIT = 56 * 1024 * 1024

F32 = jnp.float32
BF16 = jnp.bfloat16
HI = lax.Precision.HIGHEST


def _cparams(sem):
    return pltpu.CompilerParams(dimension_semantics=sem, vmem_limit_bytes=VMEM_LIMIT)


def _sigmoid(x):
    return 1.0 / (1.0 + jnp.exp(-x))


def _silu(x):
    return x * _sigmoid(x)


def _rms(x, g):
    return x * lax.rsqrt(jnp.mean(x * x, axis=-1, keepdims=True) + NORM_EPS) * g


def _dot(a, b):
    return jnp.dot(a.astype(BF16), b.astype(BF16), preferred_element_type=F32)


def _dot_nt(a, b):
    return lax.dot_general(a.astype(BF16), b.astype(BF16), (((1,), (1,)), ((), ())), preferred_element_type=F32)


def _dot_tn(a, b):
    return lax.dot_general(a.astype(BF16), b.astype(BF16), (((0,), (0,)), ((), ())), preferred_element_type=F32)


def _dot_hi(a, b):
    return jnp.dot(a, b, preferred_element_type=F32, precision=HI)


def _dot_nt_hi(a, b):
    return lax.dot_general(a, b, (((1,), (1,)), ((), ())), preferred_element_type=F32, precision=HI)


def _norm_matmul_kernel(x_ref, g_ref, w_ref, o_ref):
    h = _rms(x_ref[...], g_ref[...])
    o_ref[...] = jnp.dot(h.astype(BF16), w_ref[...], preferred_element_type=F32)


def _norm_matmul(x, g, w, tm=256):
    n, d = x.shape
    c = w.shape[1]
    return pl.pallas_call(
        _norm_matmul_kernel,
        grid=(n // tm,),
        in_specs=[pl.BlockSpec((tm, d), lambda i: (i, 0)),
                  pl.BlockSpec((1, d), lambda i: (0, 0)),
                  pl.BlockSpec((d, c), lambda i: (0, 0))],
        out_specs=pl.BlockSpec((tm, c), lambda i: (i, 0)),
        out_shape=jax.ShapeDtypeStruct((n, c), F32),
        compiler_params=_cparams(("parallel",)),
        name="norm_matmul",
    )(x, g, w)


def _retention_kernel(*refs, L, nc, has_state):
    if has_state:
        (q_ref, k_ref, v_ref, g_ref, cos_ref, sin_ref, di_ref, dq_ref, dk_ref, dc_ref, gn_ref, s0_ref,
         y_ref, so_ref, s_scr) = refs
    else:
        (q_ref, k_ref, v_ref, g_ref, cos_ref, sin_ref, di_ref, dq_ref, dk_ref, dc_ref, gn_ref,
         y_ref, so_ref, s_scr) = refs
    c = pl.program_id(1)

    @pl.when(c == 0)
    def _():
        if has_state:
            s_scr[...] = s0_ref[0]
        else:
            s_scr[...] = jnp.zeros_like(s_scr)

    qk_w = RET_HEADS * RET_DK
    half = RET_DK // 2
    lane = lax.broadcasted_iota(jnp.int32, (L, qk_w), 1)
    first_half = (lane % RET_DK) < half
    cos = cos_ref[...]
    sin = sin_ref[...]

    def rot(x):
        swapped = jnp.where(first_half, pltpu.roll(x, qk_w - half, 1), pltpu.roll(x, half, 1))
        return x * cos + swapped * sin

    q = rot(q_ref[...])
    k = rot(k_ref[...]) * (RET_DK ** -0.5)
    dq = dq_ref[...]
    dk = dk_ref[...]
    dc = dc_ref[...]
    for h in range(RET_HEADS):
        qh = q[:, h * RET_DK:(h + 1) * RET_DK]
        kh = k[:, h * RET_DK:(h + 1) * RET_DK]
        vh = v_ref[:, h * RET_DV:(h + 1) * RET_DV]
        s = _dot_nt(qh, kh) * di_ref[h]
        inner = _dot(s, vh)
        s_prev = s_scr[h]
        cross = _dot(qh, s_prev) * dq[:, h:h + 1]
        o = inner + cross
        s_scr[h] = s_prev * dc[:, h:h + 1] + _dot_tn(kh * dk[:, h:h + 1], vh)
        oc = o - jnp.mean(o, axis=-1, keepdims=True)
        yn = oc * lax.rsqrt(jnp.mean(oc * oc, axis=-1, keepdims=True) + GN_EPS) * gn_ref[:, h * RET_DV:(h + 1) * RET_DV]
        y_ref[:, h * RET_DV:(h + 1) * RET_DV] = (_silu(g_ref[:, h * RET_DV:(h + 1) * RET_DV]) * yn).astype(BF16)

    @pl.when(c == nc - 1)
    def _():
        so_ref[0] = s_scr[...]


def _retention(zret, tables, gn, state, *, B, nc, L, row0):
    cos, sin, di, dq, dk, dc = tables
    has_state = state is not None
    qk_w = RET_HEADS * RET_DK
    row = lambda b, c: row0 + b * nc + c
    in_specs = [pl.BlockSpec((L, qk_w), lambda b, c: (row(b, c), 0)),
                pl.BlockSpec((L, qk_w), lambda b, c: (row(b, c), 1)),
                pl.BlockSpec((L, RET_W), lambda b, c: (row(b, c), 1)),
                pl.BlockSpec((L, RET_W), lambda b, c: (row(b, c), 2)),
                pl.BlockSpec((L, qk_w), lambda b, c: (c, 0)),
                pl.BlockSpec((L, qk_w), lambda b, c: (c, 0)),
                pl.BlockSpec((RET_HEADS, L, L), lambda b, c: (0, 0, 0)),
                pl.BlockSpec((L, RET_HEADS), lambda b, c: (0, 0)),
                pl.BlockSpec((L, RET_HEADS), lambda b, c: (0, 0)),
                pl.BlockSpec((1, RET_HEADS), lambda b, c: (0, 0)),
                pl.BlockSpec((1, RET_W), lambda b, c: (0, 0))]
    args = [zret, zret, zret, zret, cos, sin, di, dq, dk, dc, gn]
    if has_state:
        in_specs.append(pl.BlockSpec((1, RET_HEADS, RET_DK, RET_DV), lambda b, c: (b, 0, 0, 0)))
        args.append(state)
    return pl.pallas_call(
        functools.partial(_retention_kernel, L=L, nc=nc, has_state=has_state),
        grid=(B, nc),
        in_specs=in_specs,
        out_specs=[pl.BlockSpec((L, RET_W), lambda b, c: (b * nc + c, 0)),
                   pl.BlockSpec((1, RET_HEADS, RET_DK, RET_DV), lambda b, c: (b, 0, 0, 0))],
        out_shape=[jax.ShapeDtypeStruct((B * nc * L, RET_W), BF16),
                   jax.ShapeDtypeStruct((B, RET_HEADS, RET_DK, RET_DV), F32)],
        scratch_shapes=[pltpu.VMEM((RET_HEADS, RET_DK, RET_DV), F32)],
        compiler_params=_cparams(("parallel", "arbitrary")),
        name="retention",
    )(*args)


def _retention_tables(L, Lv, pos):
    half = RET_DK // 2
    inv = ROPE_BASE ** (-jnp.arange(half, dtype=F32) / half)
    ang = pos.astype(F32)[:, None] * inv[None, :]
    cos = jnp.tile(jnp.concatenate([jnp.cos(ang), jnp.cos(ang)], axis=1), (1, RET_HEADS))
    sin = jnp.tile(jnp.concatenate([-jnp.sin(ang), jnp.sin(ang)], axis=1), (1, RET_HEADS))
    log_gamma = jnp.log1p(-(2.0 ** (-5.0 - jnp.arange(RET_HEADS, dtype=F32))))
    idx = jnp.arange(L, dtype=F32)
    diff = idx[:, None] - idx[None, :]
    causal = diff >= 0
    di = jnp.where(causal[None], jnp.exp(jnp.where(causal, diff, 0.0)[None] * log_gamma[:, None, None]), 0.0)
    dq = jnp.exp((idx[:, None] + 1.0) * log_gamma[None, :])
    dk = jnp.where((idx < Lv)[:, None], jnp.exp((Lv - 1.0 - idx)[:, None] * log_gamma[None, :]), 0.0)
    dc = jnp.exp(Lv * log_gamma)[None, :]
    return cos, sin, di, dq, dk, dc


CONV_PAD = 8


def _mlstm_kernel(*refs, L, Lv, nc, has_state):
    if has_state:
        (mx_ref, mv_ref, mo_ref, gz_ref, cw_ref, cb_ref, wq_ref, wk_ref, gb_ref, gn_ref, skip_ref,
         c0_ref, n0_ref, m0_ref, buf0_ref,
         y_ref, co_ref, no_ref, mout_ref, bufo_ref, xp_scr, c_scr, n_scr, m_scr) = refs
    else:
        (mx_ref, mv_ref, mo_ref, gz_ref, cw_ref, cb_ref, wq_ref, wk_ref, gb_ref, gn_ref, skip_ref,
         y_ref, co_ref, no_ref, mout_ref, bufo_ref, xp_scr, c_scr, n_scr, m_scr) = refs
    c = pl.program_id(1)
    tail = CONV_W - 1

    @pl.when(c == 0)
    def _():
        xp_scr[0:CONV_PAD, :] = jnp.zeros((CONV_PAD, ML_W), F32)
        if has_state:
            c_scr[...] = c0_ref[0]
            n_scr[...] = n0_ref[0]
            m_scr[...] = m0_ref[0]
            xp_scr[CONV_PAD - tail:CONV_PAD, :] = buf0_ref[0]
        else:
            c_scr[...] = jnp.zeros_like(c_scr)
            n_scr[...] = jnp.zeros_like(n_scr)
            m_scr[...] = jnp.zeros_like(m_scr)

    xp_scr[CONV_PAD:CONV_PAD + L, :] = mx_ref[...]
    xc = cb_ref[...]
    for j in range(CONV_W):
        xc = xc + cw_ref[j:j + 1, :] * xp_scr[CONV_PAD - tail + j:CONV_PAD - tail + j + L, :]
    xc = _silu(xc)
    new_tail = xp_scr[CONV_PAD + Lv - tail:CONV_PAD + Lv, :]
    xp_scr[CONV_PAD - tail:CONV_PAD, :] = new_tail

    gz = gz_ref[...] + gb_ref[...]
    lane = lax.broadcasted_iota(jnp.int32, (L, ML_GATE_W), 1)
    logsig = jnp.minimum(gz, 0.0) - jnp.log1p(jnp.exp(-jnp.abs(gz)))
    gc = jnp.where(lane < ML_HEADS, gz, logsig)
    if Lv < L:
        rowi = lax.broadcasted_iota(jnp.int32, (L, ML_GATE_W), 0)
        gc = jnp.where(rowi < Lv, gc, jnp.where(lane < ML_HEADS, NEG_BIG, 0.0))
    r_i = lax.broadcasted_iota(jnp.int32, (L, L), 0)
    c_i = lax.broadcasted_iota(jnp.int32, (L, L), 1)
    tri = r_i >= c_i
    fc = _dot_hi(tri.astype(F32), gc)
    eye = (lax.broadcasted_iota(jnp.int32, (8, ML_GATE_W), 0) == lax.broadcasted_iota(jnp.int32, (8, ML_GATE_W), 1)).astype(F32)
    g_rows = _dot_nt_hi(eye, gc)
    f_rows = _dot_nt_hi(eye, fc)

    for h in range(ML_HEADS):
        sl = slice(h * ML_DH, (h + 1) * ML_DH)
        xh = xc[:, sl]
        q = _dot(xh, wq_ref[h])
        k = _dot(xh, wk_ref[h]) * (ML_DH ** -0.5)
        v = mv_ref[:, sl]
        f_col = fc[:, ML_HEADS + h:ML_HEADS + h + 1]
        i_col = gc[:, h:h + 1]
        f_row = f_rows[ML_HEADS + h:ML_HEADS + h + 1, :]
        i_row = g_rows[h:h + 1, :]
        m_prev = m_scr[0:1, h:h + 1]
        dlog = jnp.where(tri, f_col - f_row + i_row, NEG_BIG)
        inter = f_col + m_prev
        m_t = jnp.maximum(jnp.max(dlog, axis=1, keepdims=True), inter)
        w = jnp.exp(dlog - m_t)
        s = _dot_nt(q, k) * w
        a_inter = jnp.exp(inter - m_t)
        c_prev = c_scr[h]
        n_prev = n_scr[h:h + 1, :]
        num = _dot(s, v) + a_inter * _dot(q, c_prev)
        den = jnp.sum(s, axis=1, keepdims=True) + a_inter * jnp.sum(q * n_prev, axis=1, keepdims=True)
        hh = num * (1.0 / jnp.maximum(jnp.abs(den), jnp.exp(-m_t)))
        f_last = fc[Lv - 1:Lv, ML_HEADS + h:ML_HEADS + h + 1]
        m_new = jnp.maximum(f_last + m_prev, jnp.max(f_last - f_row + i_row, axis=1, keepdims=True))
        w_end = jnp.exp(f_last - f_col + i_col - m_new)
        a_old = jnp.exp(f_last + m_prev - m_new)
        kw = k * w_end
        c_scr[h] = a_old * c_prev + _dot_tn(kw, v)
        n_scr[h:h + 1, :] = a_old * n_prev + jnp.sum(kw, axis=0, keepdims=True)
        m_scr[0:1, h:h + 1] = m_new
        hc = hh - jnp.mean(hh, axis=-1, keepdims=True)
        hn = hc * lax.rsqrt(jnp.mean(hc * hc, axis=-1, keepdims=True) + GN_EPS) * gn_ref[:, sl]
        y_ref[:, sl] = (_sigmoid(mo_ref[:, sl]) * (hn + skip_ref[:, sl] * xh)).astype(BF16)

    @pl.when(c == nc - 1)
    def _():
        co_ref[0] = c_scr[...]
        no_ref[0] = n_scr[...]
        mout_ref[0] = m_scr[...]
        bufo_ref[0] = new_tail


def _mlstm(zml, weights, state, *, B, nc, L, Lv, row0):
    cw, cb, wq, wk, gb, gn, skip = weights
    has_state = state is not None
    row = lambda b, c: row0 + b * nc + c
    const2 = lambda b, c: (0, 0)
    in_specs = [pl.BlockSpec((L, ML_W), lambda b, c: (row(b, c), 0)),
                pl.BlockSpec((L, ML_W), lambda b, c: (row(b, c), 1)),
                pl.BlockSpec((L, ML_W), lambda b, c: (row(b, c), 2)),
                pl.BlockSpec((L, ML_GATE_W), lambda b, c: (row(b, c), 3 * ML_W // ML_GATE_W)),
                pl.BlockSpec((CONV_W, ML_W), const2),
                pl.BlockSpec((1, ML_W), const2),
                pl.BlockSpec((ML_HEADS, ML_DH, ML_DH), lambda b, c: (0, 0, 0)),
                pl.BlockSpec((ML_HEADS, ML_DH, ML_DH), lambda b, c: (0, 0, 0)),
                pl.BlockSpec((1, ML_GATE_W), const2),
                pl.BlockSpec((1, ML_W), const2),
                pl.BlockSpec((1, ML_W), const2)]
    args = [zml, zml, zml, zml, cw, cb, wq, wk, gb, gn, skip]
    st_specs = [pl.BlockSpec((1, ML_HEADS, ML_DH, ML_DH), lambda b, c: (b, 0, 0, 0)),
                pl.BlockSpec((1, ML_HEADS, ML_DH), lambda b, c: (b, 0, 0)),
                pl.BlockSpec((1, 1, LANES), lambda b, c: (b, 0, 0)),
                pl.BlockSpec((1, CONV_W - 1, ML_W), lambda b, c: (b, 0, 0))]
    if has_state:
        in_specs += st_specs
        args += list(state)
    return pl.pallas_call(
        functools.partial(_mlstm_kernel, L=L, Lv=Lv, nc=nc, has_state=has_state),
        grid=(B, nc),
        in_specs=in_specs,
        out_specs=[pl.BlockSpec((L, ML_W), lambda b, c: (b * nc + c, 0))] + st_specs,
        out_shape=[jax.ShapeDtypeStruct((B * nc * L, ML_W), BF16),
                   jax.ShapeDtypeStruct((B, ML_HEADS, ML_DH, ML_DH), F32),
                   jax.ShapeDtypeStruct((B, ML_HEADS, ML_DH), F32),
                   jax.ShapeDtypeStruct((B, 1, LANES), F32),
                   jax.ShapeDtypeStruct((B, CONV_W - 1, ML_W), F32)],
        scratch_shapes=[pltpu.VMEM((CONV_PAD + L, ML_W), F32),
                        pltpu.VMEM((ML_HEADS, ML_DH, ML_DH), F32),
                        pltpu.VMEM((ML_HEADS, ML_DH), F32),
                        pltpu.VMEM((1, LANES), F32)],
        compiler_params=_cparams(("parallel", "arbitrary")),
        name="mlstm",
    )(*args)


def _rwkv_prep_kernel(*refs, L, Lv, nc, has_state):
    if has_state:
        (z_ref, mu_ref, wa0_ref, w2a_ref, g2_ref, kkp_ref, ka_ref, rk_ref, bd_ref, prev_ref,
         r_ref, w_ref, k_ref, v_ref, kk_ref, kka_ref, g_ref, bonus_ref, shift_ref, xs_scr) = refs
    else:
        (z_ref, mu_ref, wa0_ref, w2a_ref, g2_ref, kkp_ref, ka_ref, rk_ref, bd_ref,
         r_ref, w_ref, k_ref, v_ref, kk_ref, kka_ref, g_ref, bonus_ref, shift_ref, xs_scr) = refs
    c = pl.program_id(1)

    @pl.when(c == 0)
    def _():
        xs_scr[0:CONV_PAD, :] = jnp.zeros((CONV_PAD, RW_COLS), F32)
        if has_state:
            xs_scr[CONV_PAD - 1:CONV_PAD, :] = prev_ref[0]

    z = z_ref[...]
    xs_scr[CONV_PAD:CONV_PAD + L, :] = z
    zp = xs_scr[CONV_PAD - 1:CONV_PAD - 1 + L, :]
    zs = z + (zp - z) * mu_ref[...]
    last = z[Lv - 1:Lv, :]
    xs_scr[CONV_PAD - 1:CONV_PAD, :] = last

    @pl.when(c == nc - 1)
    def _():
        shift_ref[0] = last

    r = zs[:, 0:RW_W]
    kr = zs[:, RW_W:2 * RW_W]
    vr = zs[:, 2 * RW_W:3 * RW_W]
    wa = zs[:, 3 * RW_W:3 * RW_W + RW_LORA_W + RW_LORA_A]
    gl = zs[:, 3 * RW_W + RW_LORA_W + RW_LORA_A:]
    lane = lax.broadcasted_iota(jnp.int32, wa.shape, 1)
    wa_in = jnp.where(lane < RW_LORA_W, jnp.tanh(wa), wa)
    lora = _dot_hi(wa_in, w2a_ref[...]) + wa0_ref[...]
    w = jnp.exp(-RW_DECAY_SCALE * _sigmoid(lora[:, 0:RW_W]))
    a = _sigmoid(lora[:, RW_W:])
    g = _dot_hi(_sigmoid(gl), g2_ref[...])
    bd = bd_ref[...]
    kk = kr * kkp_ref[...]
    kk = kk / jnp.maximum(jnp.sqrt(_dot_hi(kk * kk, bd)), 1e-12)
    k2 = kr * (1.0 + (a - 1.0) * ka_ref[...])
    bonus = _dot_hi(r * k2 * rk_ref[...], bd) * vr
    r_ref[...] = r
    w_ref[...] = w
    k_ref[...] = k2
    v_ref[...] = vr
    kk_ref[...] = kk
    kka_ref[...] = kk * a
    g_ref[...] = g
    bonus_ref[...] = bonus


def _rwkv_prep(zrw, weights, prev, *, B, nc, L, Lv, row0):
    mu, wa0, w2a, g2, kkp, ka, rk, bd = weights
    has_state = prev is not None
    const2 = lambda b, c: (0, 0)
    in_specs = [pl.BlockSpec((L, RW_COLS), lambda b, c: (row0 + b * nc + c, 0)),
                pl.BlockSpec((1, RW_COLS), const2),
                pl.BlockSpec((1, 2 * RW_W), const2),
                pl.BlockSpec((RW_LORA_W + RW_LORA_A, 2 * RW_W), const2),
                pl.BlockSpec((RW_LORA_G, RW_W), const2),
                pl.BlockSpec((1, RW_W), const2),
                pl.BlockSpec((1, RW_W), const2),
                pl.BlockSpec((1, RW_W), const2),
                pl.BlockSpec((RW_W, RW_W), const2)]
    args = [zrw, mu, wa0, w2a, g2, kkp, ka, rk, bd]
    if has_state:
        in_specs.append(pl.BlockSpec((1, 1, RW_COLS), lambda b, c: (b, 0, 0)))
        args.append(prev)
    tok = pl.BlockSpec((L, RW_W), lambda b, c: (b * nc + c, 0))
    tok_shape = jax.ShapeDtypeStruct((B * nc * L, RW_W), F32)
    return pl.pallas_call(
        functools.partial(_rwkv_prep_kernel, L=L, Lv=Lv, nc=nc, has_state=has_state),
        grid=(B, nc),
        in_specs=in_specs,
        out_specs=[tok] * 8 + [pl.BlockSpec((1, 1, RW_COLS), lambda b, c: (b, 0, 0))],
        out_shape=[tok_shape] * 8 + [jax.ShapeDtypeStruct((B, 1, RW_COLS), F32)],
        scratch_shapes=[pltpu.VMEM((CONV_PAD + L, RW_COLS), F32)],
        compiler_params=_cparams(("parallel", "arbitrary")),
        name="rwkv_prep",
    )(*args)


RW_NB = 8
RW_TC = 16
RW_PAIRS = RW_HEADS // 2


def _rwkv_scan_kernel(*refs, Lv, nc, has_state):
    if has_state:
        (r_ref, w_ref, k_ref, v_ref, kk_ref, kka_ref, g_ref, bonus_ref, gnw_ref, gnb_ref, bd_ref, s0_ref,
         y_ref, so_ref, s_scr, vt_scr, ot_scr) = refs
    else:
        (r_ref, w_ref, k_ref, v_ref, kk_ref, kka_ref, g_ref, bonus_ref, gnw_ref, gnb_ref, bd_ref,
         y_ref, so_ref, s_scr, vt_scr, ot_scr) = refs
    c = pl.program_id(1)
    rows = RW_NB * RW_TC

    @pl.when(c == 0)
    def _():
        if has_state:
            for b in range(RW_NB):
                for j in range(RW_PAIRS):
                    s_scr[b * RW_PAIRS + j] = jnp.concatenate([s0_ref[b, 2 * j], s0_ref[b, 2 * j + 1]], axis=1)
        else:
            s_scr[...] = jnp.zeros_like(s_scr)

    vt = v_ref[...].reshape(rows, RW_W).T
    for b in range(RW_NB):
        vt_scr[b] = pltpu.roll(vt, (rows - b * RW_TC) % rows, 1) if b else vt
    ot_scr[...] = jnp.zeros_like(ot_scr)

    lane = lax.broadcasted_iota(jnp.int32, (RW_DH, LANES), 1)
    left = lane < RW_DH
    lane2 = lax.broadcasted_iota(jnp.int32, (2 * RW_DH, LANES), 1)

    def seq_body(b, carry):
        for t in range(Lv):
            for j in range(RW_PAIRS):
                cols = slice(j * LANES, (j + 1) * LANES)
                idx = b * RW_PAIRS + j
                s = s_scr[idx]
                kk_row = kk_ref[b, t:t + 1, cols]
                w_row = w_ref[b, t:t + 1, cols]
                kka_row = kka_ref[b, t:t + 1, cols]
                k_row = k_ref[b, t:t + 1, cols]
                r_row = r_ref[b, t:t + 1, cols]
                v_l = vt_scr[b, 2 * j * RW_DH:(2 * j + 1) * RW_DH, t:t + 1]
                v_r = vt_scr[b, (2 * j + 1) * RW_DH:(2 * j + 2) * RW_DH, t:t + 1]
                t1 = s * kk_row
                skk = jnp.where(left, jnp.sum(jnp.where(left, t1, 0.0), axis=1, keepdims=True),
                                jnp.sum(jnp.where(left, 0.0, t1), axis=1, keepdims=True))
                s = s * w_row - skk * kka_row + jnp.where(left, v_l, v_r) * k_row
                s_scr[idx] = s
                t2 = s * r_row
                o_col = jnp.concatenate([jnp.sum(jnp.where(left, t2, 0.0), axis=1, keepdims=True),
                                         jnp.sum(jnp.where(left, 0.0, t2), axis=1, keepdims=True)], axis=0)
                blk = slice(j * 2 * RW_DH, (j + 1) * 2 * RW_DH)
                ot_scr[b, blk, :] = jnp.where(lane2 == t, o_col, ot_scr[b, blk, :])
        return carry

    lax.fori_loop(0, RW_NB, seq_body, 0)

    lane_w = lax.broadcasted_iota(jnp.int32, (RW_W, rows), 1)
    ot = jnp.zeros((RW_W, rows), F32)
    for b in range(RW_NB):
        shifted = pltpu.roll(ot_scr[b], b * RW_TC, 1) if b else ot_scr[b]
        ot = jnp.where(lane_w // RW_TC == b, shifted, ot)
    o = ot.T
    bd = bd_ref[...]
    oc = o - _dot_hi(o, bd) * (1.0 / RW_DH)
    var = _dot_hi(oc * oc, bd) * (1.0 / RW_DH)
    yn = oc * lax.rsqrt(var + RW_GN_EPS) * gnw_ref[...] + gnb_ref[...]
    y = (yn + bonus_ref[...].reshape(rows, RW_W)) * g_ref[...].reshape(rows, RW_W)
    y_ref[...] = y.astype(BF16).reshape(RW_NB, RW_TC, RW_W)

    @pl.when(c == nc - 1)
    def _():
        for b in range(RW_NB):
            for j in range(RW_PAIRS):
                tile = s_scr[b * RW_PAIRS + j]
                so_ref[b, 2 * j] = tile[:, :RW_DH]
                so_ref[b, 2 * j + 1] = tile[:, RW_DH:]


def _rwkv_scan(toks, gnw, gnb, bd, state, *, B, T, Lv):
    has_state = state is not None
    nc = T // RW_TC
    tok = pl.BlockSpec((RW_NB, RW_TC, RW_W), lambda i, c: (i, c, 0))
    const2 = lambda i, c: (0, 0)
    st = pl.BlockSpec((RW_NB, RW_HEADS, RW_DH, RW_DH), lambda i, c: (i, 0, 0, 0))
    in_specs = [tok] * 8 + [pl.BlockSpec((1, RW_W), const2), pl.BlockSpec((1, RW_W), const2),
                            pl.BlockSpec((RW_W, RW_W), const2)]
    args = [t.reshape(B, T, RW_W) for t in toks] + [gnw, gnb, bd]
    if has_state:
        in_specs.append(st)
        args.append(state)
    return pl.pallas_call(
        functools.partial(_rwkv_scan_kernel, Lv=Lv, nc=nc, has_state=has_state),
        grid=(B // RW_NB, nc),
        in_specs=in_specs,
        out_specs=[tok, st],
        out_shape=[jax.ShapeDtypeStruct((B, T, RW_W), BF16),
                   jax.ShapeDtypeStruct((B, RW_HEADS, RW_DH, RW_DH), F32)],
        scratch_shapes=[pltpu.VMEM((RW_NB * RW_PAIRS, RW_DH, LANES), F32),
                        pltpu.VMEM((RW_NB, RW_W, RW_NB * RW_TC), F32),
                        pltpu.VMEM((RW_NB, RW_W, RW_NB * RW_TC), F32)],
        compiler_params=_cparams(("parallel", "arbitrary")),
        name="rwkv_scan",
    )(*args)


def _merge_kernel(x_ref, zg_ref, yr_ref, ym_ref, yw_ref, wb_ref, wo_ref, o_ref):
    acc = None
    for n, y_ref in enumerate((yr_ref, ym_ref, yw_ref)):
        proj = jnp.dot(y_ref[...], wb_ref[n], preferred_element_type=F32)
        term = _sigmoid(zg_ref[:, n * D_MODEL:(n + 1) * D_MODEL]) * proj
        acc = term if acc is None else acc + term
    o_ref[...] = x_ref[...] + jnp.dot(acc.astype(BF16), wo_ref[...], preferred_element_type=F32)


def _merge(x, zgate, y_ret, y_ml, y_rw, w_branch, w_out, tm=256):
    n = x.shape[0]
    tokspec = lambda w: pl.BlockSpec((tm, w), lambda i: (i, 0))
    return pl.pallas_call(
        _merge_kernel,
        grid=(n // tm,),
        in_specs=[tokspec(D_MODEL), tokspec(N_BRANCH * D_MODEL), tokspec(RET_W), tokspec(ML_W), tokspec(RW_W),
                  pl.BlockSpec((N_BRANCH, RET_W, D_MODEL), lambda i: (0, 0, 0)),
                  pl.BlockSpec((D_MODEL, D_MODEL), lambda i: (0, 0))],
        out_specs=tokspec(D_MODEL),
        out_shape=jax.ShapeDtypeStruct((n, D_MODEL), F32),
        compiler_params=_cparams(("parallel",)),
        name="merge",
    )(x, zgate, y_ret, y_ml, y_rw, w_branch, w_out)


def _ffn_kernel(*refs, ne, routed, final):
    refs = list(refs)
    x_ref, g_ref = refs[:2]
    refs = refs[2:]
    router_ref = refs.pop(0) if routed else None
    wg_ref, wu_ref, wd_ref = refs[:3]
    refs = refs[3:]
    fin_ref = refs.pop(0) if final else None
    o_ref, h_scr, acc_scr, comb_scr = refs
    e = pl.program_id(1)

    @pl.when(e == 0)
    def _():
        h = _rms(x_ref[...], g_ref[...])
        h_scr[...] = h.astype(BF16)
        acc_scr[...] = jnp.zeros_like(acc_scr)
        if routed:
            lane = lax.broadcasted_iota(jnp.int32, (h.shape[0], LANES), 1)
            logits = jnp.where(lane < ne, _dot_hi(h, router_ref[...]), NEG_BIG)
            m1 = jnp.max(logits, axis=1, keepdims=True)
            i1 = jnp.min(jnp.where(logits == m1, lane, LANES), axis=1, keepdims=True)
            rest = jnp.where(lane == i1, NEG_BIG, logits)
            m2 = jnp.max(rest, axis=1, keepdims=True)
            i2 = jnp.min(jnp.where(rest == m2, lane, LANES), axis=1, keepdims=True)
            e2 = jnp.exp(m2 - m1)
            p1 = 1.0 / (1.0 + e2)
            comb_scr[...] = jnp.where(lane == i1, p1, 0.0) + jnp.where(lane == i2, e2 * p1, 0.0)

    h = h_scr[...]
    hg = jnp.dot(h, wg_ref[0], preferred_element_type=F32)
    hu = jnp.dot(h, wu_ref[0], preferred_element_type=F32)
    y = jnp.dot((_silu(hg) * hu).astype(BF16), wd_ref[0], preferred_element_type=F32)
    if routed:
        lane = lax.broadcasted_iota(jnp.int32, comb_scr.shape, 1)
        y = y * jnp.sum(jnp.where(lane == e, comb_scr[...], 0.0), axis=1, keepdims=True)
    acc_scr[...] += y

    @pl.when(e == ne - 1)
    def _():
        out = x_ref[...] + acc_scr[...]
        if final:
            out = _rms(out, fin_ref[...])
        o_ref[...] = out


def _ffn(x, g, wg, wu, wd, router=None, fin=None, tm=512):
    n = x.shape[0]
    ne, _, f = wg.shape
    routed = router is not None
    final = fin is not None
    in_specs = [pl.BlockSpec((tm, D_MODEL), lambda i, e: (i, 0)),
                pl.BlockSpec((1, D_MODEL), lambda i, e: (0, 0))]
    args = [x, g]
    if routed:
        in_specs.append(pl.BlockSpec((D_MODEL, LANES), lambda i, e: (0, 0)))
        args.append(router)
    in_specs += [pl.BlockSpec((1, D_MODEL, f), lambda i, e: (e, 0, 0)),
                 pl.BlockSpec((1, D_MODEL, f), lambda i, e: (e, 0, 0)),
                 pl.BlockSpec((1, f, D_MODEL), lambda i, e: (e, 0, 0))]
    args += [wg, wu, wd]
    if final:
        in_specs.append(pl.BlockSpec((1, D_MODEL), lambda i, e: (0, 0)))
        args.append(fin)
    return pl.pallas_call(
        functools.partial(_ffn_kernel, ne=ne, routed=routed, final=final),
        grid=(n // tm, ne),
        in_specs=in_specs,
        out_specs=pl.BlockSpec((tm, D_MODEL), lambda i, e: (i, 0)),
        out_shape=jax.ShapeDtypeStruct((n, D_MODEL), F32),
        scratch_shapes=[pltpu.VMEM((tm, D_MODEL), BF16), pltpu.VMEM((tm, D_MODEL), F32),
                        pltpu.VMEM((tm, LANES), F32)],
        compiler_params=_cparams(("parallel", "arbitrary")),
        name="ffn",
    )(*args)


def _row(v):
    return v.reshape(1, -1)


def kernel(x_prompt, x_sample, state_ret, state_mlstm_C, state_mlstm_n, state_mlstm_m, state_mlstm_conv, state_rwkv, state_rwkv_shift, norm_mix, w_in, ret_gn, ml_conv_w, ml_conv_b, ml_wq, ml_wk, ml_bi, ml_bf, ml_gn, ml_skip, rw_mu, rw_w0, rw_w2, rw_a0, rw_a2, rw_g2, rw_kk, rw_ka, rw_rk, rw_gn_w, rw_gn_b, w_branch, w_out, norm_ffn, ffn_w_gate, ffn_w_up, ffn_w_down, moe_router, moe_w_gate, moe_w_up, moe_w_down, final_norm):
    nc_p = SEQ // CHUNK
    xs = jnp.pad(x_sample, ((0, 0), (0, S_PAD - DEC_SEQ), (0, 0)))
    x = jnp.concatenate([x_prompt.reshape(N_PROMPT, D_MODEL), xs.reshape(N_SAMPLE, D_MODEL)], axis=0)

    pos_p = jnp.arange(SEQ, dtype=jnp.int32)
    pos_s = PAST_LEN + jnp.arange(S_PAD, dtype=jnp.int32)
    ret_tab_p = _retention_tables(CHUNK, CHUNK, pos_p)
    ret_tab_s = _retention_tables(S_PAD, DEC_SEQ, pos_s)
    head_of = jnp.arange(RW_W) // RW_DH
    bd64 = (head_of[:, None] == head_of[None, :]).astype(F32)

    o_rq = 0
    o_mx = o_rq + RET_COLS
    o_mi = o_mx + 2 * ML_W
    o_mo = o_mi + 2 * ML_HEADS
    o_rw = o_mo + ML_W
    o_gate = o_rw + RW_COLS

    new_p, new_s = [], []
    for l in range(DEPTH):
        w = w_in[l]
        w_ret = w[:, o_rq:o_mx].astype(BF16)
        w_ml = jnp.concatenate([w[:, o_mx:o_mi], w[:, o_mo:o_rw], w[:, o_mi:o_mo],
                                jnp.zeros((D_MODEL, ML_GATE_W - 2 * ML_HEADS), F32)], axis=1).astype(BF16)
        w_rw = w[:, o_rw:o_gate].astype(BF16)
        w_gate = w[:, o_gate:].astype(BF16)
        g_mix = _row(norm_mix[l])
        zret = _norm_matmul(x, g_mix, w_ret)
        zml = _norm_matmul(x, g_mix, w_ml)
        zrw = _norm_matmul(x, g_mix, w_rw)
        zgate = _norm_matmul(x, g_mix, w_gate)

        gn = _row(ret_gn[l])
        yr_p, ret_p = _retention(zret, ret_tab_p, gn, None, B=BATCH, nc=nc_p, L=CHUNK, row0=0)
        yr_s, ret_s = _retention(zret, ret_tab_s, gn, state_ret[l], B=DEC_BATCH, nc=1, L=S_PAD,
                                 row0=N_PROMPT // S_PAD)

        gate_bias = jnp.concatenate([ml_bi[l], ml_bf[l], jnp.zeros((ML_GATE_W - 2 * ML_HEADS,), F32)])
        ml_weights = (ml_conv_w[l], _row(ml_conv_b[l]), ml_wq[l].astype(BF16), ml_wk[l].astype(BF16),
                      _row(gate_bias), _row(ml_gn[l]), _row(ml_skip[l]))
        m0 = jnp.pad(state_mlstm_m[l], ((0, 0), (0, LANES - ML_HEADS))).reshape(DEC_BATCH, 1, LANES)
        ym_p, c_p, n_p, m_p, buf_p = _mlstm(zml, ml_weights, None, B=BATCH, nc=nc_p, L=CHUNK, Lv=CHUNK, row0=0)
        ym_s, c_s, n_s, m_s, buf_s = _mlstm(zml, ml_weights,
                                            (state_mlstm_C[l], state_mlstm_n[l], m0, state_mlstm_conv[l]),
                                            B=DEC_BATCH, nc=1, L=S_PAD, Lv=DEC_SEQ, row0=N_PROMPT // S_PAD)

        w2a = jnp.zeros((RW_LORA_W + RW_LORA_A, 2 * RW_W), F32)
        w2a = w2a.at[:RW_LORA_W, :RW_W].set(rw_w2[l]).at[RW_LORA_W:, RW_W:].set(rw_a2[l])
        rw_weights = (_row(rw_mu[l]), _row(jnp.concatenate([rw_w0[l], rw_a0[l]])), w2a, rw_g2[l],
                      _row(rw_kk[l]), _row(rw_ka[l]), _row(rw_rk[l]), bd64)
        *tok_p, shift_p = _rwkv_prep(zrw, rw_weights, None, B=BATCH, nc=nc_p, L=CHUNK, Lv=CHUNK, row0=0)
        *tok_s, shift_s = _rwkv_prep(zrw, rw_weights, state_rwkv_shift[l].reshape(DEC_BATCH, 1, RW_COLS),
                                     B=DEC_BATCH, nc=1, L=S_PAD, Lv=DEC_SEQ, row0=N_PROMPT // S_PAD)
        gnw, gnb = _row(rw_gn_w[l]), _row(rw_gn_b[l])
        yw_p, rws_p = _rwkv_scan(tok_p, gnw, gnb, bd64, None, B=BATCH, T=SEQ, Lv=RW_TC)
        yw_s, rws_s = _rwkv_scan(tok_s, gnw, gnb, bd64, state_rwkv[l], B=DEC_BATCH, T=S_PAD, Lv=DEC_SEQ)

        y_ret = jnp.concatenate([yr_p, yr_s], axis=0)
        y_ml = jnp.concatenate([ym_p, ym_s], axis=0)
        y_rw = jnp.concatenate([yw_p.reshape(N_PROMPT, RW_W), yw_s.reshape(N_SAMPLE, RW_W)], axis=0)
        x = _merge(x, zgate, y_ret, y_ml, y_rw, w_branch[l].astype(BF16), w_out[l].astype(BF16))

        i = l // 2
        g_ffn = _row(norm_ffn[l])
        fin = _row(final_norm) if l == DEPTH - 1 else None
        if l % 2 == 0:
            halves = D_FF // D_FF_EXPERT
            wg = ffn_w_gate[i].reshape(D_MODEL, halves, D_FF_EXPERT).transpose(1, 0, 2).astype(BF16)
            wu = ffn_w_up[i].reshape(D_MODEL, halves, D_FF_EXPERT).transpose(1, 0, 2).astype(BF16)
            wd = ffn_w_down[i].reshape(halves, D_FF_EXPERT, D_MODEL).astype(BF16)
            x = _ffn(x, g_ffn, wg, wu, wd, None, fin)
        else:
            router = jnp.pad(moe_router[i], ((0, 0), (0, LANES - N_EXPERTS)))
            x = _ffn(x, g_ffn, moe_w_gate[i].astype(BF16), moe_w_up[i].astype(BF16), moe_w_down[i].astype(BF16),
                     router, fin)

        new_p.append((ret_p, c_p, n_p, m_p[:, 0, :ML_HEADS], buf_p, rws_p, shift_p[:, 0]))
        new_s.append((ret_s, c_s, n_s, m_s[:, 0, :ML_HEADS], buf_s, rws_s, shift_s[:, 0]))

    y_prompt = x[:N_PROMPT].reshape(BATCH, SEQ, D_MODEL)
    y_sample = x[N_PROMPT:].reshape(DEC_BATCH, S_PAD, D_MODEL)[:, :DEC_SEQ]
    st_p = tuple(jnp.stack([st[j] for st in new_p]) for j in range(7))
    st_s = tuple(jnp.stack([st[j] for st in new_s]) for j in range(7))
    return (y_prompt, y_sample) + st_p + st_s
```

```python
import functools
import math

import jax
import jax.numpy as jnp
from jax import lax
from jax.experimental import pallas as pl
from jax.experimental.pallas import tpu as pltpu

D_MODEL = 1024
BATCH = 8
SEQ = 2048
DEPTH = 2
DEC_BATCH = 128
DEC_SEQ = 4
PAST_LEN = 16384
RET_HEADS = 4
RET_DK = 64
RET_DV = 128
ML_HEADS = 4
ML_DH = 128
CONV_W = 4
RW_HEADS = 8
RW_DH = 64
RW_LORA_W = 64
RW_LORA_A = 64
RW_LORA_G = 128
RET_W = RET_HEADS * RET_DV
ML_W = ML_HEADS * ML_DH
RW_W = RW_HEADS * RW_DH
N_BRANCH = 3
RW_COLS = 3 * RW_W + RW_LORA_W + RW_LORA_A + RW_LORA_G
D_FF = 2816
N_EXPERTS = 8
D_FF_EXPERT = 1408
CHUNK = 128
NORM_EPS = 1e-6
GN_EPS = 1e-5
RW_GN_EPS = 64e-5
ROPE_BASE = 10000.0
RW_DECAY_SCALE = 0.606531

LANES = 128
S_PAD = 16
N_PROMPT = BATCH * SEQ
N_SAMPLE = DEC_BATCH * S_PAD
N_TOK = N_PROMPT + N_SAMPLE
ML_GATE_W = LANES
ML_COLS = 3 * ML_W + ML_GATE_W
RET_COLS = 2 * RET_HEADS * RET_DK + 2 * RET_W
NEG_BIG = -1e30
VMEM_LIMIT = 56 * 1024 * 1024

F32 = jnp.float32
BF16 = jnp.bfloat16
HI = lax.Precision.HIGHEST


def _cparams(sem):
    return pltpu.CompilerParams(dimension_semantics=sem, vmem_limit_bytes=VMEM_LIMIT)


def _sigmoid(x):
    return 1.0 / (1.0 + jnp.exp(-x))


def _silu(x):
    return x * _sigmoid(x)


def _rms(x, g):
    return x * lax.rsqrt(jnp.mean(x * x, axis=-1, keepdims=True) + NORM_EPS) * g


def _dot(a, b):
    return jnp.dot(a.astype(BF16), b.astype(BF16), preferred_element_type=F32)


def _dot_nt(a, b):
    return lax.dot_general(a.astype(BF16), b.astype(BF16), (((1,), (1,)), ((), ())), preferred_element_type=F32)


def _dot_tn(a, b):
    return lax.dot_general(a.astype(BF16), b.astype(BF16), (((0,), (0,)), ((), ())), preferred_element_type=F32)


def _dot_hi(a, b):
    return jnp.dot(a, b, preferred_element_type=F32, precision=HI)


def _dot_nt_hi(a, b):
    return lax.dot_general(a, b, (((1,), (1,)), ((), ())), preferred_element_type=F32, precision=HI)


def _norm_matmul_kernel(x_ref, g_ref, w_ref, o_ref):
    h = _rms(x_ref[...], g_ref[...])
    o_ref[...] = jnp.dot(h.astype(BF16), w_ref[...], preferred_element_type=F32)


def _norm_matmul(x, g, w, tm=256):
    n, d = x.shape
    c = w.shape[1]
    return pl.pallas_call(
        _norm_matmul_kernel,
        grid=(n // tm,),
        in_specs=[pl.BlockSpec((tm, d), lambda i: (i, 0)),
                  pl.BlockSpec((1, d), lambda i: (0, 0)),
                  pl.BlockSpec((d, c), lambda i: (0, 0))],
        out_specs=pl.BlockSpec((tm, c), lambda i: (i, 0)),
        out_shape=jax.ShapeDtypeStruct((n, c), F32),
        compiler_params=_cparams(("parallel",)),
        name="norm_matmul",
    )(x, g, w)


def _retention_kernel(*refs, L, nc, has_state):
    if has_state:
        (q_ref, k_ref, v_ref, g_ref, cos_ref, sin_ref, di_ref, dq_ref, dk_ref, dc_ref, gn_ref, s0_ref,
         y_ref, so_ref, s_scr) = refs
    else:
        (q_ref, k_ref, v_ref, g_ref, cos_ref, sin_ref, di_ref, dq_ref, dk_ref, dc_ref, gn_ref,
         y_ref, so_ref, s_scr) = refs
    c = pl.program_id(1)

    @pl.when(c == 0)
    def _():
        if has_state:
            s_scr[...] = s0_ref[0]
        else:
            s_scr[...] = jnp.zeros_like(s_scr)

    qk_w = RET_HEADS * RET_DK
    half = RET_DK // 2
    lane = lax.broadcasted_iota(jnp.int32, (L, qk_w), 1)
    first_half = (lane % RET_DK) < half
    cos = cos_ref[...]
    sin = sin_ref[...]

    def rot(x):
        swapped = jnp.where(first_half, pltpu.roll(x, qk_w - half, 1), pltpu.roll(x, half, 1))
        return x * cos + swapped * sin

    q = rot(q_ref[...])
    k = rot(k_ref[...]) * (RET_DK ** -0.5)
    dq = dq_ref[...]
    dk = dk_ref[...]
    dc = dc_ref[...]
    for h in range(RET_HEADS):
        qh = q[:, h * RET_DK:(h + 1) * RET_DK]
        kh = k[:, h * RET_DK:(h + 1) * RET_DK]
        vh = v_ref[:, h * RET_DV:(h + 1) * RET_DV]
        s = _dot_nt(qh, kh) * di_ref[h]
        inner = _dot(s, vh)
        s_prev = s_scr[h]
        cross = _dot(qh, s_prev) * dq[:, h:h + 1]
        o = inner + cross
        s_scr[h] = s_prev * dc[:, h:h + 1] + _dot_tn(kh * dk[:, h:h + 1], vh)
        oc = o - jnp.mean(o, axis=-1, keepdims=True)
        yn = oc * lax.rsqrt(jnp.mean(oc * oc, axis=-1, keepdims=True) + GN_EPS) * gn_ref[:, h * RET_DV:(h + 1) * RET_DV]
        y_ref[:, h * RET_DV:(h + 1) * RET_DV] = (_silu(g_ref[:, h * RET_DV:(h + 1) * RET_DV]) * yn).astype(BF16)

    @pl.when(c == nc - 1)
    def _():
        so_ref[0] = s_scr[...]


def _retention(zret, tables, gn, state, *, B, nc, L, row0):
    cos, sin, di, dq, dk, dc = tables
    has_state = state is not None
    qk_w = RET_HEADS * RET_DK
    row = lambda b, c: row0 + b * nc + c
    in_specs = [pl.BlockSpec((L, qk_w), lambda b, c: (row(b, c), 0)),
                pl.BlockSpec((L, qk_w), lambda b, c: (row(b, c), 1)),
                pl.BlockSpec((L, RET_W), lambda b, c: (row(b, c), 1)),
                pl.BlockSpec((L, RET_W), lambda b, c: (row(b, c), 2)),
                pl.BlockSpec((L, qk_w), lambda b, c: (c, 0)),
                pl.BlockSpec((L, qk_w), lambda b, c: (c, 0)),
                pl.BlockSpec((RET_HEADS, L, L), lambda b, c: (0, 0, 0)),
                pl.BlockSpec((L, RET_HEADS), lambda b, c: (0, 0)),
                pl.BlockSpec((L, RET_HEADS), lambda b, c: (0, 0)),
                pl.BlockSpec((1, RET_HEADS), lambda b, c: (0, 0)),
                pl.BlockSpec((1, RET_W), lambda b, c: (0, 0))]
    args = [zret, zret, zret, zret, cos, sin, di, dq, dk, dc, gn]
    if has_state:
        in_specs.append(pl.BlockSpec((1, RET_HEADS, RET_DK, RET_DV), lambda b, c: (b, 0, 0, 0)))
        args.append(state)
    return pl.pallas_call(
        functools.partial(_retention_kernel, L=L, nc=nc, has_state=has_state),
        grid=(B, nc),
        in_specs=in_specs,
        out_specs=[pl.BlockSpec((L, RET_W), lambda b, c: (b * nc + c, 0)),
                   pl.BlockSpec((1, RET_HEADS, RET_DK, RET_DV), lambda b, c: (b, 0, 0, 0))],
        out_shape=[jax.ShapeDtypeStruct((B * nc * L, RET_W), BF16),
                   jax.ShapeDtypeStruct((B, RET_HEADS, RET_DK, RET_DV), F32)],
        scratch_shapes=[pltpu.VMEM((RET_HEADS, RET_DK, RET_DV), F32)],
        compiler_params=_cparams(("parallel", "arbitrary")),
        name="retention",
    )(*args)


def _retention_tables(L, Lv, pos):
    half = RET_DK // 2
    inv = ROPE_BASE ** (-jnp.arange(half, dtype=F32) / half)
    ang = pos.astype(F32)[:, None] * inv[None, :]
    cos = jnp.tile(jnp.concatenate([jnp.cos(ang), jnp.cos(ang)], axis=1), (1, RET_HEADS))
    sin = jnp.tile(jnp.concatenate([-jnp.sin(ang), jnp.sin(ang)], axis=1), (1, RET_HEADS))
    log_gamma = jnp.log1p(-(2.0 ** (-5.0 - jnp.arange(RET_HEADS, dtype=F32))))
    idx = jnp.arange(L, dtype=F32)
    diff = idx[:, None] - idx[None, :]
    causal = diff >= 0
    di = jnp.where(causal[None], jnp.exp(jnp.where(causal, diff, 0.0)[None] * log_gamma[:, None, None]), 0.0)
    dq = jnp.exp((idx[:, None] + 1.0) * log_gamma[None, :])
    dk = jnp.where((idx < Lv)[:, None], jnp.exp((Lv - 1.0 - idx)[:, None] * log_gamma[None, :]), 0.0)
    dc = jnp.exp(Lv * log_gamma)[None, :]
    return cos, sin, di, dq, dk, dc


CONV_PAD = 8


def _mlstm_kernel(*refs, L, Lv, nc, has_state):
    if has_state:
        (mx_ref, mv_ref, mo_ref, gz_ref, cw_ref, cb_ref, wq_ref, wk_ref, gb_ref, gn_ref, skip_ref,
         c0_ref, n0_ref, m0_ref, buf0_ref,
         y_ref, co_ref, no_ref, mout_ref, bufo_ref, xp_scr, c_scr, n_scr, m_scr) = refs
    else:
        (mx_ref, mv_ref, mo_ref, gz_ref, cw_ref, cb_ref, wq_ref, wk_ref, gb_ref, gn_ref, skip_ref,
         y_ref, co_ref, no_ref, mout_ref, bufo_ref, xp_scr, c_scr, n_scr, m_scr) = refs
    c = pl.program_id(1)
    tail = CONV_W - 1

    @pl.when(c == 0)
    def _():
        xp_scr[0:CONV_PAD, :] = jnp.zeros((CONV_PAD, ML_W), F32)
        if has_state:
            c_scr[...] = c0_ref[0]
            n_scr[...] = n0_ref[0]
            m_scr[...] = m0_ref[0]
            xp_scr[CONV_PAD - tail:CONV_PAD, :] = buf0_ref[0]
        else:
            c_scr[...] = jnp.zeros_like(c_scr)
            n_scr[...] = jnp.zeros_like(n_scr)
            m_scr[...] = jnp.zeros_like(m_scr)

    xp_scr[CONV_PAD:CONV_PAD + L, :] = mx_ref[...]
    xc = cb_ref[...]
    for j in range(CONV_W):
        xc = xc + cw_ref[j:j + 1, :] * xp_scr[CONV_PAD - tail + j:CONV_PAD - tail + j + L, :]
    xc = _silu(xc)
    new_tail = xp_scr[CONV_PAD + Lv - tail:CONV_PAD + Lv, :]
    xp_scr[CONV_PAD - tail:CONV_PAD, :] = new_tail

    gz = gz_ref[...] + gb_ref[...]
    lane = lax.broadcasted_iota(jnp.int32, (L, ML_GATE_W), 1)
    logsig = jnp.minimum(gz, 0.0) - jnp.log1p(jnp.exp(-jnp.abs(gz)))
    gc = jnp.where(lane < ML_HEADS, gz, logsig)
    if Lv < L:
        rowi = lax.broadcasted_iota(jnp.int32, (L, ML_GATE_W), 0)
        gc = jnp.where(rowi < Lv, gc, jnp.where(lane < ML_HEADS, NEG_BIG, 0.0))
    r_i = lax.broadcasted_iota(jnp.int32, (L, L), 0)
    c_i = lax.broadcasted_iota(jnp.int32, (L, L), 1)
    tri = r_i >= c_i
    fc = _dot_hi(tri.astype(F32), gc)
    eye = (lax.broadcasted_iota(jnp.int32, (8, ML_GATE_W), 0) == lax.broadcasted_iota(jnp.int32, (8, ML_GATE_W), 1)).astype(F32)
    g_rows = _dot_nt_hi(eye, gc)
    f_rows = _dot_nt_hi(eye, fc)

    for h in range(ML_HEADS):
        sl = slice(h * ML_DH, (h + 1) * ML_DH)
        xh = xc[:, sl]
        q = _dot(xh, wq_ref[h])
        k = _dot(xh, wk_ref[h]) * (ML_DH ** -0.5)
        v = mv_ref[:, sl]
        f_col = fc[:, ML_HEADS + h:ML_HEADS + h + 1]
        i_col = gc[:, h:h + 1]
        f_row = f_rows[ML_HEADS + h:ML_HEADS + h + 1, :]
        i_row = g_rows[h:h + 1, :]
        m_prev = m_scr[0:1, h:h + 1]
        dlog = jnp.where(tri, f_col - f_row + i_row, NEG_BIG)
        inter = f_col + m_prev
        m_t = jnp.maximum(jnp.max(dlog, axis=1, keepdims=True), inter)
        w = jnp.exp(dlog - m_t)
        s = _dot_nt(q, k) * w
        a_inter = jnp.exp(inter - m_t)
        c_prev = c_scr[h]
        n_prev = n_scr[h:h + 1, :]
        num = _dot(s, v) + a_inter * _dot(q, c_prev)
        den = jnp.sum(s, axis=1, keepdims=True) + a_inter * jnp.sum(q * n_prev, axis=1, keepdims=True)
        hh = num * (1.0 / jnp.maximum(jnp.abs(den), jnp.exp(-m_t)))
        f_last = fc[Lv - 1:Lv, ML_HEADS + h:ML_HEADS + h + 1]
        m_new = jnp.maximum(f_last + m_prev, jnp.max(f_last - f_row + i_row, axis=1, keepdims=True))
        w_end = jnp.exp(f_last - f_col + i_col - m_new)
        a_old = jnp.exp(f_last + m_prev - m_new)
        kw = k * w_end
        c_scr[h] = a_old * c_prev + _dot_tn(kw, v)
        n_scr[h:h + 1, :] = a_old * n_prev + jnp.sum(kw, axis=0, keepdims=True)
        m_scr[0:1, h:h + 1] = m_new
        hc = hh - jnp.mean(hh, axis=-1, keepdims=True)
        hn = hc * lax.rsqrt(jnp.mean(hc * hc, axis=-1, keepdims=True) + GN_EPS) * gn_ref[:, sl]
        y_ref[:, sl] = (_sigmoid(mo_ref[:, sl]) * (hn + skip_ref[:, sl] * xh)).astype(BF16)

    @pl.when(c == nc - 1)
    def _():
        co_ref[0] = c_scr[...]
        no_ref[0] = n_scr[...]
        mout_ref[0] = m_scr[...]
        bufo_ref[0] = new_tail


def _mlstm(zml, weights, state, *, B, nc, L, Lv, row0):
    cw, cb, wq, wk, gb, gn, skip = weights
    has_state = state is not None
    row = lambda b, c: row0 + b * nc + c
    const2 = lambda b, c: (0, 0)
    in_specs = [pl.BlockSpec((L, ML_W), lambda b, c: (row(b, c), 0)),
                pl.BlockSpec((L, ML_W), lambda b, c: (row(b, c), 1)),
                pl.BlockSpec((L, ML_W), lambda b, c: (row(b, c), 2)),
                pl.BlockSpec((L, ML_GATE_W), lambda b, c: (row(b, c), 3 * ML_W // ML_GATE_W)),
                pl.BlockSpec((CONV_W, ML_W), const2),
                pl.BlockSpec((1, ML_W), const2),
                pl.BlockSpec((ML_HEADS, ML_DH, ML_DH), lambda b, c: (0, 0, 0)),
                pl.BlockSpec((ML_HEADS, ML_DH, ML_DH), lambda b, c: (0, 0, 0)),
                pl.BlockSpec((1, ML_GATE_W), const2),
                pl.BlockSpec((1, ML_W), const2),
                pl.BlockSpec((1, ML_W), const2)]
    args = [zml, zml, zml, zml, cw, cb, wq, wk, gb, gn, skip]
    st_specs = [pl.BlockSpec((1, ML_HEADS, ML_DH, ML_DH), lambda b, c: (b, 0, 0, 0)),
                pl.BlockSpec((1, ML_HEADS, ML_DH), lambda b, c: (b, 0, 0)),
                pl.BlockSpec((1, 1, LANES), lambda b, c: (b, 0, 0)),
                pl.BlockSpec((1, CONV_W - 1, ML_W), lambda b, c: (b, 0, 0))]
    if has_state:
        in_specs += st_specs
        args += list(state)
    return pl.pallas_call(
        functools.partial(_mlstm_kernel, L=L, Lv=Lv, nc=nc, has_state=has_state),
        grid=(B, nc),
        in_specs=in_specs,
        out_specs=[pl.BlockSpec((L, ML_W), lambda b, c: (b * nc + c, 0))] + st_specs,
        out_shape=[jax.ShapeDtypeStruct((B * nc * L, ML_W), BF16),
                   jax.ShapeDtypeStruct((B, ML_HEADS, ML_DH, ML_DH), F32),
                   jax.ShapeDtypeStruct((B, ML_HEADS, ML_DH), F32),
                   jax.ShapeDtypeStruct((B, 1, LANES), F32),
                   jax.ShapeDtypeStruct((B, CONV_W - 1, ML_W), F32)],
        scratch_shapes=[pltpu.VMEM((CONV_PAD + L, ML_W), F32),
                        pltpu.VMEM((ML_HEADS, ML_DH, ML_DH), F32),
                        pltpu.VMEM((ML_HEADS, ML_DH), F32),
                        pltpu.VMEM((1, LANES), F32)],
        compiler_params=_cparams(("parallel", "arbitrary")),
        name="mlstm",
    )(*args)


def _rwkv_prep_kernel(*refs, L, Lv, nc, has_state):
    if has_state:
        (z_ref, mu_ref, wa0_ref, w2a_ref, g2_ref, kkp_ref, ka_ref, rk_ref, bd_ref, prev_ref,
         r_ref, w_ref, k_ref, v_ref, kk_ref, kka_ref, g_ref, bonus_ref, shift_ref, xs_scr) = refs
    else:
        (z_ref, mu_ref, wa0_ref, w2a_ref, g2_ref, kkp_ref, ka_ref, rk_ref, bd_ref,
         r_ref, w_ref, k_ref, v_ref, kk_ref, kka_ref, g_ref, bonus_ref, shift_ref, xs_scr) = refs
    c = pl.program_id(1)

    @pl.when(c == 0)
    def _():
        xs_scr[0:CONV_PAD, :] = jnp.zeros((CONV_PAD, RW_COLS), F32)
        if has_state:
            xs_scr[CONV_PAD - 1:CONV_PAD, :] = prev_ref[0]

    z = z_ref[...]
    xs_scr[CONV_PAD:CONV_PAD + L, :] = z
    zp = xs_scr[CONV_PAD - 1:CONV_PAD - 1 + L, :]
    zs = z + (zp - z) * mu_ref[...]
    last = z[Lv - 1:Lv, :]
    xs_scr[CONV_PAD - 1:CONV_PAD, :] = last

    @pl.when(c == nc - 1)
    def _():
        shift_ref[0] = last

    r = zs[:, 0:RW_W]
    kr = zs[:, RW_W:2 * RW_W]
    vr = zs[:, 2 * RW_W:3 * RW_W]
    wa = zs[:, 3 * RW_W:3 * RW_W + RW_LORA_W + RW_LORA_A]
    gl = zs[:, 3 * RW_W + RW_LORA_W + RW_LORA_A:]
    lane = lax.broadcasted_iota(jnp.int32, wa.shape, 1)
    wa_in = jnp.where(lane < RW_LORA_W, jnp.tanh(wa), wa)
    lora = _dot_hi(wa_in, w2a_ref[...]) + wa0_ref[...]
    log_w = -RW_DECAY_SCALE * _sigmoid(lora[:, 0:RW_W])
    a = _sigmoid(lora[:, RW_W:])
    g = _dot_hi(_sigmoid(gl), g2_ref[...])
    bd = bd_ref[...]
    kk = kr * kkp_ref[...]
    kk = kk / jnp.maximum(jnp.sqrt(_dot_hi(kk * kk, bd)), 1e-12)
    k2 = kr * (1.0 + (a - 1.0) * ka_ref[...])
    bonus = _dot_hi(r * k2 * rk_ref[...], bd) * vr
    r_ref[...] = r
    w_ref[...] = log_w
    k_ref[...] = k2
    v_ref[...] = vr
    kk_ref[...] = kk
    kka_ref[...] = kk * a
    g_ref[...] = g
    bonus_ref[...] = bonus


def _rwkv_prep(zrw, weights, prev, *, B, nc, L, Lv, row0):
    mu, wa0, w2a, g2, kkp, ka, rk, bd = weights
    has_state = prev is not None
    const2 = lambda b, c: (0, 0)
    in_specs = [pl.BlockSpec((L, RW_COLS), lambda b, c: (row0 + b * nc + c, 0)),
                pl.BlockSpec((1, RW_COLS), const2),
                pl.BlockSpec((1, 2 * RW_W), const2),
                pl.BlockSpec((RW_LORA_W + RW_LORA_A, 2 * RW_W), const2),
                pl.BlockSpec((RW_LORA_G, RW_W), const2),
                pl.BlockSpec((1, RW_W), const2),
                pl.BlockSpec((1, RW_W), const2),
                pl.BlockSpec((1, RW_W), const2),
                pl.BlockSpec((RW_W, RW_W), const2)]
    args = [zrw, mu, wa0, w2a, g2, kkp, ka, rk, bd]
    if has_state:
        in_specs.append(pl.BlockSpec((1, 1, RW_COLS), lambda b, c: (b, 0, 0)))
        args.append(prev)
    tok = pl.BlockSpec((L, RW_W), lambda b, c: (b * nc + c, 0))
    tok_shape = jax.ShapeDtypeStruct((B * nc * L, RW_W), F32)
    return pl.pallas_call(
        functools.partial(_rwkv_prep_kernel, L=L, Lv=Lv, nc=nc, has_state=has_state),
        grid=(B, nc),
        in_specs=in_specs,
        out_specs=[tok] * 8 + [pl.BlockSpec((1, 1, RW_COLS), lambda b, c: (b, 0, 0))],
        out_shape=[tok_shape] * 8 + [jax.ShapeDtypeStruct((B, 1, RW_COLS), F32)],
        scratch_shapes=[pltpu.VMEM((CONV_PAD + L, RW_COLS), F32)],
        compiler_params=_cparams(("parallel", "arbitrary")),
        name="rwkv_prep",
    )(*args)


RW_C = 64
RW_PAIRS = RW_HEADS // 2


def _rwkv_chunk_kernel(*refs, C, CB, Lv, nc, has_state):
    if has_state:
        (r_ref, lw_ref, k_ref, v_ref, kk_ref, kka_ref, g_ref, bonus_ref, gnw_ref, gnb_ref, s0_ref,
         y_ref, so_ref, s_scr) = refs
    else:
        (r_ref, lw_ref, k_ref, v_ref, kk_ref, kka_ref, g_ref, bonus_ref, gnw_ref, gnb_ref,
         y_ref, so_ref, s_scr) = refs
    c = pl.program_id(1)
    C2 = 2 * C
    chains = [(i, j) for i in range(CB) for j in range(RW_PAIRS)]
    sl = lambda j: slice(j * LANES, (j + 1) * LANES)

    @pl.when(c == 0)
    def _():
        if has_state:
            z = jnp.zeros((RW_DH, RW_DH), F32)
            for n, (i, j) in enumerate(chains):
                s_scr[n] = jnp.concatenate([jnp.concatenate([s0_ref[i, 2 * j], z], axis=1),
                                            jnp.concatenate([z, s0_ref[i, 2 * j + 1]], axis=1)], axis=0)
        else:
            s_scr[...] = jnp.zeros_like(s_scr)

    srow = lax.broadcasted_iota(jnp.int32, (LANES, LANES), 0)
    scol = lax.broadcasted_iota(jnp.int32, (LANES, LANES), 1)
    same_head = jnp.logical_or(jnp.logical_and(srow < RW_DH, scol < RW_DH),
                               jnp.logical_and(srow >= RW_DH, scol >= RW_DH))
    ones_bd = same_head.astype(F32)
    valid = lax.broadcasted_iota(jnp.int32, (C, LANES), 0) < Lv
    row = lax.broadcasted_iota(jnp.int32, (C2, C2), 0)
    col = lax.broadcasted_iota(jnp.int32, (C2, C2), 1)
    tt = row & (C - 1)
    ss = col & (C - 1)
    top = row < C
    bot = row >= C
    strict = ss < tt
    incl = ss <= tt
    keep0 = jnp.logical_or(jnp.logical_and(top, strict), jnp.logical_and(bot, incl))
    keep1 = jnp.logical_or(jnp.logical_and(top, incl), jnp.logical_and(bot, strict))
    tl = jnp.logical_and(top, col < C)
    br = jnp.logical_and(bot, col >= C)
    eye = (row == col).astype(F32)
    tri = (lax.broadcasted_iota(jnp.int32, (C, C), 0) >= lax.broadcasted_iota(jnp.int32, (C, C), 1)).astype(F32)
    left = lax.broadcasted_iota(jnp.int32, (C, LANES), 1) < RW_DH
    left2 = lax.broadcasted_iota(jnp.int32, (C2, LANES), 1) < RW_DH
    zeros = jnp.zeros((C, LANES), F32)

    ar, m0, m1, vs, bk_end, tots = [], [], [], [], [], []
    for i, j in chains:
        lw = jnp.where(valid, lw_ref[i, :, sl(j)], 0.0)
        cum = _dot_hi(tri, lw)
        tot = cum[C - 1:C, :]
        e_inv = jnp.exp(-cum)
        e_end = jnp.exp(tot - cum)
        kka = jnp.where(valid, kka_ref[i, :, sl(j)], 0.0)
        k = jnp.where(valid, k_ref[i, :, sl(j)], 0.0)
        a_t = -jnp.where(valid, kk_ref[i, :, sl(j)], 0.0) * jnp.exp(cum - lw)
        r_t = r_ref[i, :, sl(j)] * jnp.exp(cum)
        b_t = kka * e_inv
        k_t = k * e_inv
        ar_c = jnp.concatenate([a_t, r_t], axis=0)
        g0 = _dot_nt(jnp.where(left2, ar_c, 0.0), jnp.concatenate([b_t, k_t], axis=0))
        g1 = _dot_nt(jnp.where(left2, 0.0, jnp.concatenate([r_t, a_t], axis=0)), jnp.concatenate([k_t, b_t], axis=0))
        ar.append(ar_c.astype(BF16))
        m0.append(jnp.where(keep0, g0, 0.0))
        m1.append(jnp.where(keep1, g1, 0.0))
        vs.append(jnp.where(valid, v_ref[i, :, sl(j)], 0.0))
        bk_end.append(jnp.concatenate([kka * e_end, k * e_end], axis=0).astype(BF16))
        tots.append(tot)

    s_prev = [s_scr[n] for n in range(len(chains))]
    w_as = [_dot_nt(ar[n], s_prev[n]) for n in range(len(chains))]
    x0 = [_dot(m0[n][0:C, :], jnp.concatenate([zeros, vs[n]], axis=0)) for n in range(len(chains))]
    x1 = [_dot(m1[n][C:, :], jnp.concatenate([vs[n], zeros], axis=0)) for n in range(len(chains))]

    p = [jnp.where(tl, m0[n], jnp.where(br, m1[n], 0.0)) for n in range(len(chains))]
    t_inv = [eye + p[n] for n in range(len(chains))]
    step = 1
    while 2 * step < Lv:
        p = [_dot(p[n], p[n]) for n in range(len(chains))]
        t_inv = [t_inv[n] + _dot(t_inv[n], p[n]) for n in range(len(chains))]
        step *= 2

    rhs = [w_as[n][0:C, :] + jnp.where(left, x0[n], x1[n]) for n in range(len(chains))]
    tu = [_dot(t_inv[n], jnp.concatenate([rhs[n], rhs[n]], axis=0)) for n in range(len(chains))]
    u = [jnp.where(left, tu[n][0:C, :], tu[n][C:, :]) for n in range(len(chains))]
    uv = [jnp.concatenate([u[n], vs[n]], axis=0).astype(BF16) for n in range(len(chains))]
    vu = [jnp.concatenate([vs[n], u[n]], axis=0).astype(BF16) for n in range(len(chains))]
    o0 = [_dot(m0[n][C:, :], uv[n]) for n in range(len(chains))]
    o1 = [_dot(m1[n][0:C, :], vu[n]) for n in range(len(chains))]
    s_new = [_dot_tn(uv[n], bk_end[n]) for n in range(len(chains))]
    for n in range(len(chains)):
        s_scr[n] = jnp.where(ones_bd > 0.0, s_prev[n] * jnp.exp(tots[n]) + s_new[n], 0.0)

    o = [w_as[n][C:, :] + jnp.where(left, o0[n], o1[n]) for n in range(len(chains))]
    mean = [_dot_hi(o[n], ones_bd) * (1.0 / RW_DH) for n in range(len(chains))]
    oc = [o[n] - mean[n] for n in range(len(chains))]
    var = [_dot_hi(oc[n] * oc[n], ones_bd) * (1.0 / RW_DH) for n in range(len(chains))]
    for n, (i, j) in enumerate(chains):
        yn = oc[n] * lax.rsqrt(var[n] + RW_GN_EPS) * gnw_ref[:, sl(j)] + gnb_ref[:, sl(j)]
        y_ref[i, :, sl(j)] = ((yn + bonus_ref[i, :, sl(j)]) * g_ref[i, :, sl(j)]).astype(BF16)

    @pl.when(c == nc - 1)
    def _():
        for n, (i, j) in enumerate(chains):
            tile = s_scr[n]
            so_ref[i, 2 * j] = tile[0:RW_DH, 0:RW_DH]
            so_ref[i, 2 * j + 1] = tile[RW_DH:, RW_DH:]


def _rwkv_chunk_scan(toks, gnw, gnb, state, *, B, T, C, CB, Lv):
    nc = T // C
    has_state = state is not None
    tok = pl.BlockSpec((CB, C, RW_W), lambda b, c: (b, c, 0))
    const2 = lambda b, c: (0, 0)
    st = pl.BlockSpec((CB, RW_HEADS, RW_DH, RW_DH), lambda b, c: (b, 0, 0, 0))
    in_specs = [tok] * 8 + [pl.BlockSpec((1, RW_W), const2), pl.BlockSpec((1, RW_W), const2)]
    args = [t.reshape(B, T, RW_W) for t in toks] + [gnw, gnb]
    if has_state:
        in_specs.append(st)
        args.append(state)
    y, s = pl.pallas_call(
        functools.partial(_rwkv_chunk_kernel, C=C, CB=CB, Lv=Lv, nc=nc, has_state=has_state),
        grid=(B // CB, nc),
        in_specs=in_specs,
        out_specs=[tok, st],
        out_shape=[jax.ShapeDtypeStruct((B, T, RW_W), BF16),
                   jax.ShapeDtypeStruct((B, RW_HEADS, RW_DH, RW_DH), F32)],
        scratch_shapes=[pltpu.VMEM((CB * RW_PAIRS, LANES, LANES), F32)],
        compiler_params=_cparams(("parallel", "arbitrary")),
        name="rwkv_chunk",
    )(*args)
    return y.reshape(B * T, RW_W), s


def _merge_kernel(x_ref, zg_ref, yr_ref, ym_ref, yw_ref, wb_ref, wo_ref, o_ref):
    acc = None
    for n, y_ref in enumerate((yr_ref, ym_ref, yw_ref)):
        proj = jnp.dot(y_ref[...], wb_ref[n], preferred_element_type=F32)
        term = _sigmoid(zg_ref[:, n * D_MODEL:(n + 1) * D_MODEL]) * proj
        acc = term if acc is None else acc + term
    o_ref[...] = x_ref[...] + jnp.dot(acc.astype(BF16), wo_ref[...], preferred_element_type=F32)


def _merge(x, zgate, y_ret, y_ml, y_rw, w_branch, w_out, tm=256):
    n = x.shape[0]
    tokspec = lambda w: pl.BlockSpec((tm, w), lambda i: (i, 0))
    return pl.pallas_call(
        _merge_kernel,
        grid=(n // tm,),
        in_specs=[tokspec(D_MODEL), tokspec(N_BRANCH * D_MODEL), tokspec(RET_W), tokspec(ML_W), tokspec(RW_W),
                  pl.BlockSpec((N_BRANCH, RET_W, D_MODEL), lambda i: (0, 0, 0)),
                  pl.BlockSpec((D_MODEL, D_MODEL), lambda i: (0, 0))],
        out_specs=tokspec(D_MODEL),
        out_shape=jax.ShapeDtypeStruct((n, D_MODEL), F32),
        compiler_params=_cparams(("parallel",)),
        name="merge",
    )(x, zgate, y_ret, y_ml, y_rw, w_branch, w_out)


def _ffn_kernel(*refs, ne, routed, final):
    refs = list(refs)
    x_ref, g_ref = refs[:2]
    refs = refs[2:]
    router_ref = refs.pop(0) if routed else None
    wg_ref, wu_ref, wd_ref = refs[:3]
    refs = refs[3:]
    fin_ref = refs.pop(0) if final else None
    o_ref, h_scr, acc_scr, comb_scr = refs
    e = pl.program_id(1)

    @pl.when(e == 0)
    def _():
        h = _rms(x_ref[...], g_ref[...])
        h_scr[...] = h.astype(BF16)
        acc_scr[...] = jnp.zeros_like(acc_scr)
        if routed:
            lane = lax.broadcasted_iota(jnp.int32, (h.shape[0], LANES), 1)
            logits = jnp.where(lane < ne, _dot_hi(h, router_ref[...]), NEG_BIG)
            m1 = jnp.max(logits, axis=1, keepdims=True)
            i1 = jnp.min(jnp.where(logits == m1, lane, LANES), axis=1, keepdims=True)
            rest = jnp.where(lane == i1, NEG_BIG, logits)
            m2 = jnp.max(rest, axis=1, keepdims=True)
            i2 = jnp.min(jnp.where(rest == m2, lane, LANES), axis=1, keepdims=True)
            e2 = jnp.exp(m2 - m1)
            p1 = 1.0 / (1.0 + e2)
            comb_scr[...] = jnp.where(lane == i1, p1, 0.0) + jnp.where(lane == i2, e2 * p1, 0.0)

    h = h_scr[...]
    hg = jnp.dot(h, wg_ref[0], preferred_element_type=F32)
    hu = jnp.dot(h, wu_ref[0], preferred_element_type=F32)
    y = jnp.dot((_silu(hg) * hu).astype(BF16), wd_ref[0], preferred_element_type=F32)
    if routed:
        lane = lax.broadcasted_iota(jnp.int32, comb_scr.shape, 1)
        y = y * jnp.sum(jnp.where(lane == e, comb_scr[...], 0.0), axis=1, keepdims=True)
    acc_scr[...] += y

    @pl.when(e == ne - 1)
    def _():
        out = x_ref[...] + acc_scr[...]
        if final:
            out = _rms(out, fin_ref[...])
        o_ref[...] = out


def _ffn(x, g, wg, wu, wd, router=None, fin=None, tm=512):
    n = x.shape[0]
    ne, _, f = wg.shape
    routed = router is not None
    final = fin is not None
    in_specs = [pl.BlockSpec((tm, D_MODEL), lambda i, e: (i, 0)),
                pl.BlockSpec((1, D_MODEL), lambda i, e: (0, 0))]
    args = [x, g]
    if routed:
        in_specs.append(pl.BlockSpec((D_MODEL, LANES), lambda i, e: (0, 0)))
        args.append(router)
    in_specs += [pl.BlockSpec((1, D_MODEL, f), lambda i, e: (e, 0, 0)),
                 pl.BlockSpec((1, D_MODEL, f), lambda i, e: (e, 0, 0)),
                 pl.BlockSpec((1, f, D_MODEL), lambda i, e: (e, 0, 0))]
    args += [wg, wu, wd]
    if final:
        in_specs.append(pl.BlockSpec((1, D_MODEL), lambda i, e: (0, 0)))
        args.append(fin)
    return pl.pallas_call(
        functools.partial(_ffn_kernel, ne=ne, routed=routed, final=final),
        grid=(n // tm, ne),
        in_specs=in_specs,
        out_specs=pl.BlockSpec((tm, D_MODEL), lambda i, e: (i, 0)),
        out_shape=jax.ShapeDtypeStruct((n, D_MODEL), F32),
        scratch_shapes=[pltpu.VMEM((tm, D_MODEL), BF16), pltpu.VMEM((tm, D_MODEL), F32),
                        pltpu.VMEM((tm, LANES), F32)],
        compiler_params=_cparams(("parallel", "arbitrary")),
        name="ffn",
    )(*args)


def _row(v):
    return v.reshape(1, -1)


def kernel(x_prompt, x_sample, state_ret, state_mlstm_C, state_mlstm_n, state_mlstm_m, state_mlstm_conv, state_rwkv, state_rwkv_shift, norm_mix, w_in, ret_gn, ml_conv_w, ml_conv_b, ml_wq, ml_wk, ml_bi, ml_bf, ml_gn, ml_skip, rw_mu, rw_w0, rw_w2, rw_a0, rw_a2, rw_g2, rw_kk, rw_ka, rw_rk, rw_gn_w, rw_gn_b, w_branch, w_out, norm_ffn, ffn_w_gate, ffn_w_up, ffn_w_down, moe_router, moe_w_gate, moe_w_up, moe_w_down, final_norm):
    nc_p = SEQ // CHUNK
    xs = jnp.pad(x_sample, ((0, 0), (0, S_PAD - DEC_SEQ), (0, 0)))
    x = jnp.concatenate([x_prompt.reshape(N_PROMPT, D_MODEL), xs.reshape(N_SAMPLE, D_MODEL)], axis=0)

    pos_p = jnp.arange(SEQ, dtype=jnp.int32)
    pos_s = PAST_LEN + jnp.arange(S_PAD, dtype=jnp.int32)
    ret_tab_p = _retention_tables(CHUNK, CHUNK, pos_p)
    ret_tab_s = _retention_tables(S_PAD, DEC_SEQ, pos_s)
    head_of = jnp.arange(RW_W) // RW_DH
    bd64 = (head_of[:, None] == head_of[None, :]).astype(F32)

    o_rq = 0
    o_mx = o_rq + RET_COLS
    o_mi = o_mx + 2 * ML_W
    o_mo = o_mi + 2 * ML_HEADS
    o_rw = o_mo + ML_W
    o_gate = o_rw + RW_COLS

    new_p, new_s = [], []
    for l in range(DEPTH):
        w = w_in[l]
        w_ret = w[:, o_rq:o_mx].astype(BF16)
        w_ml = jnp.concatenate([w[:, o_mx:o_mi], w[:, o_mo:o_rw], w[:, o_mi:o_mo],
                                jnp.zeros((D_MODEL, ML_GATE_W - 2 * ML_HEADS), F32)], axis=1).astype(BF16)
        w_rw = w[:, o_rw:o_gate].astype(BF16)
        w_gate = w[:, o_gate:].astype(BF16)
        g_mix = _row(norm_mix[l])
        zret = _norm_matmul(x, g_mix, w_ret)
        zml = _norm_matmul(x, g_mix, w_ml)
        zrw = _norm_matmul(x, g_mix, w_rw)
        zgate = _norm_matmul(x, g_mix, w_gate)

        gn = _row(ret_gn[l])
        yr_p, ret_p = _retention(zret, ret_tab_p, gn, None, B=BATCH, nc=nc_p, L=CHUNK, row0=0)
        yr_s, ret_s = _retention(zret, ret_tab_s, gn, state_ret[l], B=DEC_BATCH, nc=1, L=S_PAD,
                                 row0=N_PROMPT // S_PAD)

        gate_bias = jnp.concatenate([ml_bi[l], ml_bf[l], jnp.zeros((ML_GATE_W - 2 * ML_HEADS,), F32)])
        ml_weights = (ml_conv_w[l], _row(ml_conv_b[l]), ml_wq[l].astype(BF16), ml_wk[l].astype(BF16),
                      _row(gate_bias), _row(ml_gn[l]), _row(ml_skip[l]))
        m0 = jnp.pad(state_mlstm_m[l], ((0, 0), (0, LANES - ML_HEADS))).reshape(DEC_BATCH, 1, LANES)
        ym_p, c_p, n_p, m_p, buf_p = _mlstm(zml, ml_weights, None, B=BATCH, nc=nc_p, L=CHUNK, Lv=CHUNK, row0=0)
        ym_s, c_s, n_s, m_s, buf_s = _mlstm(zml, ml_weights,
                                            (state_mlstm_C[l], state_mlstm_n[l], m0, state_mlstm_conv[l]),
                                            B=DEC_BATCH, nc=1, L=S_PAD, Lv=DEC_SEQ, row0=N_PROMPT // S_PAD)

        w2a = jnp.zeros((RW_LORA_W + RW_LORA_A, 2 * RW_W), F32)
        w2a = w2a.at[:RW_LORA_W, :RW_W].set(rw_w2[l]).at[RW_LORA_W:, RW_W:].set(rw_a2[l])
        rw_weights = (_row(rw_mu[l]), _row(jnp.concatenate([rw_w0[l], rw_a0[l]])), w2a, rw_g2[l],
                      _row(rw_kk[l]), _row(rw_ka[l]), _row(rw_rk[l]), bd64)
        *tok_p, shift_p = _rwkv_prep(zrw, rw_weights, None, B=BATCH, nc=nc_p, L=CHUNK, Lv=CHUNK, row0=0)
        *tok_s, shift_s = _rwkv_prep(zrw, rw_weights, state_rwkv_shift[l].reshape(DEC_BATCH, 1, RW_COLS),
                                     B=DEC_BATCH, nc=1, L=S_PAD, Lv=DEC_SEQ, row0=N_PROMPT // S_PAD)
        gnw, gnb = _row(rw_gn_w[l]), _row(rw_gn_b[l])
        yw_p, rws_p = _rwkv_chunk_scan(tok_p, gnw, gnb, None, B=BATCH, T=SEQ, C=RW_C, CB=2, Lv=RW_C)
        yw_s, rws_s = _rwkv_chunk_scan(tok_s, gnw, gnb, state_rwkv[l], B=DEC_BATCH, T=S_PAD, C=S_PAD, CB=4,
                                       Lv=DEC_SEQ)

        y_ret = jnp.concatenate([yr_p, yr_s], axis=0)
        y_ml = jnp.concatenate([ym_p, ym_s], axis=0)
        y_rw = jnp.concatenate([yw_p, yw_s], axis=0)
        x = _merge(x, zgate, y_ret, y_ml, y_rw, w_branch[l].astype(BF16), w_out[l].astype(BF16))

        i = l // 2
        g_ffn = _row(norm_ffn[l])
        fin = _row(final_norm) if l == DEPTH - 1 else None
        if l % 2 == 0:
            halves = D_FF // D_FF_EXPERT
            wg = ffn_w_gate[i].reshape(D_MODEL, halves, D_FF_EXPERT).transpose(1, 0, 2).astype(BF16)
            wu = ffn_w_up[i].reshape(D_MODEL, halves, D_FF_EXPERT).transpose(1, 0, 2).astype(BF16)
            wd = ffn_w_down[i].reshape(halves, D_FF_EXPERT, D_MODEL).astype(BF16)
            x = _ffn(x, g_ffn, wg, wu, wd, None, fin)
        else:
            router = jnp.pad(moe_router[i], ((0, 0), (0, LANES - N_EXPERTS)))
            x = _ffn(x, g_ffn, moe_w_gate[i].astype(BF16), moe_w_up[i].astype(BF16), moe_w_down[i].astype(BF16),
                     router, fin)

        new_p.append((ret_p, c_p, n_p, m_p[:, 0, :ML_HEADS], buf_p, rws_p, shift_p[:, 0]))
        new_s.append((ret_s, c_s, n_s, m_s[:, 0, :ML_HEADS], buf_s, rws_s, shift_s[:, 0]))

    y_prompt = x[:N_PROMPT].reshape(BATCH, SEQ, D_MODEL)
    y_sample = x[N_PROMPT:].reshape(DEC_BATCH, S_PAD, D_MODEL)[:, :DEC_SEQ]
    st_p = tuple(jnp.stack([st[j] for st in new_p]) for j in range(7))
    st_s = tuple(jnp.stack([st[j] for st in new_s]) for j in range(7))
    return (y_prompt, y_sample) + st_p + st_s
```

```python
import functools

import jax
import jax.numpy as jnp
from jax import lax
from jax.experimental import pallas as pl
from jax.experimental.pallas import tpu as pltpu

D_MODEL = 1024
BATCH = 8
SEQ = 2048
DEPTH = 2
DEC_BATCH = 128
DEC_SEQ = 4
PAST_LEN = 16384
RET_HEADS = 4
RET_DK = 64
RET_DV = 128
ML_HEADS = 4
ML_DH = 128
CONV_W = 4
RW_HEADS = 8
RW_DH = 64
RW_LORA_W = 64
RW_LORA_A = 64
RW_LORA_G = 128
RET_W = RET_HEADS * RET_DV
ML_W = ML_HEADS * ML_DH
RW_W = RW_HEADS * RW_DH
N_BRANCH = 3
RW_COLS = 3 * RW_W + RW_LORA_W + RW_LORA_A + RW_LORA_G
D_FF = 2816
N_EXPERTS = 8
D_FF_EXPERT = 1408
CHUNK = 128
NORM_EPS = 1e-6
GN_EPS = 1e-5
RW_GN_EPS = 64e-5
ROPE_BASE = 10000.0
RW_DECAY_SCALE = 0.606531

LANES = 128
SUBLANES = 8
S_PAD = 16
N_PROMPT = BATCH * SEQ
N_SAMPLE = DEC_BATCH * S_PAD
N_TOK = N_PROMPT + N_SAMPLE
ML_GATE_W = LANES
ML_COLS = 3 * ML_W + ML_GATE_W
RET_COLS = 2 * RET_HEADS * RET_DK + 2 * RET_W
NEG_BIG = -1e30
VMEM_LIMIT = 56 * 1024 * 1024

RW_C = 64
RW_PAIRS = RW_HEADS // 2
RW_CB_PROMPT = 4
RW_CB_SAMPLE = 4
RET_NB_SAMPLE = 8
ML_NB_SAMPLE = 4
RWP_NB_SAMPLE = 8

F32 = jnp.float32
BF16 = jnp.bfloat16
HI = lax.Precision.HIGHEST


def _cparams(sem):
    return pltpu.CompilerParams(dimension_semantics=sem, vmem_limit_bytes=VMEM_LIMIT)


def _sigmoid(x):
    return 1.0 / (1.0 + jnp.exp(-x))


def _silu(x):
    return x * _sigmoid(x)


def _rms(x, g):
    return x * lax.rsqrt(jnp.mean(x * x, axis=-1, keepdims=True) + NORM_EPS) * g


def _dot(a, b):
    return jnp.dot(a.astype(BF16), b.astype(BF16), preferred_element_type=F32)


def _dot_nt(a, b):
    return lax.dot_general(a.astype(BF16), b.astype(BF16), (((1,), (1,)), ((), ())), preferred_element_type=F32)


def _dot_tn(a, b):
    return lax.dot_general(a.astype(BF16), b.astype(BF16), (((0,), (0,)), ((), ())), preferred_element_type=F32)


def _dot_hi(a, b):
    return jnp.dot(a, b, preferred_element_type=F32, precision=HI)


def _split(x):
    hi = x.astype(BF16)
    return hi, (x - hi.astype(F32)).astype(BF16)


def _as_bf16_mask(m):
    return m if m.dtype == BF16 else m.astype(F32).astype(BF16)


def _sel_dot(m, x):
    hi, lo = _split(x)
    m = _as_bf16_mask(m)
    return jnp.dot(m, hi, preferred_element_type=F32) + jnp.dot(m, lo, preferred_element_type=F32)


def _dot_sel(x, m):
    hi, lo = _split(x)
    m = _as_bf16_mask(m)
    return jnp.dot(hi, m, preferred_element_type=F32) + jnp.dot(lo, m, preferred_element_type=F32)


def _sel_dot_nt(m, x):
    hi, lo = _split(x)
    m = _as_bf16_mask(m)
    dn = (((1,), (1,)), ((), ()))
    return (lax.dot_general(m, hi, dn, preferred_element_type=F32)
            + lax.dot_general(m, lo, dn, preferred_element_type=F32))


def _norm_matmul_kernel(x_ref, g_ref, w_ref, o_ref):
    h = _rms(x_ref[...], g_ref[...])
    o_ref[...] = jnp.dot(h.astype(BF16), w_ref[...], preferred_element_type=F32)


def _norm_matmul(x, g, w, tm=256):
    n, d = x.shape
    c = w.shape[1]
    return pl.pallas_call(
        _norm_matmul_kernel,
        grid=(n // tm,),
        in_specs=[pl.BlockSpec((tm, d), lambda i: (i, 0)),
                  pl.BlockSpec((1, d), lambda i: (0, 0)),
                  pl.BlockSpec((d, c), lambda i: (0, 0))],
        out_specs=pl.BlockSpec((tm, c), lambda i: (i, 0)),
        out_shape=jax.ShapeDtypeStruct((n, c), F32),
        compiler_params=_cparams(("parallel",)),
        name="norm_matmul",
    )(x, g, w)


def _retention_kernel(*refs, L, nc, nb, has_state):
    if has_state:
        (q_ref, k_ref, v_ref, g_ref, cos_ref, sin_ref, di_ref, dq_ref, dk_ref, dc_ref, gn_ref, s0_ref,
         y_ref, so_ref, s_scr) = refs
    else:
        (q_ref, k_ref, v_ref, g_ref, cos_ref, sin_ref, di_ref, dq_ref, dk_ref, dc_ref, gn_ref,
         y_ref, so_ref, s_scr) = refs
    c = pl.program_id(1)

    @pl.when(c == 0)
    def _():
        if has_state:
            s_scr[...] = s0_ref[...]
        else:
            s_scr[...] = jnp.zeros_like(s_scr)

    qk_w = RET_HEADS * RET_DK
    half = RET_DK // 2
    lane = lax.broadcasted_iota(jnp.int32, (nb * L, qk_w), 1)
    first_half = (lane % RET_DK) < half
    cos = cos_ref[...]
    sin = sin_ref[...]

    def rot(x):
        swapped = jnp.where(first_half, pltpu.roll(x, qk_w - half, 1), pltpu.roll(x, half, 1))
        return x * cos + swapped * sin

    q = rot(q_ref[...])
    k = rot(k_ref[...]) * (RET_DK ** -0.5)
    dq = dq_ref[...]
    dk = dk_ref[...]
    dc = dc_ref[...]
    chains = [(i, h) for i in range(nb) for h in range(RET_HEADS)]
    rows = lambda i: slice(i * L, (i + 1) * L)
    kcols = lambda h: slice(h * RET_DK, (h + 1) * RET_DK)
    vcols = lambda h: slice(h * RET_DV, (h + 1) * RET_DV)

    qh = [q[rows(i), kcols(h)].astype(BF16) for i, h in chains]
    kh = [k[rows(i), kcols(h)] for i, h in chains]
    vh = [v_ref[rows(i), vcols(h)].astype(BF16) for i, h in chains]
    s = [_dot_nt(qh[n], kh[n]) * di_ref[h] for n, (i, h) in enumerate(chains)]
    s_prev = [s_scr[i, h] for i, h in chains]
    cross = [_dot(qh[n], s_prev[n]) for n in range(len(chains))]
    upd = [_dot_tn(kh[n] * dk[:, h:h + 1], vh[n]) for n, (i, h) in enumerate(chains)]
    inner = [_dot(s[n], vh[n]) for n in range(len(chains))]
    for n, (i, h) in enumerate(chains):
        s_scr[i, h] = s_prev[n] * dc[:, h:h + 1] + upd[n]
        o = inner[n] + cross[n] * dq[:, h:h + 1]
        oc = o - jnp.mean(o, axis=-1, keepdims=True)
        yn = oc * lax.rsqrt(jnp.mean(oc * oc, axis=-1, keepdims=True) + GN_EPS) * gn_ref[:, vcols(h)]
        y_ref[rows(i), vcols(h)] = (_silu(g_ref[rows(i), vcols(h)]) * yn).astype(BF16)

    @pl.when(c == nc - 1)
    def _():
        so_ref[...] = s_scr[...]


def _retention(zret, tables, gn, state, *, B, nc, L, nb, row0):
    cos, sin, di, dq, dk, dc = tables
    has_state = state is not None
    qk_w = RET_HEADS * RET_DK
    R = nb * L
    row = lambda b, c: row0 + b * nc + c
    st = pl.BlockSpec((nb, RET_HEADS, RET_DK, RET_DV), lambda b, c: (b, 0, 0, 0))
    in_specs = [pl.BlockSpec((R, qk_w), lambda b, c: (row(b, c), 0)),
                pl.BlockSpec((R, qk_w), lambda b, c: (row(b, c), 1)),
                pl.BlockSpec((R, RET_W), lambda b, c: (row(b, c), 1)),
                pl.BlockSpec((R, RET_W), lambda b, c: (row(b, c), 2)),
                pl.BlockSpec((R, qk_w), lambda b, c: (c, 0)),
                pl.BlockSpec((R, qk_w), lambda b, c: (c, 0)),
                pl.BlockSpec((RET_HEADS, L, L), lambda b, c: (0, 0, 0)),
                pl.BlockSpec((L, RET_HEADS), lambda b, c: (0, 0)),
                pl.BlockSpec((L, RET_HEADS), lambda b, c: (0, 0)),
                pl.BlockSpec((1, RET_HEADS), lambda b, c: (0, 0)),
                pl.BlockSpec((1, RET_W), lambda b, c: (0, 0))]
    args = [zret, zret, zret, zret, cos, sin, di, dq, dk, dc, gn]
    if has_state:
        in_specs.append(st)
        args.append(state)
    return pl.pallas_call(
        functools.partial(_retention_kernel, L=L, nc=nc, nb=nb, has_state=has_state),
        grid=(B // nb, nc),
        in_specs=in_specs,
        out_specs=[pl.BlockSpec((R, RET_W), lambda b, c: (b * nc + c, 0)), st],
        out_shape=[jax.ShapeDtypeStruct((B * nc * L, RET_W), BF16),
                   jax.ShapeDtypeStruct((B, RET_HEADS, RET_DK, RET_DV), F32)],
        scratch_shapes=[pltpu.VMEM((nb, RET_HEADS, RET_DK, RET_DV), F32)],
        compiler_params=_cparams(("parallel", "arbitrary")),
        name="retention",
    )(*args)


def _retention_tables(L, Lv, pos, nb):
    half = RET_DK // 2
    inv = ROPE_BASE ** (-jnp.arange(half, dtype=F32) / half)
    ang = pos.astype(F32)[:, None] * inv[None, :]
    cos = jnp.tile(jnp.concatenate([jnp.cos(ang), jnp.cos(ang)], axis=1), (nb, RET_HEADS))
    sin = jnp.tile(jnp.concatenate([-jnp.sin(ang), jnp.sin(ang)], axis=1), (nb, RET_HEADS))
    log_gamma = jnp.log1p(-(2.0 ** (-5.0 - jnp.arange(RET_HEADS, dtype=F32))))
    idx = jnp.arange(L, dtype=F32)
    diff = idx[:, None] - idx[None, :]
    causal = diff >= 0
    di = jnp.where(causal[None], jnp.exp(jnp.where(causal, diff, 0.0)[None] * log_gamma[:, None, None]), 0.0)
    dq = jnp.exp((idx[:, None] + 1.0) * log_gamma[None, :])
    dk = jnp.where((idx < Lv)[:, None], jnp.exp((Lv - 1.0 - idx)[:, None] * log_gamma[None, :]), 0.0)
    dc = jnp.exp(Lv * log_gamma)[None, :]
    return cos, sin, di, dq, dk, dc


ROW_PAD = SUBLANES


def _mlstm_kernel(*refs, L, Lv, nc, nb, has_state):
    if has_state:
        (mx_ref, mv_ref, mo_ref, gz_ref, cw_ref, cb_ref, wq_ref, wk_ref, gb_ref, gn_ref, skip_ref,
         c0_ref, n0_ref, m0_ref, buf0_ref,
         y_ref, co_ref, no_ref, mout_ref, bufo_ref, xp_scr, c_scr, n_scr, m_scr) = refs
    else:
        (mx_ref, mv_ref, mo_ref, gz_ref, cw_ref, cb_ref, wq_ref, wk_ref, gb_ref, gn_ref, skip_ref,
         y_ref, co_ref, no_ref, mout_ref, bufo_ref, xp_scr, c_scr, n_scr, m_scr) = refs
    c = pl.program_id(1)
    tail = CONV_W - 1

    @pl.when(c == 0)
    def _():
        for i in range(nb):
            xp_scr[i, 0:ROW_PAD, :] = jnp.zeros((ROW_PAD, ML_W), F32)
        if has_state:
            c_scr[...] = c0_ref[...]
            n_scr[...] = n0_ref[...]
            m_scr[...] = m0_ref[...]
            for i in range(nb):
                xp_scr[i, ROW_PAD - tail:ROW_PAD, :] = buf0_ref[i]
        else:
            c_scr[...] = jnp.zeros_like(c_scr)
            n_scr[...] = jnp.zeros_like(n_scr)
            m_scr[...] = jnp.zeros_like(m_scr)

    rows = lambda i: slice(i * L, (i + 1) * L)
    cols = lambda h: slice(h * ML_DH, (h + 1) * ML_DH)
    lane = lax.broadcasted_iota(jnp.int32, (L, ML_GATE_W), 1)
    rowi = lax.broadcasted_iota(jnp.int32, (L, ML_GATE_W), 0)
    tri = lax.broadcasted_iota(jnp.int32, (L, L), 0) >= lax.broadcasted_iota(jnp.int32, (L, L), 1)
    eye = (lax.broadcasted_iota(jnp.int32, (SUBLANES, ML_GATE_W), 0)
           == lax.broadcasted_iota(jnp.int32, (SUBLANES, ML_GATE_W), 1))

    xc, new_tail, gc, fc, g_rows, f_rows = [], [], [], [], [], []
    for i in range(nb):
        xp_scr[i, ROW_PAD:ROW_PAD + L, :] = mx_ref[rows(i), :]
        acc = cb_ref[...]
        for j in range(CONV_W):
            acc = acc + cw_ref[j:j + 1, :] * xp_scr[i, ROW_PAD - tail + j:ROW_PAD - tail + j + L, :]
        xc.append(_silu(acc))
        new_tail.append(xp_scr[i, ROW_PAD + Lv - tail:ROW_PAD + Lv, :])
        xp_scr[i, ROW_PAD - tail:ROW_PAD, :] = new_tail[i]
        gz = gz_ref[rows(i), :] + gb_ref[...]
        logsig = jnp.minimum(gz, 0.0) - jnp.log1p(jnp.exp(-jnp.abs(gz)))
        g = jnp.where(lane < ML_HEADS, gz, logsig)
        if Lv < L:
            g = jnp.where(rowi < Lv, g, jnp.where(lane < ML_HEADS, NEG_BIG, 0.0))
        gc.append(g)
        fc.append(_sel_dot(tri, g))
        g_rows.append(_sel_dot_nt(eye, g))
        f_rows.append(_sel_dot_nt(eye, fc[i]))

    chains = [(i, h) for i in range(nb) for h in range(ML_HEADS)]
    nch = len(chains)
    xh = [xc[i][:, cols(h)] for i, h in chains]
    xh_b = [x.astype(BF16) for x in xh]
    q = [jnp.dot(xh_b[n], wq_ref[h], preferred_element_type=F32) for n, (i, h) in enumerate(chains)]
    k = [jnp.dot(xh_b[n], wk_ref[h], preferred_element_type=F32) * (ML_DH ** -0.5) for n, (i, h) in enumerate(chains)]
    q_b = [x.astype(BF16) for x in q]
    v_b = [mv_ref[rows(i), cols(h)].astype(BF16) for i, h in chains]
    qk = [_dot_nt(q_b[n], k[n]) for n in range(nch)]
    c_prev = [c_scr[i, h] for i, h in chains]
    qc = [_dot(q_b[n], c_prev[n]) for n in range(nch)]

    s, a_inter, m_t, kw, a_old, m_new = [], [], [], [], [], []
    for n, (i, h) in enumerate(chains):
        f_col = fc[i][:, ML_HEADS + h:ML_HEADS + h + 1]
        i_col = gc[i][:, h:h + 1]
        f_row = f_rows[i][ML_HEADS + h:ML_HEADS + h + 1, :]
        i_row = g_rows[i][h:h + 1, :]
        m_prev = m_scr[i, 0:1, h:h + 1]
        dlog = jnp.where(tri, f_col - f_row + i_row, NEG_BIG)
        inter = f_col + m_prev
        m_t.append(jnp.maximum(jnp.max(dlog, axis=1, keepdims=True), inter))
        s.append(qk[n] * jnp.exp(dlog - m_t[n]))
        a_inter.append(jnp.exp(inter - m_t[n]))
        f_last = fc[i][Lv - 1:Lv, ML_HEADS + h:ML_HEADS + h + 1]
        m_new.append(jnp.maximum(f_last + m_prev, jnp.max(f_last - f_row + i_row, axis=1, keepdims=True)))
        kw.append(k[n] * jnp.exp(f_last - f_col + i_col - m_new[n]))
        a_old.append(jnp.exp(f_last + m_prev - m_new[n]))

    sv = [_dot(s[n], v_b[n]) for n in range(nch)]
    ktv = [_dot_tn(kw[n], v_b[n]) for n in range(nch)]
    for n, (i, h) in enumerate(chains):
        n_prev = n_scr[i, h:h + 1, :]
        num = sv[n] + a_inter[n] * qc[n]
        den = jnp.sum(s[n], axis=1, keepdims=True) + a_inter[n] * jnp.sum(q[n] * n_prev, axis=1, keepdims=True)
        hh = num * (1.0 / jnp.maximum(jnp.abs(den), jnp.exp(-m_t[n])))
        c_scr[i, h] = a_old[n] * c_prev[n] + ktv[n]
        n_scr[i, h:h + 1, :] = a_old[n] * n_prev + jnp.sum(kw[n], axis=0, keepdims=True)
        m_scr[i, 0:1, h:h + 1] = m_new[n]
        hc = hh - jnp.mean(hh, axis=-1, keepdims=True)
        hn = hc * lax.rsqrt(jnp.mean(hc * hc, axis=-1, keepdims=True) + GN_EPS) * gn_ref[:, cols(h)]
        y_ref[rows(i), cols(h)] = (_sigmoid(mo_ref[rows(i), cols(h)]) * (hn + skip_ref[:, cols(h)] * xh[n])).astype(BF16)

    @pl.when(c == nc - 1)
    def _():
        co_ref[...] = c_scr[...]
        no_ref[...] = n_scr[...]
        mout_ref[...] = m_scr[...]
        for i in range(nb):
            bufo_ref[i] = new_tail[i]


def _mlstm(zml, weights, state, *, B, nc, L, Lv, nb, row0):
    cw, cb, wq, wk, gb, gn, skip = weights
    has_state = state is not None
    R = nb * L
    row = lambda b, c: row0 + b * nc + c
    const2 = lambda b, c: (0, 0)
    in_specs = [pl.BlockSpec((R, ML_W), lambda b, c: (row(b, c), 0)),
                pl.BlockSpec((R, ML_W), lambda b, c: (row(b, c), 1)),
                pl.BlockSpec((R, ML_W), lambda b, c: (row(b, c), 2)),
                pl.BlockSpec((R, ML_GATE_W), lambda b, c: (row(b, c), 3 * ML_W // ML_GATE_W)),
                pl.BlockSpec((CONV_W, ML_W), const2),
                pl.BlockSpec((1, ML_W), const2),
                pl.BlockSpec((ML_HEADS, ML_DH, ML_DH), lambda b, c: (0, 0, 0)),
                pl.BlockSpec((ML_HEADS, ML_DH, ML_DH), lambda b, c: (0, 0, 0)),
                pl.BlockSpec((1, ML_GATE_W), const2),
                pl.BlockSpec((1, ML_W), const2),
                pl.BlockSpec((1, ML_W), const2)]
    args = [zml, zml, zml, zml, cw, cb, wq, wk, gb, gn, skip]
    st_specs = [pl.BlockSpec((nb, ML_HEADS, ML_DH, ML_DH), lambda b, c: (b, 0, 0, 0)),
                pl.BlockSpec((nb, ML_HEADS, ML_DH), lambda b, c: (b, 0, 0)),
                pl.BlockSpec((nb, 1, LANES), lambda b, c: (b, 0, 0)),
                pl.BlockSpec((nb, CONV_W - 1, ML_W), lambda b, c: (b, 0, 0))]
    if has_state:
        in_specs += st_specs
        args += list(state)
    return pl.pallas_call(
        functools.partial(_mlstm_kernel, L=L, Lv=Lv, nc=nc, nb=nb, has_state=has_state),
        grid=(B // nb, nc),
        in_specs=in_specs,
        out_specs=[pl.BlockSpec((R, ML_W), lambda b, c: (b * nc + c, 0))] + st_specs,
        out_shape=[jax.ShapeDtypeStruct((B * nc * L, ML_W), BF16),
                   jax.ShapeDtypeStruct((B, ML_HEADS, ML_DH, ML_DH), F32),
                   jax.ShapeDtypeStruct((B, ML_HEADS, ML_DH), F32),
                   jax.ShapeDtypeStruct((B, 1, LANES), F32),
                   jax.ShapeDtypeStruct((B, CONV_W - 1, ML_W), F32)],
        scratch_shapes=[pltpu.VMEM((nb, ROW_PAD + L, ML_W), F32),
                        pltpu.VMEM((nb, ML_HEADS, ML_DH, ML_DH), F32),
                        pltpu.VMEM((nb, ML_HEADS, ML_DH), F32),
                        pltpu.VMEM((nb, 1, LANES), F32)],
        compiler_params=_cparams(("parallel", "arbitrary")),
        name="mlstm",
    )(*args)


def _rwkv_prep_kernel(*refs, L, Lv, nc, nb, has_state):
    if has_state:
        (z_ref, mu_ref, wa0_ref, w2a_ref, g2_ref, kkp_ref, ka_ref, rk_ref, bd_ref, prev_ref,
         r_ref, w_ref, k_ref, v_ref, kk_ref, kka_ref, g_ref, bonus_ref, shift_ref, xs_scr) = refs
    else:
        (z_ref, mu_ref, wa0_ref, w2a_ref, g2_ref, kkp_ref, ka_ref, rk_ref, bd_ref,
         r_ref, w_ref, k_ref, v_ref, kk_ref, kka_ref, g_ref, bonus_ref, shift_ref, xs_scr) = refs
    c = pl.program_id(1)
    R = nb * L

    @pl.when(c == 0)
    def _():
        xs_scr[0:ROW_PAD, :] = jnp.zeros((ROW_PAD, RW_COLS), F32)
        if has_state and nb == 1:
            xs_scr[ROW_PAD - 1:ROW_PAD, :] = prev_ref[0]

    z = z_ref[...]
    xs_scr[ROW_PAD:ROW_PAD + R, :] = z
    zp = xs_scr[ROW_PAD - 1:ROW_PAD - 1 + R, :]
    if nb > 1:
        first = [prev_ref[i] if has_state else jnp.zeros((1, RW_COLS), F32) for i in range(nb)]
        prev_rows = jnp.concatenate([jnp.broadcast_to(f, (L, RW_COLS)) for f in first], axis=0)
        rowi = lax.broadcasted_iota(jnp.int32, (R, RW_COLS), 0)
        zp = jnp.where((rowi & (L - 1)) == 0, prev_rows, zp)
    zs = z + (zp - z) * mu_ref[...]
    last = [z[i * L + Lv - 1:i * L + Lv, :] for i in range(nb)]
    if nb == 1:
        xs_scr[ROW_PAD - 1:ROW_PAD, :] = last[0]

    @pl.when(c == nc - 1)
    def _():
        for i in range(nb):
            shift_ref[i] = last[i]

    r = zs[:, 0:RW_W]
    kr = zs[:, RW_W:2 * RW_W]
    vr = zs[:, 2 * RW_W:3 * RW_W]
    wa = zs[:, 3 * RW_W:3 * RW_W + RW_LORA_W + RW_LORA_A]
    gl = zs[:, 3 * RW_W + RW_LORA_W + RW_LORA_A:]
    lane = lax.broadcasted_iota(jnp.int32, wa.shape, 1)
    wa_in = jnp.where(lane < RW_LORA_W, jnp.tanh(wa), wa)
    lora = jnp.dot(wa_in.astype(BF16), w2a_ref[...], preferred_element_type=F32) + wa0_ref[...]
    log_w = -RW_DECAY_SCALE * _sigmoid(lora[:, 0:RW_W])
    a = _sigmoid(lora[:, RW_W:])
    g = jnp.dot(_sigmoid(gl).astype(BF16), g2_ref[...], preferred_element_type=F32)
    bd = bd_ref[...]
    kk = kr * kkp_ref[...]
    kk = kk / jnp.maximum(jnp.sqrt(_dot_sel(kk * kk, bd)), 1e-12)
    k2 = kr * (1.0 + (a - 1.0) * ka_ref[...])
    bonus = _dot_sel(r * k2 * rk_ref[...], bd) * vr
    r_ref[...] = r
    w_ref[...] = log_w
    k_ref[...] = k2
    v_ref[...] = vr
    kk_ref[...] = kk
    kka_ref[...] = kk * a
    g_ref[...] = g
    bonus_ref[...] = bonus


def _rwkv_prep(zrw, weights, prev, *, B, nc, L, Lv, nb, row0):
    assert nb == 1 or nc == 1
    mu, wa0, w2a, g2, kkp, ka, rk, bd = weights
    has_state = prev is not None
    R = nb * L
    const2 = lambda b, c: (0, 0)
    in_specs = [pl.BlockSpec((R, RW_COLS), lambda b, c: (row0 + b * nc + c, 0)),
                pl.BlockSpec((1, RW_COLS), const2),
                pl.BlockSpec((1, 2 * RW_W), const2),
                pl.BlockSpec((RW_LORA_W + RW_LORA_A, 2 * RW_W), const2),
                pl.BlockSpec((RW_LORA_G, RW_W), const2),
                pl.BlockSpec((1, RW_W), const2),
                pl.BlockSpec((1, RW_W), const2),
                pl.BlockSpec((1, RW_W), const2),
                pl.BlockSpec((RW_W, RW_W), const2)]
    args = [zrw, mu, wa0, w2a, g2, kkp, ka, rk, bd]
    st = pl.BlockSpec((nb, 1, RW_COLS), lambda b, c: (b, 0, 0))
    if has_state:
        in_specs.append(st)
        args.append(prev)
    tok = pl.BlockSpec((R, RW_W), lambda b, c: (b * nc + c, 0))
    tok_shape = jax.ShapeDtypeStruct((B * nc * L, RW_W), F32)
    return pl.pallas_call(
        functools.partial(_rwkv_prep_kernel, L=L, Lv=Lv, nc=nc, nb=nb, has_state=has_state),
        grid=(B // nb, nc),
        in_specs=in_specs,
        out_specs=[tok] * 8 + [st],
        out_shape=[tok_shape] * 8 + [jax.ShapeDtypeStruct((B, 1, RW_COLS), F32)],
        scratch_shapes=[pltpu.VMEM((ROW_PAD + R, RW_COLS), F32)],
        compiler_params=_cparams(("parallel", "arbitrary")),
        name="rwkv_prep",
    )(*args)


def _rwkv_chunk_kernel(*refs, C, CB, Lv, nc, has_state):
    if has_state:
        (r_ref, lw_ref, k_ref, v_ref, kk_ref, kka_ref, g_ref, bonus_ref, gnw_ref, gnb_ref, s0_ref,
         y_ref, so_ref, s_scr) = refs
    else:
        (r_ref, lw_ref, k_ref, v_ref, kk_ref, kka_ref, g_ref, bonus_ref, gnw_ref, gnb_ref,
         y_ref, so_ref, s_scr) = refs
    c = pl.program_id(1)
    C2 = 2 * C
    chains = [(i, j) for i in range(CB) for j in range(RW_PAIRS)]
    nch = len(chains)
    sl = lambda j: slice(j * LANES, (j + 1) * LANES)

    @pl.when(c == 0)
    def _():
        if has_state:
            z = jnp.zeros((RW_DH, RW_DH), F32)
            for n, (i, j) in enumerate(chains):
                s_scr[n] = jnp.concatenate([jnp.concatenate([s0_ref[i, 2 * j], z], axis=1),
                                            jnp.concatenate([z, s0_ref[i, 2 * j + 1]], axis=1)], axis=0)
        else:
            s_scr[...] = jnp.zeros_like(s_scr)

    srow = lax.broadcasted_iota(jnp.int32, (LANES, LANES), 0)
    scol = lax.broadcasted_iota(jnp.int32, (LANES, LANES), 1)
    same_head = jnp.logical_or(jnp.logical_and(srow < RW_DH, scol < RW_DH),
                               jnp.logical_and(srow >= RW_DH, scol >= RW_DH))
    valid = lax.broadcasted_iota(jnp.int32, (C, LANES), 0) < Lv
    row = lax.broadcasted_iota(jnp.int32, (C2, C2), 0)
    col = lax.broadcasted_iota(jnp.int32, (C2, C2), 1)
    tt = row & (C - 1)
    ss = col & (C - 1)
    top = row < C
    bot = row >= C
    strict = ss < tt
    incl = ss <= tt
    keep0 = jnp.logical_or(jnp.logical_and(top, strict), jnp.logical_and(bot, incl))
    keep1 = jnp.logical_or(jnp.logical_and(top, incl), jnp.logical_and(bot, strict))
    tl = jnp.logical_and(top, col < C)
    br = jnp.logical_and(bot, col >= C)
    eye = (row == col).astype(F32)
    tri = lax.broadcasted_iota(jnp.int32, (C, C), 0) >= lax.broadcasted_iota(jnp.int32, (C, C), 1)
    left = lax.broadcasted_iota(jnp.int32, (C, LANES), 1) < RW_DH
    left2 = lax.broadcasted_iota(jnp.int32, (C2, LANES), 1) < RW_DH
    zeros = jnp.zeros((C, LANES), F32)

    def head_mean(x):
        lsum = jnp.sum(jnp.where(left, x, 0.0), axis=1, keepdims=True)
        rsum = jnp.sum(jnp.where(left, 0.0, x), axis=1, keepdims=True)
        return jnp.where(left, lsum, rsum) * (1.0 / RW_DH)

    ar, m0, m1, vs, bk_end, tots = [], [], [], [], [], []
    for i, j in chains:
        lw = jnp.where(valid, lw_ref[i, :, sl(j)], 0.0)
        cum = _sel_dot(tri, lw)
        tot = cum[C - 1:C, :]
        e_inv = jnp.exp(-cum)
        e_end = jnp.exp(tot - cum)
        kka = jnp.where(valid, kka_ref[i, :, sl(j)], 0.0)
        k = jnp.where(valid, k_ref[i, :, sl(j)], 0.0)
        a_t = -jnp.where(valid, kk_ref[i, :, sl(j)], 0.0) * jnp.exp(cum - lw)
        r_t = r_ref[i, :, sl(j)] * jnp.exp(cum)
        b_t = kka * e_inv
        k_t = k * e_inv
        ar_c = jnp.concatenate([a_t, r_t], axis=0)
        g0 = _dot_nt(jnp.where(left2, ar_c, 0.0), jnp.concatenate([b_t, k_t], axis=0))
        g1 = _dot_nt(jnp.where(left2, 0.0, jnp.concatenate([r_t, a_t], axis=0)), jnp.concatenate([k_t, b_t], axis=0))
        ar.append(ar_c.astype(BF16))
        m0.append(jnp.where(keep0, g0, 0.0))
        m1.append(jnp.where(keep1, g1, 0.0))
        vs.append(jnp.where(valid, v_ref[i, :, sl(j)], 0.0))
        bk_end.append(jnp.concatenate([kka * e_end, k * e_end], axis=0).astype(BF16))
        tots.append(tot)

    s_prev = [s_scr[n] for n in range(nch)]
    w_as = [_dot_nt(ar[n], s_prev[n]) for n in range(nch)]
    x0 = [_dot(m0[n][0:C, :], jnp.concatenate([zeros, vs[n]], axis=0)) for n in range(nch)]
    x1 = [_dot(m1[n][C:, :], jnp.concatenate([vs[n], zeros], axis=0)) for n in range(nch)]

    p = [jnp.where(tl, m0[n], jnp.where(br, m1[n], 0.0)) for n in range(nch)]
    t_inv = [eye + p[n] for n in range(nch)]
    step = 1
    while 2 * step < Lv:
        p = [_dot(p[n], p[n]) for n in range(nch)]
        t_inv = [t_inv[n] + _dot(t_inv[n], p[n]) for n in range(nch)]
        step *= 2

    rhs = [w_as[n][0:C, :] + jnp.where(left, x0[n], x1[n]) for n in range(nch)]
    tu = [_dot(t_inv[n], jnp.concatenate([rhs[n], rhs[n]], axis=0)) for n in range(nch)]
    u = [jnp.where(left, tu[n][0:C, :], tu[n][C:, :]) for n in range(nch)]
    uv = [jnp.concatenate([u[n], vs[n]], axis=0).astype(BF16) for n in range(nch)]
    vu = [jnp.concatenate([vs[n], u[n]], axis=0).astype(BF16) for n in range(nch)]
    o0 = [_dot(m0[n][C:, :], uv[n]) for n in range(nch)]
    o1 = [_dot(m1[n][0:C, :], vu[n]) for n in range(nch)]
    s_new = [_dot_tn(uv[n], bk_end[n]) for n in range(nch)]
    for n in range(nch):
        s_scr[n] = jnp.where(same_head, s_prev[n] * jnp.exp(tots[n]) + s_new[n], 0.0)

    for n, (i, j) in enumerate(chains):
        o = w_as[n][C:, :] + jnp.where(left, o0[n], o1[n])
        oc = o - head_mean(o)
        yn = oc * lax.rsqrt(head_mean(oc * oc) + RW_GN_EPS) * gnw_ref[:, sl(j)] + gnb_ref[:, sl(j)]
        y_ref[i, :, sl(j)] = ((yn + bonus_ref[i, :, sl(j)]) * g_ref[i, :, sl(j)]).astype(BF16)

    @pl.when(c == nc - 1)
    def _():
        for n, (i, j) in enumerate(chains):
            tile = s_scr[n]
            so_ref[i, 2 * j] = tile[0:RW_DH, 0:RW_DH]
            so_ref[i, 2 * j + 1] = tile[RW_DH:, RW_DH:]


def _rwkv_chunk_scan(toks, gnw, gnb, state, *, B, T, C, CB, Lv):
    nc = T // C
    has_state = state is not None
    tok = pl.BlockSpec((CB, C, RW_W), lambda b, c: (b, c, 0))
    const2 = lambda b, c: (0, 0)
    st = pl.BlockSpec((CB, RW_HEADS, RW_DH, RW_DH), lambda b, c: (b, 0, 0, 0))
    in_specs = [tok] * 8 + [pl.BlockSpec((1, RW_W), const2), pl.BlockSpec((1, RW_W), const2)]
    args = [t.reshape(B, T, RW_W) for t in toks] + [gnw, gnb]
    if has_state:
        in_specs.append(st)
        args.append(state)
    y, s = pl.pallas_call(
        functools.partial(_rwkv_chunk_kernel, C=C, CB=CB, Lv=Lv, nc=nc, has_state=has_state),
        grid=(B // CB, nc),
        in_specs=in_specs,
        out_specs=[tok, st],
        out_shape=[jax.ShapeDtypeStruct((B, T, RW_W), BF16),
                   jax.ShapeDtypeStruct((B, RW_HEADS, RW_DH, RW_DH), F32)],
        scratch_shapes=[pltpu.VMEM((CB * RW_PAIRS, LANES, LANES), F32)],
        compiler_params=_cparams(("parallel", "arbitrary")),
        name="rwkv_chunk",
    )(*args)
    return y.reshape(B * T, RW_W), s


def _merge_kernel(x_ref, zg_ref, yr_ref, ym_ref, yw_ref, wb_ref, wo_ref, o_ref):
    acc = None
    for n, y_ref in enumerate((yr_ref, ym_ref, yw_ref)):
        proj = jnp.dot(y_ref[...], wb_ref[n], preferred_element_type=F32)
        term = _sigmoid(zg_ref[:, n * D_MODEL:(n + 1) * D_MODEL]) * proj
        acc = term if acc is None else acc + term
    o_ref[...] = x_ref[...] + jnp.dot(acc.astype(BF16), wo_ref[...], preferred_element_type=F32)


def _merge(x, zgate, y_ret, y_ml, y_rw, w_branch, w_out, tm=256):
    n = x.shape[0]
    tokspec = lambda w: pl.BlockSpec((tm, w), lambda i: (i, 0))
    return pl.pallas_call(
        _merge_kernel,
        grid=(n // tm,),
        in_specs=[tokspec(D_MODEL), tokspec(N_BRANCH * D_MODEL), tokspec(RET_W), tokspec(ML_W), tokspec(RW_W),
                  pl.BlockSpec((N_BRANCH, RET_W, D_MODEL), lambda i: (0, 0, 0)),
                  pl.BlockSpec((D_MODEL, D_MODEL), lambda i: (0, 0))],
        out_specs=tokspec(D_MODEL),
        out_shape=jax.ShapeDtypeStruct((n, D_MODEL), F32),
        compiler_params=_cparams(("parallel",)),
        name="merge",
    )(x, zgate, y_ret, y_ml, y_rw, w_branch, w_out)


def _ffn_kernel(*refs, ne, routed, final):
    refs = list(refs)
    x_ref, g_ref = refs[:2]
    refs = refs[2:]
    router_ref = refs.pop(0) if routed else None
    wg_ref, wu_ref, wd_ref = refs[:3]
    refs = refs[3:]
    fin_ref = refs.pop(0) if final else None
    o_ref, h_scr, acc_scr, comb_scr = refs
    e = pl.program_id(1)

    @pl.when(e == 0)
    def _():
        h = _rms(x_ref[...], g_ref[...])
        h_scr[...] = h.astype(BF16)
        acc_scr[...] = jnp.zeros_like(acc_scr)
        if routed:
            lane = lax.broadcasted_iota(jnp.int32, (h.shape[0], LANES), 1)
            logits = jnp.where(lane < ne, _dot_hi(h, router_ref[...]), NEG_BIG)
            m1 = jnp.max(logits, axis=1, keepdims=True)
            i1 = jnp.min(jnp.where(logits == m1, lane, LANES), axis=1, keepdims=True)
            rest = jnp.where(lane == i1, NEG_BIG, logits)
            m2 = jnp.max(rest, axis=1, keepdims=True)
            i2 = jnp.min(jnp.where(rest == m2, lane, LANES), axis=1, keepdims=True)
            e2 = jnp.exp(m2 - m1)
            p1 = 1.0 / (1.0 + e2)
            comb_scr[...] = jnp.where(lane == i1, p1, 0.0) + jnp.where(lane == i2, e2 * p1, 0.0)

    h = h_scr[...]
    hg = jnp.dot(h, wg_ref[0], preferred_element_type=F32)
    hu = jnp.dot(h, wu_ref[0], preferred_element_type=F32)
    y = jnp.dot((_silu(hg) * hu).astype(BF16), wd_ref[0], preferred_element_type=F32)
    if routed:
        lane = lax.broadcasted_iota(jnp.int32, comb_scr.shape, 1)
        y = y * jnp.sum(jnp.where(lane == e, comb_scr[...], 0.0), axis=1, keepdims=True)
    acc_scr[...] += y

    @pl.when(e == ne - 1)
    def _():
        out = x_ref[...] + acc_scr[...]
        if final:
            out = _rms(out, fin_ref[...])
        o_ref[...] = out


def _ffn(x, g, wg, wu, wd, router=None, fin=None, tm=512):
    n = x.shape[0]
    ne, _, f = wg.shape
    routed = router is not None
    final = fin is not None
    in_specs = [pl.BlockSpec((tm, D_MODEL), lambda i, e: (i, 0)),
                pl.BlockSpec((1, D_MODEL), lambda i, e: (0, 0))]
    args = [x, g]
    if routed:
        in_specs.append(pl.BlockSpec((D_MODEL, LANES), lambda i, e: (0, 0)))
        args.append(router)
    in_specs += [pl.BlockSpec((1, D_MODEL, f), lambda i, e: (e, 0, 0)),
                 pl.BlockSpec((1, D_MODEL, f), lambda i, e: (e, 0, 0)),
                 pl.BlockSpec((1, f, D_MODEL), lambda i, e: (e, 0, 0))]
    args += [wg, wu, wd]
    if final:
        in_specs.append(pl.BlockSpec((1, D_MODEL), lambda i, e: (0, 0)))
        args.append(fin)
    return pl.pallas_call(
        functools.partial(_ffn_kernel, ne=ne, routed=routed, final=final),
        grid=(n // tm, ne),
        in_specs=in_specs,
        out_specs=pl.BlockSpec((tm, D_MODEL), lambda i, e: (i, 0)),
        out_shape=jax.ShapeDtypeStruct((n, D_MODEL), F32),
        scratch_shapes=[pltpu.VMEM((tm, D_MODEL), BF16), pltpu.VMEM((tm, D_MODEL), F32),
                        pltpu.VMEM((tm, LANES), F32)],
        compiler_params=_cparams(("parallel", "arbitrary")),
        name="ffn",
    )(*args)


def _row(v):
    return v.reshape(1, -1)


def kernel(x_prompt, x_sample, state_ret, state_mlstm_C, state_mlstm_n, state_mlstm_m, state_mlstm_conv, state_rwkv, state_rwkv_shift, norm_mix, w_in, ret_gn, ml_conv_w, ml_conv_b, ml_wq, ml_wk, ml_bi, ml_bf, ml_gn, ml_skip, rw_mu, rw_w0, rw_w2, rw_a0, rw_a2, rw_g2, rw_kk, rw_ka, rw_rk, rw_gn_w, rw_gn_b, w_branch, w_out, norm_ffn, ffn_w_gate, ffn_w_up, ffn_w_down, moe_router, moe_w_gate, moe_w_up, moe_w_down, final_norm):
    nc_p = SEQ // CHUNK
    xs = jnp.pad(x_sample, ((0, 0), (0, S_PAD - DEC_SEQ), (0, 0)))
    x = jnp.concatenate([x_prompt.reshape(N_PROMPT, D_MODEL), xs.reshape(N_SAMPLE, D_MODEL)], axis=0)

    pos_p = jnp.arange(SEQ, dtype=jnp.int32)
    pos_s = PAST_LEN + jnp.arange(S_PAD, dtype=jnp.int32)
    ret_tab_p = _retention_tables(CHUNK, CHUNK, pos_p, 1)
    ret_tab_s = _retention_tables(S_PAD, DEC_SEQ, pos_s, RET_NB_SAMPLE)
    head_of = jnp.arange(RW_W) // RW_DH
    bd64 = (head_of[:, None] == head_of[None, :]).astype(BF16)

    o_rq = 0
    o_mx = o_rq + RET_COLS
    o_mi = o_mx + 2 * ML_W
    o_mo = o_mi + 2 * ML_HEADS
    o_rw = o_mo + ML_W
    o_gate = o_rw + RW_COLS

    new_p, new_s = [], []
    for l in range(DEPTH):
        w = w_in[l]
        w_ret = w[:, o_rq:o_mx].astype(BF16)
        w_ml = jnp.concatenate([w[:, o_mx:o_mi], w[:, o_mo:o_rw], w[:, o_mi:o_mo],
                                jnp.zeros((D_MODEL, ML_GATE_W - 2 * ML_HEADS), F32)], axis=1).astype(BF16)
        w_rw = w[:, o_rw:o_gate].astype(BF16)
        w_gate = w[:, o_gate:].astype(BF16)
        g_mix = _row(norm_mix[l])
        zret = _norm_matmul(x, g_mix, w_ret)
        zml = _norm_matmul(x, g_mix, w_ml)
        zrw = _norm_matmul(x, g_mix, w_rw)
        zgate = _norm_matmul(x, g_mix, w_gate)

        gn = _row(ret_gn[l])
        yr_p, ret_p = _retention(zret, ret_tab_p, gn, None, B=BATCH, nc=nc_p, L=CHUNK, nb=1, row0=0)
        yr_s, ret_s = _retention(zret, ret_tab_s, gn, state_ret[l], B=DEC_BATCH, nc=1, L=S_PAD, nb=RET_NB_SAMPLE,
                                 row0=N_PROMPT // (RET_NB_SAMPLE * S_PAD))

        gate_bias = jnp.concatenate([ml_bi[l], ml_bf[l], jnp.zeros((ML_GATE_W - 2 * ML_HEADS,), F32)])
        ml_weights = (ml_conv_w[l], _row(ml_conv_b[l]), ml_wq[l].astype(BF16), ml_wk[l].astype(BF16),
                      _row(gate_bias), _row(ml_gn[l]), _row(ml_skip[l]))
        m0 = jnp.pad(state_mlstm_m[l], ((0, 0), (0, LANES - ML_HEADS))).reshape(DEC_BATCH, 1, LANES)
        ym_p, c_p, n_p, m_p, buf_p = _mlstm(zml, ml_weights, None, B=BATCH, nc=nc_p, L=CHUNK, Lv=CHUNK, nb=1,
                                            row0=0)
        ym_s, c_s, n_s, m_s, buf_s = _mlstm(zml, ml_weights,
                                            (state_mlstm_C[l], state_mlstm_n[l], m0, state_mlstm_conv[l]),
                                            B=DEC_BATCH, nc=1, L=S_PAD, Lv=DEC_SEQ, nb=ML_NB_SAMPLE,
                                            row0=N_PROMPT // (ML_NB_SAMPLE * S_PAD))

        w2a = jnp.zeros((RW_LORA_W + RW_LORA_A, 2 * RW_W), F32)
        w2a = w2a.at[:RW_LORA_W, :RW_W].set(rw_w2[l]).at[RW_LORA_W:, RW_W:].set(rw_a2[l])
        rw_weights = (_row(rw_mu[l]), _row(jnp.concatenate([rw_w0[l], rw_a0[l]])), w2a.astype(BF16),
                      rw_g2[l].astype(BF16), _row(rw_kk[l]), _row(rw_ka[l]), _row(rw_rk[l]), bd64)
        *tok_p, shift_p = _rwkv_prep(zrw, rw_weights, None, B=BATCH, nc=nc_p, L=CHUNK, Lv=CHUNK, nb=1, row0=0)
        *tok_s, shift_s = _rwkv_prep(zrw, rw_weights, state_rwkv_shift[l].reshape(DEC_BATCH, 1, RW_COLS),
                                     B=DEC_BATCH, nc=1, L=S_PAD, Lv=DEC_SEQ, nb=RWP_NB_SAMPLE,
                                     row0=N_PROMPT // (RWP_NB_SAMPLE * S_PAD))
        gnw, gnb = _row(rw_gn_w[l]), _row(rw_gn_b[l])
        yw_p, rws_p = _rwkv_chunk_scan(tok_p, gnw, gnb, None, B=BATCH, T=SEQ, C=RW_C, CB=RW_CB_PROMPT, Lv=RW_C)
        yw_s, rws_s = _rwkv_chunk_scan(tok_s, gnw, gnb, state_rwkv[l], B=DEC_BATCH, T=S_PAD, C=S_PAD,
                                       CB=RW_CB_SAMPLE, Lv=DEC_SEQ)

        y_ret = jnp.concatenate([yr_p, yr_s], axis=0)
        y_ml = jnp.concatenate([ym_p, ym_s], axis=0)
        y_rw = jnp.concatenate([yw_p, yw_s], axis=0)
        x = _merge(x, zgate, y_ret, y_ml, y_rw, w_branch[l].astype(BF16), w_out[l].astype(BF16))

        i = l // 2
        g_ffn = _row(norm_ffn[l])
        fin = _row(final_norm) if l == DEPTH - 1 else None
        if l % 2 == 0:
            halves = D_FF // D_FF_EXPERT
            wg = ffn_w_gate[i].reshape(D_MODEL, halves, D_FF_EXPERT).transpose(1, 0, 2).astype(BF16)
            wu = ffn_w_up[i].reshape(D_MODEL, halves, D_FF_EXPERT).transpose(1, 0, 2).astype(BF16)
            wd = ffn_w_down[i].reshape(halves, D_FF_EXPERT, D_MODEL).astype(BF16)
            x = _ffn(x, g_ffn, wg, wu, wd, None, fin)
        else:
            router = jnp.pad(moe_router[i], ((0, 0), (0, LANES - N_EXPERTS)))
            x = _ffn(x, g_ffn, moe_w_gate[i].astype(BF16), moe_w_up[i].astype(BF16), moe_w_down[i].astype(BF16),
                     router, fin)

        new_p.append((ret_p, c_p, n_p, m_p[:, 0, :ML_HEADS], buf_p, rws_p, shift_p[:, 0]))
        new_s.append((ret_s, c_s, n_s, m_s[:, 0, :ML_HEADS], buf_s, rws_s, shift_s[:, 0]))

    y_prompt = x[:N_PROMPT].reshape(BATCH, SEQ, D_MODEL)
    y_sample = x[N_PROMPT:].reshape(DEC_BATCH, S_PAD, D_MODEL)[:, :DEC_SEQ]
    st_p = tuple(jnp.stack([st[j] for st in new_p]) for j in range(7))
    st_s = tuple(jnp.stack([st[j] for st in new_s]) for j in range(7))
    return (y_prompt, y_sample) + st_p + st_s
```

```python
import functools

import jax
import jax.numpy as jnp
from jax import lax
from jax.experimental import pallas as pl
from jax.experimental.pallas import tpu as pltpu

D_MODEL = 1024
BATCH = 8
SEQ = 2048
DEPTH = 2
DEC_BATCH = 128
DEC_SEQ = 4
PAST_LEN = 16384
RET_HEADS = 4
RET_DK = 64
RET_DV = 128
ML_HEADS = 4
ML_DH = 128
CONV_W = 4
RW_HEADS = 8
RW_DH = 64
RW_LORA_W = 64
RW_LORA_A = 64
RW_LORA_G = 128
RET_W = RET_HEADS * RET_DV
ML_W = ML_HEADS * ML_DH
RW_W = RW_HEADS * RW_DH
N_BRANCH = 3
RW_COLS = 3 * RW_W + RW_LORA_W + RW_LORA_A + RW_LORA_G
D_FF = 2816
N_EXPERTS = 8
D_FF_EXPERT = 1408
CHUNK = 128
NORM_EPS = 1e-6
GN_EPS = 1e-5
RW_GN_EPS = 64e-5
ROPE_BASE = 10000.0
RW_DECAY_SCALE = 0.606531

LANES = 128
SUBLANES = 8
S_PAD = 16
N_PROMPT = BATCH * SEQ
N_SAMPLE = DEC_BATCH * S_PAD
N_TOK = N_PROMPT + N_SAMPLE
ML_GATE_W = LANES
ML_COLS = 3 * ML_W + ML_GATE_W
RET_COLS = 2 * RET_HEADS * RET_DK + 2 * RET_W
NEG_BIG = -1e30
VMEM_LIMIT = 56 * 1024 * 1024

RW_C = 64
RW_PAIRS = RW_HEADS // 2
RW_CB_PROMPT = 4
RW_CB_SAMPLE = 4
RET_NB_SAMPLE = 8
ML_NB_SAMPLE = 4
RWP_NB_SAMPLE = 8

F32 = jnp.float32
BF16 = jnp.bfloat16
HI = lax.Precision.HIGHEST


def _cparams(sem):
    return pltpu.CompilerParams(dimension_semantics=sem, vmem_limit_bytes=VMEM_LIMIT)


def _sigmoid(x):
    return 1.0 / (1.0 + jnp.exp(-x))


def _silu(x):
    return x * _sigmoid(x)


def _rms(x, g):
    return x * lax.rsqrt(jnp.mean(x * x, axis=-1, keepdims=True) + NORM_EPS) * g


def _dot(a, b):
    return jnp.dot(a.astype(BF16), b.astype(BF16), preferred_element_type=F32)


def _dot_nt(a, b):
    return lax.dot_general(a.astype(BF16), b.astype(BF16), (((1,), (1,)), ((), ())), preferred_element_type=F32)


def _dot_tn(a, b):
    return lax.dot_general(a.astype(BF16), b.astype(BF16), (((0,), (0,)), ((), ())), preferred_element_type=F32)


def _dot_hi(a, b):
    return jnp.dot(a, b, preferred_element_type=F32, precision=HI)


def _split(x):
    hi = x.astype(BF16)
    return hi, (x - hi.astype(F32)).astype(BF16)


def _as_bf16_mask(m):
    return m if m.dtype == BF16 else m.astype(F32).astype(BF16)


def _sel_dot(m, x):
    hi, lo = _split(x)
    m = _as_bf16_mask(m)
    return jnp.dot(m, hi, preferred_element_type=F32) + jnp.dot(m, lo, preferred_element_type=F32)


def _dot_sel(x, m):
    hi, lo = _split(x)
    m = _as_bf16_mask(m)
    return jnp.dot(hi, m, preferred_element_type=F32) + jnp.dot(lo, m, preferred_element_type=F32)


def _sel_dot_nt(m, x):
    hi, lo = _split(x)
    m = _as_bf16_mask(m)
    dn = (((1,), (1,)), ((), ()))
    return (lax.dot_general(m, hi, dn, preferred_element_type=F32)
            + lax.dot_general(m, lo, dn, preferred_element_type=F32))


def _norm_matmul_kernel(x_ref, g_ref, w_ref, o_ref):
    h = _rms(x_ref[...], g_ref[...])
    o_ref[...] = jnp.dot(h.astype(BF16), w_ref[...], preferred_element_type=F32)


def _norm_matmul(x, g, w, tm=256):
    n, d = x.shape
    c = w.shape[1]
    return pl.pallas_call(
        _norm_matmul_kernel,
        grid=(n // tm,),
        in_specs=[pl.BlockSpec((tm, d), lambda i: (i, 0)),
                  pl.BlockSpec((1, d), lambda i: (0, 0)),
                  pl.BlockSpec((d, c), lambda i: (0, 0))],
        out_specs=pl.BlockSpec((tm, c), lambda i: (i, 0)),
        out_shape=jax.ShapeDtypeStruct((n, c), F32),
        compiler_params=_cparams(("parallel",)),
        name="norm_matmul",
    )(x, g, w)


def _retention_kernel(*refs, L, nc, nb, has_state):
    if has_state:
        (q_ref, k_ref, v_ref, g_ref, cos_ref, sin_ref, di_ref, dq_ref, dk_ref, dc_ref, gn_ref, s0_ref,
         y_ref, so_ref, s_scr) = refs
    else:
        (q_ref, k_ref, v_ref, g_ref, cos_ref, sin_ref, di_ref, dq_ref, dk_ref, dc_ref, gn_ref,
         y_ref, so_ref, s_scr) = refs
    c = pl.program_id(1)

    @pl.when(c == 0)
    def _():
        if has_state:
            s_scr[...] = s0_ref[...]
        else:
            s_scr[...] = jnp.zeros_like(s_scr)

    qk_w = RET_HEADS * RET_DK
    half = RET_DK // 2
    lane = lax.broadcasted_iota(jnp.int32, (nb * L, qk_w), 1)
    first_half = (lane % RET_DK) < half
    cos = cos_ref[...]
    sin = sin_ref[...]

    def rot(x):
        swapped = jnp.where(first_half, pltpu.roll(x, qk_w - half, 1), pltpu.roll(x, half, 1))
        return x * cos + swapped * sin

    q = rot(q_ref[...])
    k = rot(k_ref[...]) * (RET_DK ** -0.5)
    dq = dq_ref[...]
    dk = dk_ref[...]
    dc = dc_ref[...]
    chains = [(i, h) for i in range(nb) for h in range(RET_HEADS)]
    rows = lambda i: slice(i * L, (i + 1) * L)
    kcols = lambda h: slice(h * RET_DK, (h + 1) * RET_DK)
    vcols = lambda h: slice(h * RET_DV, (h + 1) * RET_DV)

    qh = [q[rows(i), kcols(h)].astype(BF16) for i, h in chains]
    kh = [k[rows(i), kcols(h)] for i, h in chains]
    vh = [v_ref[rows(i), vcols(h)].astype(BF16) for i, h in chains]
    s = [_dot_nt(qh[n], kh[n]) * di_ref[h] for n, (i, h) in enumerate(chains)]
    s_prev = [s_scr[i, h] for i, h in chains]
    cross = [_dot(qh[n], s_prev[n]) for n in range(len(chains))]
    upd = [_dot_tn(kh[n] * dk[:, h:h + 1], vh[n]) for n, (i, h) in enumerate(chains)]
    inner = [_dot(s[n], vh[n]) for n in range(len(chains))]
    for n, (i, h) in enumerate(chains):
        s_scr[i, h] = s_prev[n] * dc[:, h:h + 1] + upd[n]
        o = inner[n] + cross[n] * dq[:, h:h + 1]
        oc = o - jnp.mean(o, axis=-1, keepdims=True)
        yn = oc * lax.rsqrt(jnp.mean(oc * oc, axis=-1, keepdims=True) + GN_EPS) * gn_ref[:, vcols(h)]
        y_ref[rows(i), vcols(h)] = (_silu(g_ref[rows(i), vcols(h)]) * yn).astype(BF16)

    @pl.when(c == nc - 1)
    def _():
        so_ref[...] = s_scr[...]


def _retention(zret, tables, gn, state, *, B, nc, L, nb, row0):
    cos, sin, di, dq, dk, dc = tables
    has_state = state is not None
    qk_w = RET_HEADS * RET_DK
    R = nb * L
    row = lambda b, c: row0 + b * nc + c
    st = pl.BlockSpec((nb, RET_HEADS, RET_DK, RET_DV), lambda b, c: (b, 0, 0, 0))
    in_specs = [pl.BlockSpec((R, qk_w), lambda b, c: (row(b, c), 0)),
                pl.BlockSpec((R, qk_w), lambda b, c: (row(b, c), 1)),
                pl.BlockSpec((R, RET_W), lambda b, c: (row(b, c), 1)),
                pl.BlockSpec((R, RET_W), lambda b, c: (row(b, c), 2)),
                pl.BlockSpec((R, qk_w), lambda b, c: (c, 0)),
                pl.BlockSpec((R, qk_w), lambda b, c: (c, 0)),
                pl.BlockSpec((RET_HEADS, L, L), lambda b, c: (0, 0, 0)),
                pl.BlockSpec((L, RET_HEADS), lambda b, c: (0, 0)),
                pl.BlockSpec((L, RET_HEADS), lambda b, c: (0, 0)),
                pl.BlockSpec((1, RET_HEADS), lambda b, c: (0, 0)),
                pl.BlockSpec((1, RET_W), lambda b, c: (0, 0))]
    args = [zret, zret, zret, zret, cos, sin, di, dq, dk, dc, gn]
    if has_state:
        in_specs.append(st)
        args.append(state)
    return pl.pallas_call(
        functools.partial(_retention_kernel, L=L, nc=nc, nb=nb, has_state=has_state),
        grid=(B // nb, nc),
        in_specs=in_specs,
        out_specs=[pl.BlockSpec((R, RET_W), lambda b, c: (b * nc + c, 0)), st],
        out_shape=[jax.ShapeDtypeStruct((B * nc * L, RET_W), BF16),
                   jax.ShapeDtypeStruct((B, RET_HEADS, RET_DK, RET_DV), F32)],
        scratch_shapes=[pltpu.VMEM((nb, RET_HEADS, RET_DK, RET_DV), F32)],
        compiler_params=_cparams(("parallel", "arbitrary")),
        name="retention",
    )(*args)


def _retention_tables(L, Lv, pos, nb):
    half = RET_DK // 2
    inv = ROPE_BASE ** (-jnp.arange(half, dtype=F32) / half)
    ang = pos.astype(F32)[:, None] * inv[None, :]
    cos = jnp.tile(jnp.concatenate([jnp.cos(ang), jnp.cos(ang)], axis=1), (nb, RET_HEADS))
    sin = jnp.tile(jnp.concatenate([-jnp.sin(ang), jnp.sin(ang)], axis=1), (nb, RET_HEADS))
    log_gamma = jnp.log1p(-(2.0 ** (-5.0 - jnp.arange(RET_HEADS, dtype=F32))))
    idx = jnp.arange(L, dtype=F32)
    diff = idx[:, None] - idx[None, :]
    causal = diff >= 0
    di = jnp.where(causal[None], jnp.exp(jnp.where(causal, diff, 0.0)[None] * log_gamma[:, None, None]), 0.0)
    dq = jnp.exp((idx[:, None] + 1.0) * log_gamma[None, :])
    dk = jnp.where((idx < Lv)[:, None], jnp.exp((Lv - 1.0 - idx)[:, None] * log_gamma[None, :]), 0.0)
    dc = jnp.exp(Lv * log_gamma)[None, :]
    return cos, sin, di, dq, dk, dc


ROW_PAD = SUBLANES


def _mlstm_kernel(*refs, L, Lv, nc, nb, has_state):
    if has_state:
        (mx_ref, mv_ref, mo_ref, gz_ref, cw_ref, cb_ref, wq_ref, wk_ref, gb_ref, gn_ref, skip_ref,
         c0_ref, n0_ref, m0_ref, buf0_ref,
         y_ref, co_ref, no_ref, mout_ref, bufo_ref, xp_scr, c_scr, n_scr, m_scr) = refs
    else:
        (mx_ref, mv_ref, mo_ref, gz_ref, cw_ref, cb_ref, wq_ref, wk_ref, gb_ref, gn_ref, skip_ref,
         y_ref, co_ref, no_ref, mout_ref, bufo_ref, xp_scr, c_scr, n_scr, m_scr) = refs
    c = pl.program_id(1)
    tail = CONV_W - 1

    @pl.when(c == 0)
    def _():
        for i in range(nb):
            xp_scr[i, 0:ROW_PAD, :] = jnp.zeros((ROW_PAD, ML_W), F32)
        if has_state:
            c_scr[...] = c0_ref[...]
            n_scr[...] = n0_ref[...]
            m_scr[...] = m0_ref[...]
            for i in range(nb):
                xp_scr[i, ROW_PAD - tail:ROW_PAD, :] = buf0_ref[i]
        else:
            c_scr[...] = jnp.zeros_like(c_scr)
            n_scr[...] = jnp.zeros_like(n_scr)
            m_scr[...] = jnp.zeros_like(m_scr)

    rows = lambda i: slice(i * L, (i + 1) * L)
    cols = lambda h: slice(h * ML_DH, (h + 1) * ML_DH)
    lane = lax.broadcasted_iota(jnp.int32, (L, ML_GATE_W), 1)
    rowi = lax.broadcasted_iota(jnp.int32, (L, ML_GATE_W), 0)
    tri = lax.broadcasted_iota(jnp.int32, (L, L), 0) >= lax.broadcasted_iota(jnp.int32, (L, L), 1)
    eye = (lax.broadcasted_iota(jnp.int32, (SUBLANES, ML_GATE_W), 0)
           == lax.broadcasted_iota(jnp.int32, (SUBLANES, ML_GATE_W), 1))

    xc, new_tail, gc, fc, g_rows, f_rows = [], [], [], [], [], []
    for i in range(nb):
        xp_scr[i, ROW_PAD:ROW_PAD + L, :] = mx_ref[rows(i), :]
        acc = cb_ref[...]
        for j in range(CONV_W):
            acc = acc + cw_ref[j:j + 1, :] * xp_scr[i, ROW_PAD - tail + j:ROW_PAD - tail + j + L, :]
        xc.append(_silu(acc))
        new_tail.append(xp_scr[i, ROW_PAD + Lv - tail:ROW_PAD + Lv, :])
        xp_scr[i, ROW_PAD - tail:ROW_PAD, :] = new_tail[i]
        gz = gz_ref[rows(i), :] + gb_ref[...]
        logsig = jnp.minimum(gz, 0.0) - jnp.log1p(jnp.exp(-jnp.abs(gz)))
        g = jnp.where(lane < ML_HEADS, gz, logsig)
        if Lv < L:
            g = jnp.where(rowi < Lv, g, jnp.where(lane < ML_HEADS, NEG_BIG, 0.0))
        gc.append(g)
        fc.append(_sel_dot(tri, g))
        g_rows.append(_sel_dot_nt(eye, g))
        f_rows.append(_sel_dot_nt(eye, fc[i]))

    chains = [(i, h) for i in range(nb) for h in range(ML_HEADS)]
    nch = len(chains)
    xh = [xc[i][:, cols(h)] for i, h in chains]
    xh_b = [x.astype(BF16) for x in xh]
    q = [jnp.dot(xh_b[n], wq_ref[h], preferred_element_type=F32) for n, (i, h) in enumerate(chains)]
    k = [jnp.dot(xh_b[n], wk_ref[h], preferred_element_type=F32) * (ML_DH ** -0.5) for n, (i, h) in enumerate(chains)]
    q_b = [x.astype(BF16) for x in q]
    v_b = [mv_ref[rows(i), cols(h)].astype(BF16) for i, h in chains]
    qk = [_dot_nt(q_b[n], k[n]) for n in range(nch)]
    c_prev = [c_scr[i, h] for i, h in chains]
    qc = [_dot(q_b[n], c_prev[n]) for n in range(nch)]

    s, a_inter, m_t, kw, a_old, m_new = [], [], [], [], [], []
    for n, (i, h) in enumerate(chains):
        f_col = fc[i][:, ML_HEADS + h:ML_HEADS + h + 1]
        i_col = gc[i][:, h:h + 1]
        f_row = f_rows[i][ML_HEADS + h:ML_HEADS + h + 1, :]
        i_row = g_rows[i][h:h + 1, :]
        m_prev = m_scr[i, 0:1, h:h + 1]
        dlog = jnp.where(tri, f_col - f_row + i_row, NEG_BIG)
        inter = f_col + m_prev
        m_t.append(jnp.maximum(jnp.max(dlog, axis=1, keepdims=True), inter))
        s.append(qk[n] * jnp.exp(dlog - m_t[n]))
        a_inter.append(jnp.exp(inter - m_t[n]))
        f_last = fc[i][Lv - 1:Lv, ML_HEADS + h:ML_HEADS + h + 1]
        m_new.append(jnp.maximum(f_last + m_prev, jnp.max(f_last - f_row + i_row, axis=1, keepdims=True)))
        kw.append(k[n] * jnp.exp(f_last - f_col + i_col - m_new[n]))
        a_old.append(jnp.exp(f_last + m_prev - m_new[n]))

    sv = [_dot(s[n], v_b[n]) for n in range(nch)]
    ktv = [_dot_tn(kw[n], v_b[n]) for n in range(nch)]
    for n, (i, h) in enumerate(chains):
        n_prev = n_scr[i, h:h + 1, :]
        num = sv[n] + a_inter[n] * qc[n]
        den = jnp.sum(s[n], axis=1, keepdims=True) + a_inter[n] * jnp.sum(q[n] * n_prev, axis=1, keepdims=True)
        hh = num * (1.0 / jnp.maximum(jnp.abs(den), jnp.exp(-m_t[n])))
        c_scr[i, h] = a_old[n] * c_prev[n] + ktv[n]
        n_scr[i, h:h + 1, :] = a_old[n] * n_prev + jnp.sum(kw[n], axis=0, keepdims=True)
        m_scr[i, 0:1, h:h + 1] = m_new[n]
        hc = hh - jnp.mean(hh, axis=-1, keepdims=True)
        hn = hc * lax.rsqrt(jnp.mean(hc * hc, axis=-1, keepdims=True) + GN_EPS) * gn_ref[:, cols(h)]
        y_ref[rows(i), cols(h)] = (_sigmoid(mo_ref[rows(i), cols(h)]) * (hn + skip_ref[:, cols(h)] * xh[n])).astype(BF16)

    @pl.when(c == nc - 1)
    def _():
        co_ref[...] = c_scr[...]
        no_ref[...] = n_scr[...]
        mout_ref[...] = m_scr[...]
        for i in range(nb):
            bufo_ref[i] = new_tail[i]


def _mlstm(zml, weights, state, *, B, nc, L, Lv, nb, row0):
    cw, cb, wq, wk, gb, gn, skip = weights
    has_state = state is not None
    R = nb * L
    row = lambda b, c: row0 + b * nc + c
    const2 = lambda b, c: (0, 0)
    in_specs = [pl.BlockSpec((R, ML_W), lambda b, c: (row(b, c), 0)),
                pl.BlockSpec((R, ML_W), lambda b, c: (row(b, c), 1)),
                pl.BlockSpec((R, ML_W), lambda b, c: (row(b, c), 2)),
                pl.BlockSpec((R, ML_GATE_W), lambda b, c: (row(b, c), 3 * ML_W // ML_GATE_W)),
                pl.BlockSpec((CONV_W, ML_W), const2),
                pl.BlockSpec((1, ML_W), const2),
                pl.BlockSpec((ML_HEADS, ML_DH, ML_DH), lambda b, c: (0, 0, 0)),
                pl.BlockSpec((ML_HEADS, ML_DH, ML_DH), lambda b, c: (0, 0, 0)),
                pl.BlockSpec((1, ML_GATE_W), const2),
                pl.BlockSpec((1, ML_W), const2),
                pl.BlockSpec((1, ML_W), const2)]
    args = [zml, zml, zml, zml, cw, cb, wq, wk, gb, gn, skip]
    st_specs = [pl.BlockSpec((nb, ML_HEADS, ML_DH, ML_DH), lambda b, c: (b, 0, 0, 0)),
                pl.BlockSpec((nb, ML_HEADS, ML_DH), lambda b, c: (b, 0, 0)),
                pl.BlockSpec((nb, 1, LANES), lambda b, c: (b, 0, 0)),
                pl.BlockSpec((nb, CONV_W - 1, ML_W), lambda b, c: (b, 0, 0))]
    if has_state:
        in_specs += st_specs
        args += list(state)
    return pl.pallas_call(
        functools.partial(_mlstm_kernel, L=L, Lv=Lv, nc=nc, nb=nb, has_state=has_state),
        grid=(B // nb, nc),
        in_specs=in_specs,
        out_specs=[pl.BlockSpec((R, ML_W), lambda b, c: (b * nc + c, 0))] + st_specs,
        out_shape=[jax.ShapeDtypeStruct((B * nc * L, ML_W), BF16),
                   jax.ShapeDtypeStruct((B, ML_HEADS, ML_DH, ML_DH), F32),
                   jax.ShapeDtypeStruct((B, ML_HEADS, ML_DH), F32),
                   jax.ShapeDtypeStruct((B, 1, LANES), F32),
                   jax.ShapeDtypeStruct((B, CONV_W - 1, ML_W), F32)],
        scratch_shapes=[pltpu.VMEM((nb, ROW_PAD + L, ML_W), F32),
                        pltpu.VMEM((nb, ML_HEADS, ML_DH, ML_DH), F32),
                        pltpu.VMEM((nb, ML_HEADS, ML_DH), F32),
                        pltpu.VMEM((nb, 1, LANES), F32)],
        compiler_params=_cparams(("parallel", "arbitrary")),
        name="mlstm",
    )(*args)


def _rwkv_prep_kernel(*refs, L, Lv, nc, nb, has_state):
    if has_state:
        (z_ref, mu_ref, wa0_ref, w2a_ref, g2_ref, kkp_ref, ka_ref, rk_ref, bd_ref, prev_ref,
         r_ref, w_ref, k_ref, v_ref, kk_ref, kka_ref, g_ref, bonus_ref, shift_ref, xs_scr) = refs
    else:
        (z_ref, mu_ref, wa0_ref, w2a_ref, g2_ref, kkp_ref, ka_ref, rk_ref, bd_ref,
         r_ref, w_ref, k_ref, v_ref, kk_ref, kka_ref, g_ref, bonus_ref, shift_ref, xs_scr) = refs
    c = pl.program_id(1)
    R = nb * L

    @pl.when(c == 0)
    def _():
        xs_scr[0:ROW_PAD, :] = jnp.zeros((ROW_PAD, RW_COLS), F32)
        if has_state and nb == 1:
            xs_scr[ROW_PAD - 1:ROW_PAD, :] = prev_ref[0]

    z = z_ref[...]
    xs_scr[ROW_PAD:ROW_PAD + R, :] = z
    zp = xs_scr[ROW_PAD - 1:ROW_PAD - 1 + R, :]
    if nb > 1:
        first = [prev_ref[i] if has_state else jnp.zeros((1, RW_COLS), F32) for i in range(nb)]
        prev_rows = jnp.concatenate([jnp.broadcast_to(f, (L, RW_COLS)) for f in first], axis=0)
        rowi = lax.broadcasted_iota(jnp.int32, (R, RW_COLS), 0)
        zp = jnp.where((rowi & (L - 1)) == 0, prev_rows, zp)
    zs = z + (zp - z) * mu_ref[...]
    last = [z[i * L + Lv - 1:i * L + Lv, :] for i in range(nb)]
    if nb == 1:
        xs_scr[ROW_PAD - 1:ROW_PAD, :] = last[0]

    @pl.when(c == nc - 1)
    def _():
        for i in range(nb):
            shift_ref[i] = last[i]

    r = zs[:, 0:RW_W]
    kr = zs[:, RW_W:2 * RW_W]
    vr = zs[:, 2 * RW_W:3 * RW_W]
    wa = zs[:, 3 * RW_W:3 * RW_W + RW_LORA_W + RW_LORA_A]
    gl = zs[:, 3 * RW_W + RW_LORA_W + RW_LORA_A:]
    lane = lax.broadcasted_iota(jnp.int32, wa.shape, 1)
    wa_in = jnp.where(lane < RW_LORA_W, jnp.tanh(wa), wa)
    lora = jnp.dot(wa_in.astype(BF16), w2a_ref[...], preferred_element_type=F32) + wa0_ref[...]
    log_w = -RW_DECAY_SCALE * _sigmoid(lora[:, 0:RW_W])
    a = _sigmoid(lora[:, RW_W:])
    g = jnp.dot(_sigmoid(gl).astype(BF16), g2_ref[...], preferred_element_type=F32)
    bd = bd_ref[...]
    kk = kr * kkp_ref[...]
    kk = kk / jnp.maximum(jnp.sqrt(_dot_sel(kk * kk, bd)), 1e-12)
    k2 = kr * (1.0 + (a - 1.0) * ka_ref[...])
    bonus = _dot_sel(r * k2 * rk_ref[...], bd) * vr
    r_ref[...] = r
    w_ref[...] = log_w
    k_ref[...] = k2
    v_ref[...] = vr
    kk_ref[...] = kk
    kka_ref[...] = kk * a
    g_ref[...] = g
    bonus_ref[...] = bonus


def _rwkv_prep(zrw, weights, prev, *, B, nc, L, Lv, nb, row0):
    assert nb == 1 or nc == 1
    mu, wa0, w2a, g2, kkp, ka, rk, bd = weights
    has_state = prev is not None
    R = nb * L
    const2 = lambda b, c: (0, 0)
    in_specs = [pl.BlockSpec((R, RW_COLS), lambda b, c: (row0 + b * nc + c, 0)),
                pl.BlockSpec((1, RW_COLS), const2),
                pl.BlockSpec((1, 2 * RW_W), const2),
                pl.BlockSpec((RW_LORA_W + RW_LORA_A, 2 * RW_W), const2),
                pl.BlockSpec((RW_LORA_G, RW_W), const2),
                pl.BlockSpec((1, RW_W), const2),
                pl.BlockSpec((1, RW_W), const2),
                pl.BlockSpec((1, RW_W), const2),
                pl.BlockSpec((RW_W, RW_W), const2)]
    args = [zrw, mu, wa0, w2a, g2, kkp, ka, rk, bd]
    st = pl.BlockSpec((nb, 1, RW_COLS), lambda b, c: (b, 0, 0))
    if has_state:
        in_specs.append(st)
        args.append(prev)
    tok = pl.BlockSpec((R, RW_W), lambda b, c: (b * nc + c, 0))
    tok_shape = jax.ShapeDtypeStruct((B * nc * L, RW_W), F32)
    return pl.pallas_call(
        functools.partial(_rwkv_prep_kernel, L=L, Lv=Lv, nc=nc, nb=nb, has_state=has_state),
        grid=(B // nb, nc),
        in_specs=in_specs,
        out_specs=[tok] * 8 + [st],
        out_shape=[tok_shape] * 8 + [jax.ShapeDtypeStruct((B, 1, RW_COLS), F32)],
        scratch_shapes=[pltpu.VMEM((ROW_PAD + R, RW_COLS), F32)],
        compiler_params=_cparams(("parallel", "arbitrary")),
        name="rwkv_prep",
    )(*args)


def _rwkv_chunk_kernel(*refs, C, CB, Lv, nc, has_state):
    if has_state:
        (r_ref, lw_ref, k_ref, v_ref, kk_ref, kka_ref, g_ref, bonus_ref, gnw_ref, gnb_ref, s0_ref,
         y_ref, so_ref, s_scr) = refs
    else:
        (r_ref, lw_ref, k_ref, v_ref, kk_ref, kka_ref, g_ref, bonus_ref, gnw_ref, gnb_ref,
         y_ref, so_ref, s_scr) = refs
    c = pl.program_id(1)
    C2 = 2 * C
    chains = [(i, j) for i in range(CB) for j in range(RW_PAIRS)]
    nch = len(chains)
    sl = lambda j: slice(j * LANES, (j + 1) * LANES)

    @pl.when(c == 0)
    def _():
        if has_state:
            z = jnp.zeros((RW_DH, RW_DH), F32)
            for n, (i, j) in enumerate(chains):
                s_scr[n] = jnp.concatenate([jnp.concatenate([s0_ref[i, 2 * j], z], axis=1),
                                            jnp.concatenate([z, s0_ref[i, 2 * j + 1]], axis=1)], axis=0)
        else:
            s_scr[...] = jnp.zeros_like(s_scr)

    srow = lax.broadcasted_iota(jnp.int32, (LANES, LANES), 0)
    scol = lax.broadcasted_iota(jnp.int32, (LANES, LANES), 1)
    same_head = jnp.logical_or(jnp.logical_and(srow < RW_DH, scol < RW_DH),
                               jnp.logical_and(srow >= RW_DH, scol >= RW_DH))
    valid = lax.broadcasted_iota(jnp.int32, (C, LANES), 0) < Lv
    row = lax.broadcasted_iota(jnp.int32, (C2, C2), 0)
    col = lax.broadcasted_iota(jnp.int32, (C2, C2), 1)
    tt = row & (C - 1)
    ss = col & (C - 1)
    top = row < C
    bot = row >= C
    strict = ss < tt
    incl = ss <= tt
    keep0 = jnp.logical_or(jnp.logical_and(top, strict), jnp.logical_and(bot, incl))
    keep1 = jnp.logical_or(jnp.logical_and(top, incl), jnp.logical_and(bot, strict))
    tl = jnp.logical_and(top, col < C)
    br = jnp.logical_and(bot, col >= C)
    eye = (row == col).astype(F32)
    tri = lax.broadcasted_iota(jnp.int32, (C, C), 0) >= lax.broadcasted_iota(jnp.int32, (C, C), 1)
    left = lax.broadcasted_iota(jnp.int32, (C, LANES), 1) < RW_DH
    left2 = lax.broadcasted_iota(jnp.int32, (C2, LANES), 1) < RW_DH
    zeros = jnp.zeros((C, LANES), F32)

    def head_mean(x):
        lsum = jnp.sum(jnp.where(left, x, 0.0), axis=1, keepdims=True)
        rsum = jnp.sum(jnp.where(left, 0.0, x), axis=1, keepdims=True)
        return jnp.where(left, lsum, rsum) * (1.0 / RW_DH)

    ar, m0, m1, vs, bk_end, tots = [], [], [], [], [], []
    for i, j in chains:
        lw = jnp.where(valid, lw_ref[i, :, sl(j)], 0.0)
        cum = _sel_dot(tri, lw)
        tot = cum[C - 1:C, :]
        e_inv = jnp.exp(-cum)
        e_end = jnp.exp(tot - cum)
        kka = jnp.where(valid, kka_ref[i, :, sl(j)], 0.0)
        k = jnp.where(valid, k_ref[i, :, sl(j)], 0.0)
        a_t = -jnp.where(valid, kk_ref[i, :, sl(j)], 0.0) * jnp.exp(cum - lw)
        r_t = r_ref[i, :, sl(j)] * jnp.exp(cum)
        b_t = kka * e_inv
        k_t = k * e_inv
        ar_c = jnp.concatenate([a_t, r_t], axis=0)
        g0 = _dot_nt(jnp.where(left2, ar_c, 0.0), jnp.concatenate([b_t, k_t], axis=0))
        g1 = _dot_nt(jnp.where(left2, 0.0, jnp.concatenate([r_t, a_t], axis=0)), jnp.concatenate([k_t, b_t], axis=0))
        ar.append(ar_c.astype(BF16))
        m0.append(jnp.where(keep0, g0, 0.0))
        m1.append(jnp.where(keep1, g1, 0.0))
        vs.append(jnp.where(valid, v_ref[i, :, sl(j)], 0.0))
        bk_end.append(jnp.concatenate([kka * e_end, k * e_end], axis=0).astype(BF16))
        tots.append(tot)

    s_prev = [s_scr[n] for n in range(nch)]
    w_as = [_dot_nt(ar[n], s_prev[n]) for n in range(nch)]
    x0 = [_dot(m0[n][0:C, :], jnp.concatenate([zeros, vs[n]], axis=0)) for n in range(nch)]
    x1 = [_dot(m1[n][C:, :], jnp.concatenate([vs[n], zeros], axis=0)) for n in range(nch)]

    p = [jnp.where(tl, m0[n], jnp.where(br, m1[n], 0.0)) for n in range(nch)]
    t_inv = [eye + p[n] for n in range(nch)]
    step = 1
    while 2 * step < Lv:
        p = [_dot(p[n], p[n]) for n in range(nch)]
        t_inv = [t_inv[n] + _dot(t_inv[n], p[n]) for n in range(nch)]
        step *= 2

    rhs = [w_as[n][0:C, :] + jnp.where(left, x0[n], x1[n]) for n in range(nch)]
    tu = [_dot(t_inv[n], jnp.concatenate([rhs[n], rhs[n]], axis=0)) for n in range(nch)]
    u = [jnp.where(left, tu[n][0:C, :], tu[n][C:, :]) for n in range(nch)]
    uv = [jnp.concatenate([u[n], vs[n]], axis=0).astype(BF16) for n in range(nch)]
    vu = [jnp.concatenate([vs[n], u[n]], axis=0).astype(BF16) for n in range(nch)]
    o0 = [_dot(m0[n][C:, :], uv[n]) for n in range(nch)]
    o1 = [_dot(m1[n][0:C, :], vu[n]) for n in range(nch)]
    s_new = [_dot_tn(uv[n], bk_end[n]) for n in range(nch)]
    for n in range(nch):
        s_scr[n] = jnp.where(same_head, s_prev[n] * jnp.exp(tots[n]) + s_new[n], 0.0)

    for n, (i, j) in enumerate(chains):
        o = w_as[n][C:, :] + jnp.where(left, o0[n], o1[n])
        oc = o - head_mean(o)
        yn = oc * lax.rsqrt(head_mean(oc * oc) + RW_GN_EPS) * gnw_ref[:, sl(j)] + gnb_ref[:, sl(j)]
        y_ref[i, :, sl(j)] = ((yn + bonus_ref[i, :, sl(j)]) * g_ref[i, :, sl(j)]).astype(BF16)

    @pl.when(c == nc - 1)
    def _():
        for n, (i, j) in enumerate(chains):
            tile = s_scr[n]
            so_ref[i, 2 * j] = tile[0:RW_DH, 0:RW_DH]
            so_ref[i, 2 * j + 1] = tile[RW_DH:, RW_DH:]


def _rwkv_chunk_scan(toks, gnw, gnb, state, *, B, T, C, CB, Lv):
    nc = T // C
    has_state = state is not None
    tok = pl.BlockSpec((CB, C, RW_W), lambda b, c: (b, c, 0))
    const2 = lambda b, c: (0, 0)
    st = pl.BlockSpec((CB, RW_HEADS, RW_DH, RW_DH), lambda b, c: (b, 0, 0, 0))
    in_specs = [tok] * 8 + [pl.BlockSpec((1, RW_W), const2), pl.BlockSpec((1, RW_W), const2)]
    args = [t.reshape(B, T, RW_W) for t in toks] + [gnw, gnb]
    if has_state:
        in_specs.append(st)
        args.append(state)
    y, s = pl.pallas_call(
        functools.partial(_rwkv_chunk_kernel, C=C, CB=CB, Lv=Lv, nc=nc, has_state=has_state),
        grid=(B // CB, nc),
        in_specs=in_specs,
        out_specs=[tok, st],
        out_shape=[jax.ShapeDtypeStruct((B, T, RW_W), BF16),
                   jax.ShapeDtypeStruct((B, RW_HEADS, RW_DH, RW_DH), F32)],
        scratch_shapes=[pltpu.VMEM((CB * RW_PAIRS, LANES, LANES), F32)],
        compiler_params=_cparams(("parallel", "arbitrary")),
        name="rwkv_chunk",
    )(*args)
    return y.reshape(B * T, RW_W), s


def _merge_kernel(x_ref, zg_ref, yrp_ref, ymp_ref, ywp_ref, yrs_ref, yms_ref, yws_ref, wb_ref, wo_ref, o_ref, *, ntp):
    def run(branches):
        acc = None
        for n, y_ref in enumerate(branches):
            proj = jnp.dot(y_ref[...], wb_ref[n], preferred_element_type=F32)
            term = _sigmoid(zg_ref[:, n * D_MODEL:(n + 1) * D_MODEL]) * proj
            acc = term if acc is None else acc + term
        o_ref[...] = x_ref[...] + jnp.dot(acc.astype(BF16), wo_ref[...], preferred_element_type=F32)

    i = pl.program_id(0)

    @pl.when(i < ntp)
    def _():
        run((yrp_ref, ymp_ref, ywp_ref))

    @pl.when(i >= ntp)
    def _():
        run((yrs_ref, yms_ref, yws_ref))


def _merge(x, zgate, y_prompt, y_sample, w_branch, w_out, tm=256):
    n = x.shape[0]
    ntp = y_prompt[0].shape[0] // tm
    tokspec = lambda w: pl.BlockSpec((tm, w), lambda i: (i, 0))
    pspec = pl.BlockSpec((tm, RET_W), lambda i: (jnp.minimum(i, ntp - 1), 0))
    sspec = pl.BlockSpec((tm, RET_W), lambda i: (jnp.maximum(i - ntp, 0), 0))
    return pl.pallas_call(
        functools.partial(_merge_kernel, ntp=ntp),
        grid=(n // tm,),
        in_specs=[tokspec(D_MODEL), tokspec(N_BRANCH * D_MODEL), pspec, pspec, pspec, sspec, sspec, sspec,
                  pl.BlockSpec((N_BRANCH, RET_W, D_MODEL), lambda i: (0, 0, 0)),
                  pl.BlockSpec((D_MODEL, D_MODEL), lambda i: (0, 0))],
        out_specs=tokspec(D_MODEL),
        out_shape=jax.ShapeDtypeStruct((n, D_MODEL), F32),
        compiler_params=_cparams(("arbitrary",)),
        name="merge",
    )(x, zgate, *y_prompt, *y_sample, w_branch, w_out)


def _ffn_kernel(x_ref, g_ref, wg_ref, wu_ref, wd_ref, fin_ref, o_ref, h_scr, acc_scr, *, ne, final):
    e = pl.program_id(1)

    @pl.when(e == 0)
    def _():
        h_scr[...] = _rms(x_ref[...], g_ref[...]).astype(BF16)
        acc_scr[...] = jnp.zeros_like(acc_scr)

    h = h_scr[...]
    hg = jnp.dot(h, wg_ref[0], preferred_element_type=F32)
    hu = jnp.dot(h, wu_ref[0], preferred_element_type=F32)
    acc_scr[...] += jnp.dot((_silu(hg) * hu).astype(BF16), wd_ref[0], preferred_element_type=F32)

    @pl.when(e == ne - 1)
    def _():
        out = x_ref[...] + acc_scr[...]
        if final:
            out = _rms(out, fin_ref[...])
        o_ref[...] = out


def _ffn(x, g, wg, wu, wd, fin, final, tm=512):
    n = x.shape[0]
    ne, _, f = wg.shape
    return pl.pallas_call(
        functools.partial(_ffn_kernel, ne=ne, final=final),
        grid=(n // tm, ne),
        in_specs=[pl.BlockSpec((tm, D_MODEL), lambda i, e: (i, 0)),
                  pl.BlockSpec((1, D_MODEL), lambda i, e: (0, 0)),
                  pl.BlockSpec((1, D_MODEL, f), lambda i, e: (e, 0, 0)),
                  pl.BlockSpec((1, D_MODEL, f), lambda i, e: (e, 0, 0)),
                  pl.BlockSpec((1, f, D_MODEL), lambda i, e: (e, 0, 0)),
                  pl.BlockSpec((1, D_MODEL), lambda i, e: (0, 0))],
        out_specs=pl.BlockSpec((tm, D_MODEL), lambda i, e: (i, 0)),
        out_shape=jax.ShapeDtypeStruct((n, D_MODEL), F32),
        scratch_shapes=[pltpu.VMEM((tm, D_MODEL), BF16), pltpu.VMEM((tm, D_MODEL), F32)],
        compiler_params=_cparams(("parallel", "arbitrary")),
        name="ffn",
    )(x, g, wg, wu, wd, fin)


MOE_TM = 1024
MOE_SUB = LANES
MOE_CAP = 48
MOE_NSUB = MOE_TM // MOE_SUB


def _moe_kernel(x_ref, g_ref, router_ref, wg_ref, wu_ref, wd_ref, fin_ref, o_ref,
                h_scr, acc_scr, comb_scr, combt_scr, xe_scr, *, ne, final):
    e = pl.program_id(1)
    lane = lax.broadcasted_iota(jnp.int32, (MOE_TM, LANES), 1)

    @pl.when(e == 0)
    def _():
        h = _rms(x_ref[...], g_ref[...])
        h_scr[...] = h.astype(BF16)
        acc_scr[...] = jnp.zeros_like(acc_scr)
        logits = jnp.where(lane < ne, _dot_hi(h, router_ref[...]), NEG_BIG)
        m1 = jnp.max(logits, axis=1, keepdims=True)
        i1 = jnp.min(jnp.where(logits == m1, lane, LANES), axis=1, keepdims=True)
        rest = jnp.where(lane == i1, NEG_BIG, logits)
        m2 = jnp.max(rest, axis=1, keepdims=True)
        i2 = jnp.min(jnp.where(rest == m2, lane, LANES), axis=1, keepdims=True)
        e2 = jnp.exp(m2 - m1)
        p1 = 1.0 / (1.0 + e2)
        comb = jnp.where(lane == i1, p1, 0.0) + jnp.where(lane == i2, e2 * p1, 0.0)
        comb_scr[...] = comb
        for s in range(MOE_NSUB):
            combt_scr[s] = comb[s * MOE_SUB:(s + 1) * MOE_SUB, :].T

    sub = lambda s: slice(s * MOE_SUB, (s + 1) * MOE_SUB)
    slot = lambda s: slice(s * MOE_CAP, (s + 1) * MOE_CAP)
    r_i = lax.broadcasted_iota(jnp.int32, (MOE_SUB, MOE_SUB), 0)
    c_i = lax.broadcasted_iota(jnp.int32, (MOE_SUB, MOE_SUB), 1)
    before_row = (c_i < r_i).astype(F32).astype(BF16)
    before_col = (r_i < c_i).astype(F32).astype(BF16)
    slot_row = lax.broadcasted_iota(jnp.int32, (MOE_CAP, MOE_SUB), 0).astype(F32)
    slot_col = lax.broadcasted_iota(jnp.int32, (MOE_SUB, MOE_CAP), 1).astype(F32)

    w_col = jnp.sum(jnp.where(lane == e, comb_scr[...], 0.0), axis=1, keepdims=True)
    hit_col = [(w_col[sub(s), :] > 0.0).astype(F32) for s in range(MOE_NSUB)]
    hit_row = [(combt_scr[s, pl.ds(e, 1), :] > 0.0).astype(F32) for s in range(MOE_NSUB)]
    rank_col = [jnp.dot(before_row, jnp.broadcast_to(hit_col[s], (MOE_SUB, MOE_CAP)).astype(BF16),
                        preferred_element_type=F32) for s in range(MOE_NSUB)]
    rank_row = [jnp.dot(jnp.broadcast_to(hit_row[s], (SUBLANES, MOE_SUB)).astype(BF16), before_col,
                        preferred_element_type=F32)[0:1, :] for s in range(MOE_NSUB)]
    count = jnp.sum(hit_row[0], axis=1, keepdims=True)
    for s in range(1, MOE_NSUB):
        count = jnp.maximum(count, jnp.sum(hit_row[s], axis=1, keepdims=True))
    n_pass = lax.div(jnp.max(count).astype(jnp.int32) + (MOE_CAP - 1), MOE_CAP)

    def one_pass(p, carry):
        base = (p * MOE_CAP).astype(F32)
        for s in range(MOE_NSUB):
            pick = jnp.logical_and(rank_row[s] - base == slot_row, hit_row[s] > 0.0).astype(F32).astype(BF16)
            xe_scr[slot(s), :] = jnp.dot(pick, h_scr[sub(s), :], preferred_element_type=F32).astype(BF16)
        xe = xe_scr[...]
        hg = jnp.dot(xe, wg_ref[0], preferred_element_type=F32)
        hu = jnp.dot(xe, wu_ref[0], preferred_element_type=F32)
        y = jnp.dot((_silu(hg) * hu).astype(BF16), wd_ref[0], preferred_element_type=F32)
        for s in range(MOE_NSUB):
            put = jnp.logical_and(rank_col[s] - base == slot_col, hit_col[s] > 0.0).astype(F32).astype(BF16)
            hi, lo = _split(y[slot(s), :])
            back = jnp.dot(put, hi, preferred_element_type=F32) + jnp.dot(put, lo, preferred_element_type=F32)
            acc_scr[sub(s), :] += w_col[sub(s), :] * back
        return carry

    lax.fori_loop(0, n_pass, one_pass, 0)

    @pl.when(e == ne - 1)
    def _():
        out = x_ref[...] + acc_scr[...]
        if final:
            out = _rms(out, fin_ref[...])
        o_ref[...] = out


def _moe(x, g, wg, wu, wd, router, fin, final):
    n = x.shape[0]
    ne, _, f = wg.shape
    return pl.pallas_call(
        functools.partial(_moe_kernel, ne=ne, final=final),
        grid=(n // MOE_TM, ne),
        in_specs=[pl.BlockSpec((MOE_TM, D_MODEL), lambda i, e: (i, 0)),
                  pl.BlockSpec((1, D_MODEL), lambda i, e: (0, 0)),
                  pl.BlockSpec((D_MODEL, LANES), lambda i, e: (0, 0)),
                  pl.BlockSpec((1, D_MODEL, f), lambda i, e: (e, 0, 0)),
                  pl.BlockSpec((1, D_MODEL, f), lambda i, e: (e, 0, 0)),
                  pl.BlockSpec((1, f, D_MODEL), lambda i, e: (e, 0, 0)),
                  pl.BlockSpec((1, D_MODEL), lambda i, e: (0, 0))],
        out_specs=pl.BlockSpec((MOE_TM, D_MODEL), lambda i, e: (i, 0)),
        out_shape=jax.ShapeDtypeStruct((n, D_MODEL), F32),
        scratch_shapes=[pltpu.VMEM((MOE_TM, D_MODEL), BF16), pltpu.VMEM((MOE_TM, D_MODEL), F32),
                        pltpu.VMEM((MOE_TM, LANES), F32), pltpu.VMEM((MOE_NSUB, LANES, MOE_SUB), F32),
                        pltpu.VMEM((MOE_NSUB * MOE_CAP, D_MODEL), BF16)],
        compiler_params=_cparams(("parallel", "arbitrary")),
        name="moe",
    )(x, g, router, wg, wu, wd, fin)


def _row(v):
    return v.reshape(1, -1)


def kernel(x_prompt, x_sample, state_ret, state_mlstm_C, state_mlstm_n, state_mlstm_m, state_mlstm_conv, state_rwkv, state_rwkv_shift, norm_mix, w_in, ret_gn, ml_conv_w, ml_conv_b, ml_wq, ml_wk, ml_bi, ml_bf, ml_gn, ml_skip, rw_mu, rw_w0, rw_w2, rw_a0, rw_a2, rw_g2, rw_kk, rw_ka, rw_rk, rw_gn_w, rw_gn_b, w_branch, w_out, norm_ffn, ffn_w_gate, ffn_w_up, ffn_w_down, moe_router, moe_w_gate, moe_w_up, moe_w_down, final_norm):
    nc_p = SEQ // CHUNK
    xs = jnp.pad(x_sample, ((0, 0), (0, S_PAD - DEC_SEQ), (0, 0)))
    x = jnp.concatenate([x_prompt.reshape(N_PROMPT, D_MODEL), xs.reshape(N_SAMPLE, D_MODEL)], axis=0)

    pos_p = jnp.arange(SEQ, dtype=jnp.int32)
    pos_s = PAST_LEN + jnp.arange(S_PAD, dtype=jnp.int32)
    ret_tab_p = _retention_tables(CHUNK, CHUNK, pos_p, 1)
    ret_tab_s = _retention_tables(S_PAD, DEC_SEQ, pos_s, RET_NB_SAMPLE)
    head_of = jnp.arange(RW_W) // RW_DH
    bd64 = (head_of[:, None] == head_of[None, :]).astype(BF16)

    o_rq = 0
    o_mx = o_rq + RET_COLS
    o_mi = o_mx + 2 * ML_W
    o_mo = o_mi + 2 * ML_HEADS
    o_rw = o_mo + ML_W
    o_gate = o_rw + RW_COLS

    new_p, new_s = [], []
    for l in range(DEPTH):
        w = w_in[l]
        w_ret = w[:, o_rq:o_mx].astype(BF16)
        w_ml = jnp.concatenate([w[:, o_mx:o_mi], w[:, o_mo:o_rw], w[:, o_mi:o_mo],
                                jnp.zeros((D_MODEL, ML_GATE_W - 2 * ML_HEADS), F32)], axis=1).astype(BF16)
        w_rw = w[:, o_rw:o_gate].astype(BF16)
        w_gate = w[:, o_gate:].astype(BF16)
        g_mix = _row(norm_mix[l])
        zret = _norm_matmul(x, g_mix, w_ret)
        zml = _norm_matmul(x, g_mix, w_ml)
        zrw = _norm_matmul(x, g_mix, w_rw)
        zgate = _norm_matmul(x, g_mix, w_gate)

        gn = _row(ret_gn[l])
        yr_p, ret_p = _retention(zret, ret_tab_p, gn, None, B=BATCH, nc=nc_p, L=CHUNK, nb=1, row0=0)
        yr_s, ret_s = _retention(zret, ret_tab_s, gn, state_ret[l], B=DEC_BATCH, nc=1, L=S_PAD, nb=RET_NB_SAMPLE,
                                 row0=N_PROMPT // (RET_NB_SAMPLE * S_PAD))

        gate_bias = jnp.concatenate([ml_bi[l], ml_bf[l], jnp.zeros((ML_GATE_W - 2 * ML_HEADS,), F32)])
        ml_weights = (ml_conv_w[l], _row(ml_conv_b[l]), ml_wq[l].astype(BF16), ml_wk[l].astype(BF16),
                      _row(gate_bias), _row(ml_gn[l]), _row(ml_skip[l]))
        m0 = jnp.pad(state_mlstm_m[l], ((0, 0), (0, LANES - ML_HEADS))).reshape(DEC_BATCH, 1, LANES)
        ym_p, c_p, n_p, m_p, buf_p = _mlstm(zml, ml_weights, None, B=BATCH, nc=nc_p, L=CHUNK, Lv=CHUNK, nb=1,
                                            row0=0)
        ym_s, c_s, n_s, m_s, buf_s = _mlstm(zml, ml_weights,
                                            (state_mlstm_C[l], state_mlstm_n[l], m0, state_mlstm_conv[l]),
                                            B=DEC_BATCH, nc=1, L=S_PAD, Lv=DEC_SEQ, nb=ML_NB_SAMPLE,
                                            row0=N_PROMPT // (ML_NB_SAMPLE * S_PAD))

        w2a = jnp.zeros((RW_LORA_W + RW_LORA_A, 2 * RW_W), F32)
        w2a = w2a.at[:RW_LORA_W, :RW_W].set(rw_w2[l]).at[RW_LORA_W:, RW_W:].set(rw_a2[l])
        rw_weights = (_row(rw_mu[l]), _row(jnp.concatenate([rw_w0[l], rw_a0[l]])), w2a.astype(BF16),
                      rw_g2[l].astype(BF16), _row(rw_kk[l]), _row(rw_ka[l]), _row(rw_rk[l]), bd64)
        *tok_p, shift_p = _rwkv_prep(zrw, rw_weights, None, B=BATCH, nc=nc_p, L=CHUNK, Lv=CHUNK, nb=1, row0=0)
        *tok_s, shift_s = _rwkv_prep(zrw, rw_weights, state_rwkv_shift[l].reshape(DEC_BATCH, 1, RW_COLS),
                                     B=DEC_BATCH, nc=1, L=S_PAD, Lv=DEC_SEQ, nb=RWP_NB_SAMPLE,
                                     row0=N_PROMPT // (RWP_NB_SAMPLE * S_PAD))
        gnw, gnb = _row(rw_gn_w[l]), _row(rw_gn_b[l])
        yw_p, rws_p = _rwkv_chunk_scan(tok_p, gnw, gnb, None, B=BATCH, T=SEQ, C=RW_C, CB=RW_CB_PROMPT, Lv=RW_C)
        yw_s, rws_s = _rwkv_chunk_scan(tok_s, gnw, gnb, state_rwkv[l], B=DEC_BATCH, T=S_PAD, C=S_PAD,
                                       CB=RW_CB_SAMPLE, Lv=DEC_SEQ)

        x = _merge(x, zgate, (yr_p, ym_p, yw_p), (yr_s, ym_s, yw_s), w_branch[l].astype(BF16), w_out[l].astype(BF16))

        i = l // 2
        g_ffn = _row(norm_ffn[l])
        fin = _row(final_norm)
        final = l == DEPTH - 1
        if l % 2 == 0:
            halves = D_FF // D_FF_EXPERT
            wg = ffn_w_gate[i].reshape(D_MODEL, halves, D_FF_EXPERT).transpose(1, 0, 2).astype(BF16)
            wu = ffn_w_up[i].reshape(D_MODEL, halves, D_FF_EXPERT).transpose(1, 0, 2).astype(BF16)
            wd = ffn_w_down[i].reshape(halves, D_FF_EXPERT, D_MODEL).astype(BF16)
            x = _ffn(x, g_ffn, wg, wu, wd, fin, final)
        else:
            router = jnp.pad(moe_router[i], ((0, 0), (0, LANES - N_EXPERTS)))
            x = _moe(x, g_ffn, moe_w_gate[i].astype(BF16), moe_w_up[i].astype(BF16), moe_w_down[i].astype(BF16),
                     router, fin, final)

        new_p.append((ret_p, c_p, n_p, m_p[:, 0, :ML_HEADS], buf_p, rws_p, shift_p[:, 0]))
        new_s.append((ret_s, c_s, n_s, m_s[:, 0, :ML_HEADS], buf_s, rws_s, shift_s[:, 0]))

    y_prompt = x[:N_PROMPT].reshape(BATCH, SEQ, D_MODEL)
    y_sample = x[N_PROMPT:].reshape(DEC_BATCH, S_PAD, D_MODEL)[:, :DEC_SEQ]
    st_p = tuple(jnp.stack([st[j] for st in new_p]) for j in range(7))
    st_s = tuple(jnp.stack([st[j] for st in new_s]) for j in range(7))
    return (y_prompt, y_sample) + st_p + st_s
```

```python
import functools

import jax
import jax.numpy as jnp
from jax import lax
from jax.experimental import pallas as pl
from jax.experimental.pallas import tpu as pltpu

D_MODEL = 1024
BATCH = 8
SEQ = 2048
DEPTH = 2
DEC_BATCH = 128
DEC_SEQ = 4
PAST_LEN = 16384
RET_HEADS = 4
RET_DK = 64
RET_DV = 128
ML_HEADS = 4
ML_DH = 128
CONV_W = 4
RW_HEADS = 8
RW_DH = 64
RW_LORA_W = 64
RW_LORA_A = 64
RW_LORA_G = 128
RET_W = RET_HEADS * RET_DV
ML_W = ML_HEADS * ML_DH
RW_W = RW_HEADS * RW_DH
N_BRANCH = 3
RW_COLS = 3 * RW_W + RW_LORA_W + RW_LORA_A + RW_LORA_G
D_FF = 2816
N_EXPERTS = 8
D_FF_EXPERT = 1408
CHUNK = 128
NORM_EPS = 1e-6
GN_EPS = 1e-5
RW_GN_EPS = 64e-5
ROPE_BASE = 10000.0
RW_DECAY_SCALE = 0.606531

LANES = 128
SUBLANES = 8
S_PAD = 16
N_PROMPT = BATCH * SEQ
N_SAMPLE = DEC_BATCH * S_PAD
N_TOK = N_PROMPT + N_SAMPLE
ML_GATE_W = LANES
ML_COLS = 3 * ML_W + ML_GATE_W
RET_COLS = 2 * RET_HEADS * RET_DK + 2 * RET_W
NEG_BIG = -1e30
VMEM_LIMIT = 56 * 1024 * 1024

RW_C = 64
RW_PAIRS = RW_HEADS // 2
RW_CB_PROMPT = 4
RW_CB_SAMPLE = 4
RET_NB_SAMPLE = 8
ML_NB_SAMPLE = 4

F32 = jnp.float32
BF16 = jnp.bfloat16
HI = lax.Precision.HIGHEST


def _cparams(sem):
    return pltpu.CompilerParams(dimension_semantics=sem, vmem_limit_bytes=VMEM_LIMIT)


def _sigmoid(x):
    return 1.0 / (1.0 + jnp.exp(-x))


def _silu(x):
    return x * _sigmoid(x)


def _rms(x, g):
    return x * lax.rsqrt(jnp.mean(x * x, axis=-1, keepdims=True) + NORM_EPS) * g


def _dot(a, b):
    return jnp.dot(a.astype(BF16), b.astype(BF16), preferred_element_type=F32)


def _dot_nt(a, b):
    return lax.dot_general(a.astype(BF16), b.astype(BF16), (((1,), (1,)), ((), ())), preferred_element_type=F32)


def _dot_tn(a, b):
    return lax.dot_general(a.astype(BF16), b.astype(BF16), (((0,), (0,)), ((), ())), preferred_element_type=F32)


def _dot_hi(a, b):
    return jnp.dot(a, b, preferred_element_type=F32, precision=HI)


def _split(x):
    hi = x.astype(BF16)
    return hi, (x - hi.astype(F32)).astype(BF16)


def _as_bf16_mask(m):
    return m if m.dtype == BF16 else m.astype(F32).astype(BF16)


def _sel_dot(m, x):
    hi, lo = _split(x)
    m = _as_bf16_mask(m)
    return jnp.dot(m, hi, preferred_element_type=F32) + jnp.dot(m, lo, preferred_element_type=F32)


def _dot_sel(x, m):
    hi, lo = _split(x)
    m = _as_bf16_mask(m)
    return jnp.dot(hi, m, preferred_element_type=F32) + jnp.dot(lo, m, preferred_element_type=F32)


def _sel_dot_nt(m, x):
    hi, lo = _split(x)
    m = _as_bf16_mask(m)
    dn = (((1,), (1,)), ((), ()))
    return (lax.dot_general(m, hi, dn, preferred_element_type=F32)
            + lax.dot_general(m, lo, dn, preferred_element_type=F32))


def _norm_matmul_kernel(x_ref, g_ref, w_ref, o_ref):
    h = _rms(x_ref[...], g_ref[...])
    o_ref[...] = jnp.dot(h.astype(BF16), w_ref[...], preferred_element_type=F32)


def _norm_matmul(x, g, w, tm=512):
    n, d = x.shape
    c = w.shape[1]
    return pl.pallas_call(
        _norm_matmul_kernel,
        grid=(n // tm,),
        in_specs=[pl.BlockSpec((tm, d), lambda i: (i, 0)),
                  pl.BlockSpec((1, d), lambda i: (0, 0)),
                  pl.BlockSpec((d, c), lambda i: (0, 0))],
        out_specs=pl.BlockSpec((tm, c), lambda i: (i, 0)),
        out_shape=jax.ShapeDtypeStruct((n, c), F32),
        compiler_params=_cparams(("parallel",)),
        name="norm_matmul",
    )(x, g, w)


def _retention_kernel(*refs, L, nc, nb, has_state):
    if has_state:
        (q_ref, k_ref, v_ref, g_ref, cos_ref, sin_ref, di_ref, dq_ref, dk_ref, dc_ref, gn_ref, s0_ref,
         y_ref, so_ref, s_scr) = refs
    else:
        (q_ref, k_ref, v_ref, g_ref, cos_ref, sin_ref, di_ref, dq_ref, dk_ref, dc_ref, gn_ref,
         y_ref, so_ref, s_scr) = refs
    c = pl.program_id(1)

    @pl.when(c == 0)
    def _():
        if has_state:
            s_scr[...] = s0_ref[...]
        else:
            s_scr[...] = jnp.zeros_like(s_scr)

    qk_w = RET_HEADS * RET_DK
    half = RET_DK // 2
    lane = lax.broadcasted_iota(jnp.int32, (nb * L, qk_w), 1)
    first_half = (lane % RET_DK) < half
    cos = cos_ref[...]
    sin = sin_ref[...]

    def rot(x):
        swapped = jnp.where(first_half, pltpu.roll(x, qk_w - half, 1), pltpu.roll(x, half, 1))
        return x * cos + swapped * sin

    q = rot(q_ref[...])
    k = rot(k_ref[...]) * (RET_DK ** -0.5)
    dq = dq_ref[...]
    dk = dk_ref[...]
    dc = dc_ref[...]
    chains = [(i, h) for i in range(nb) for h in range(RET_HEADS)]
    rows = lambda i: slice(i * L, (i + 1) * L)
    kcols = lambda h: slice(h * RET_DK, (h + 1) * RET_DK)
    vcols = lambda h: slice(h * RET_DV, (h + 1) * RET_DV)

    qh = [q[rows(i), kcols(h)].astype(BF16) for i, h in chains]
    kh = [k[rows(i), kcols(h)] for i, h in chains]
    vh = [v_ref[rows(i), vcols(h)].astype(BF16) for i, h in chains]
    s = [_dot_nt(qh[n], kh[n]) * di_ref[h] for n, (i, h) in enumerate(chains)]
    s_prev = [s_scr[i, h] for i, h in chains]
    cross = [_dot(qh[n], s_prev[n]) for n in range(len(chains))]
    upd = [_dot_tn(kh[n] * dk[:, h:h + 1], vh[n]) for n, (i, h) in enumerate(chains)]
    inner = [_dot(s[n], vh[n]) for n in range(len(chains))]
    for n, (i, h) in enumerate(chains):
        s_scr[i, h] = s_prev[n] * dc[:, h:h + 1] + upd[n]
        o = inner[n] + cross[n] * dq[:, h:h + 1]
        oc = o - jnp.mean(o, axis=-1, keepdims=True)
        yn = oc * lax.rsqrt(jnp.mean(oc * oc, axis=-1, keepdims=True) + GN_EPS) * gn_ref[:, vcols(h)]
        y_ref[rows(i), vcols(h)] = (_silu(g_ref[rows(i), vcols(h)]) * yn).astype(BF16)

    @pl.when(c == nc - 1)
    def _():
        so_ref[...] = s_scr[...]


def _retention(zret, tables, gn, state, layer, *, B, nc, L, nb, row0):
    cos, sin, di, dq, dk, dc = tables
    has_state = state is not None
    qk_w = RET_HEADS * RET_DK
    R = nb * L
    row = lambda b, c: row0 + b * nc + c
    st = pl.BlockSpec((nb, RET_HEADS, RET_DK, RET_DV), lambda b, c: (b, 0, 0, 0))
    in_specs = [pl.BlockSpec((R, qk_w), lambda b, c: (row(b, c), 0)),
                pl.BlockSpec((R, qk_w), lambda b, c: (row(b, c), 1)),
                pl.BlockSpec((R, RET_W), lambda b, c: (row(b, c), 1)),
                pl.BlockSpec((R, RET_W), lambda b, c: (row(b, c), 2)),
                pl.BlockSpec((R, qk_w), lambda b, c: (c, 0)),
                pl.BlockSpec((R, qk_w), lambda b, c: (c, 0)),
                pl.BlockSpec((RET_HEADS, L, L), lambda b, c: (0, 0, 0)),
                pl.BlockSpec((L, RET_HEADS), lambda b, c: (0, 0)),
                pl.BlockSpec((L, RET_HEADS), lambda b, c: (0, 0)),
                pl.BlockSpec((1, RET_HEADS), lambda b, c: (0, 0)),
                pl.BlockSpec((1, RET_W), lambda b, c: (0, 0))]
    args = [zret, zret, zret, zret, cos, sin, di, dq, dk, dc, gn]
    if has_state:
        in_specs.append(pl.BlockSpec((None, nb, RET_HEADS, RET_DK, RET_DV), lambda b, c: (layer, b, 0, 0, 0)))
        args.append(state)
    return pl.pallas_call(
        functools.partial(_retention_kernel, L=L, nc=nc, nb=nb, has_state=has_state),
        grid=(B // nb, nc),
        in_specs=in_specs,
        out_specs=[pl.BlockSpec((R, RET_W), lambda b, c: (b * nc + c, 0)), st],
        out_shape=[jax.ShapeDtypeStruct((B * nc * L, RET_W), BF16),
                   jax.ShapeDtypeStruct((B, RET_HEADS, RET_DK, RET_DV), F32)],
        scratch_shapes=[pltpu.VMEM((nb, RET_HEADS, RET_DK, RET_DV), F32)],
        compiler_params=_cparams(("parallel", "arbitrary")),
        name="retention",
    )(*args)


def _retention_tables(L, Lv, pos, nb):
    half = RET_DK // 2
    inv = ROPE_BASE ** (-jnp.arange(half, dtype=F32) / half)
    ang = pos.astype(F32)[:, None] * inv[None, :]
    cos = jnp.tile(jnp.concatenate([jnp.cos(ang), jnp.cos(ang)], axis=1), (nb, RET_HEADS))
    sin = jnp.tile(jnp.concatenate([-jnp.sin(ang), jnp.sin(ang)], axis=1), (nb, RET_HEADS))
    log_gamma = jnp.log1p(-(2.0 ** (-5.0 - jnp.arange(RET_HEADS, dtype=F32))))
    idx = jnp.arange(L, dtype=F32)
    diff = idx[:, None] - idx[None, :]
    causal = diff >= 0
    di = jnp.where(causal[None], jnp.exp(jnp.where(causal, diff, 0.0)[None] * log_gamma[:, None, None]), 0.0)
    dq = jnp.exp((idx[:, None] + 1.0) * log_gamma[None, :])
    dk = jnp.where((idx < Lv)[:, None], jnp.exp((Lv - 1.0 - idx)[:, None] * log_gamma[None, :]), 0.0)
    dc = jnp.exp(Lv * log_gamma)[None, :]
    return cos, sin, di, dq, dk, dc


ROW_PAD = SUBLANES


def _mlstm_kernel(*refs, L, Lv, nc, nb, has_state):
    if has_state:
        (mx_ref, mv_ref, mo_ref, gz_ref, cw_ref, cb_ref, wq_ref, wk_ref, gb_ref, gn_ref, skip_ref,
         c0_ref, n0_ref, m0_ref, buf0_ref,
         y_ref, co_ref, no_ref, mout_ref, bufo_ref, xp_scr, c_scr, n_scr, m_scr) = refs
    else:
        (mx_ref, mv_ref, mo_ref, gz_ref, cw_ref, cb_ref, wq_ref, wk_ref, gb_ref, gn_ref, skip_ref,
         y_ref, co_ref, no_ref, mout_ref, bufo_ref, xp_scr, c_scr, n_scr, m_scr) = refs
    c = pl.program_id(1)
    tail = CONV_W - 1

    @pl.when(c == 0)
    def _():
        for i in range(nb):
            xp_scr[i, 0:ROW_PAD, :] = jnp.zeros((ROW_PAD, ML_W), F32)
        if has_state:
            c_scr[...] = c0_ref[...]
            n_scr[...] = n0_ref[...]
            m_scr[...] = m0_ref[...]
            for i in range(nb):
                xp_scr[i, ROW_PAD - tail:ROW_PAD, :] = buf0_ref[i]
        else:
            c_scr[...] = jnp.zeros_like(c_scr)
            n_scr[...] = jnp.zeros_like(n_scr)
            m_scr[...] = jnp.zeros_like(m_scr)

    rows = lambda i: slice(i * L, (i + 1) * L)
    cols = lambda h: slice(h * ML_DH, (h + 1) * ML_DH)
    lane = lax.broadcasted_iota(jnp.int32, (L, ML_GATE_W), 1)
    rowi = lax.broadcasted_iota(jnp.int32, (L, ML_GATE_W), 0)
    tri = lax.broadcasted_iota(jnp.int32, (L, L), 0) >= lax.broadcasted_iota(jnp.int32, (L, L), 1)
    eye = (lax.broadcasted_iota(jnp.int32, (SUBLANES, ML_GATE_W), 0)
           == lax.broadcasted_iota(jnp.int32, (SUBLANES, ML_GATE_W), 1))

    xc, new_tail, gc, fc, g_rows, f_rows = [], [], [], [], [], []
    for i in range(nb):
        xp_scr[i, ROW_PAD:ROW_PAD + L, :] = mx_ref[rows(i), :]
        acc = cb_ref[...]
        for j in range(CONV_W):
            acc = acc + cw_ref[j:j + 1, :] * xp_scr[i, ROW_PAD - tail + j:ROW_PAD - tail + j + L, :]
        xc.append(_silu(acc))
        new_tail.append(xp_scr[i, ROW_PAD + Lv - tail:ROW_PAD + Lv, :])
        xp_scr[i, ROW_PAD - tail:ROW_PAD, :] = new_tail[i]
        gz = gz_ref[rows(i), :] + gb_ref[...]
        logsig = jnp.minimum(gz, 0.0) - jnp.log1p(jnp.exp(-jnp.abs(gz)))
        g = jnp.where(lane < ML_HEADS, gz, logsig)
        if Lv < L:
            g = jnp.where(rowi < Lv, g, jnp.where(lane < ML_HEADS, NEG_BIG, 0.0))
        gc.append(g)
        fc.append(_sel_dot(tri, g))
        g_rows.append(_sel_dot_nt(eye, g))
        f_rows.append(_sel_dot_nt(eye, fc[i]))

    chains = [(i, h) for i in range(nb) for h in range(ML_HEADS)]
    nch = len(chains)
    xh = [xc[i][:, cols(h)] for i, h in chains]
    xh_b = [x.astype(BF16) for x in xh]
    q = [jnp.dot(xh_b[n], wq_ref[h], preferred_element_type=F32) for n, (i, h) in enumerate(chains)]
    k = [jnp.dot(xh_b[n], wk_ref[h], preferred_element_type=F32) * (ML_DH ** -0.5) for n, (i, h) in enumerate(chains)]
    q_b = [x.astype(BF16) for x in q]
    v_b = [mv_ref[rows(i), cols(h)].astype(BF16) for i, h in chains]
    qk = [_dot_nt(q_b[n], k[n]) for n in range(nch)]
    c_prev = [c_scr[i, h] for i, h in chains]
    qc = [_dot(q_b[n], c_prev[n]) for n in range(nch)]

    s, a_inter, m_t, kw, a_old, m_new = [], [], [], [], [], []
    for n, (i, h) in enumerate(chains):
        f_col = fc[i][:, ML_HEADS + h:ML_HEADS + h + 1]
        i_col = gc[i][:, h:h + 1]
        f_row = f_rows[i][ML_HEADS + h:ML_HEADS + h + 1, :]
        i_row = g_rows[i][h:h + 1, :]
        m_prev = m_scr[i, 0:1, h:h + 1]
        dlog = jnp.where(tri, f_col - f_row + i_row, NEG_BIG)
        inter = f_col + m_prev
        m_t.append(jnp.maximum(jnp.max(dlog, axis=1, keepdims=True), inter))
        s.append(qk[n] * jnp.exp(dlog - m_t[n]))
        a_inter.append(jnp.exp(inter - m_t[n]))
        f_last = fc[i][Lv - 1:Lv, ML_HEADS + h:ML_HEADS + h + 1]
        m_new.append(jnp.maximum(f_last + m_prev, jnp.max(f_last - f_row + i_row, axis=1, keepdims=True)))
        kw.append(k[n] * jnp.exp(f_last - f_col + i_col - m_new[n]))
        a_old.append(jnp.exp(f_last + m_prev - m_new[n]))

    sv = [_dot(s[n], v_b[n]) for n in range(nch)]
    ktv = [_dot_tn(kw[n], v_b[n]) for n in range(nch)]
    for n, (i, h) in enumerate(chains):
        n_prev = n_scr[i, h:h + 1, :]
        num = sv[n] + a_inter[n] * qc[n]
        den = jnp.sum(s[n], axis=1, keepdims=True) + a_inter[n] * jnp.sum(q[n] * n_prev, axis=1, keepdims=True)
        hh = num * (1.0 / jnp.maximum(jnp.abs(den), jnp.exp(-m_t[n])))
        c_scr[i, h] = a_old[n] * c_prev[n] + ktv[n]
        n_scr[i, h:h + 1, :] = a_old[n] * n_prev + jnp.sum(kw[n], axis=0, keepdims=True)
        m_scr[i, 0:1, h:h + 1] = m_new[n]
        hc = hh - jnp.mean(hh, axis=-1, keepdims=True)
        hn = hc * lax.rsqrt(jnp.mean(hc * hc, axis=-1, keepdims=True) + GN_EPS) * gn_ref[:, cols(h)]
        y_ref[rows(i), cols(h)] = (_sigmoid(mo_ref[rows(i), cols(h)]) * (hn + skip_ref[:, cols(h)] * xh[n])).astype(BF16)

    @pl.when(c == nc - 1)
    def _():
        co_ref[...] = c_scr[...]
        no_ref[...] = n_scr[...]
        mout_ref[...] = m_scr[...]
        for i in range(nb):
            bufo_ref[i] = new_tail[i]


def _mlstm(zml, weights, state, layer, *, B, nc, L, Lv, nb, row0):
    cw, cb, wq, wk, gb, gn, skip = weights
    has_state = state is not None
    R = nb * L
    row = lambda b, c: row0 + b * nc + c
    const2 = lambda b, c: (0, 0)
    in_specs = [pl.BlockSpec((R, ML_W), lambda b, c: (row(b, c), 0)),
                pl.BlockSpec((R, ML_W), lambda b, c: (row(b, c), 1)),
                pl.BlockSpec((R, ML_W), lambda b, c: (row(b, c), 2)),
                pl.BlockSpec((R, ML_GATE_W), lambda b, c: (row(b, c), 3 * ML_W // ML_GATE_W)),
                pl.BlockSpec((CONV_W, ML_W), const2),
                pl.BlockSpec((1, ML_W), const2),
                pl.BlockSpec((ML_HEADS, ML_DH, ML_DH), lambda b, c: (0, 0, 0)),
                pl.BlockSpec((ML_HEADS, ML_DH, ML_DH), lambda b, c: (0, 0, 0)),
                pl.BlockSpec((1, ML_GATE_W), const2),
                pl.BlockSpec((1, ML_W), const2),
                pl.BlockSpec((1, ML_W), const2)]
    args = [zml, zml, zml, zml, cw, cb, wq, wk, gb, gn, skip]
    st_specs = [pl.BlockSpec((nb, ML_HEADS, ML_DH, ML_DH), lambda b, c: (b, 0, 0, 0)),
                pl.BlockSpec((nb, ML_HEADS, ML_DH), lambda b, c: (b, 0, 0)),
                pl.BlockSpec((nb, 1, LANES), lambda b, c: (b, 0, 0)),
                pl.BlockSpec((nb, CONV_W - 1, ML_W), lambda b, c: (b, 0, 0))]
    if has_state:
        in_specs += [pl.BlockSpec((None, nb, ML_HEADS, ML_DH, ML_DH), lambda b, c: (layer, b, 0, 0, 0)),
                     pl.BlockSpec((None, nb, ML_HEADS, ML_DH), lambda b, c: (layer, b, 0, 0)),
                     pl.BlockSpec((None, nb, 1, LANES), lambda b, c: (layer, b, 0, 0)),
                     pl.BlockSpec((None, nb, CONV_W - 1, ML_W), lambda b, c: (layer, b, 0, 0))]
        args += list(state)
    return pl.pallas_call(
        functools.partial(_mlstm_kernel, L=L, Lv=Lv, nc=nc, nb=nb, has_state=has_state),
        grid=(B // nb, nc),
        in_specs=in_specs,
        out_specs=[pl.BlockSpec((R, ML_W), lambda b, c: (b * nc + c, 0))] + st_specs,
        out_shape=[jax.ShapeDtypeStruct((B * nc * L, ML_W), BF16),
                   jax.ShapeDtypeStruct((B, ML_HEADS, ML_DH, ML_DH), F32),
                   jax.ShapeDtypeStruct((B, ML_HEADS, ML_DH), F32),
                   jax.ShapeDtypeStruct((B, 1, LANES), F32),
                   jax.ShapeDtypeStruct((B, CONV_W - 1, ML_W), F32)],
        scratch_shapes=[pltpu.VMEM((nb, ROW_PAD + L, ML_W), F32),
                        pltpu.VMEM((nb, ML_HEADS, ML_DH, ML_DH), F32),
                        pltpu.VMEM((nb, ML_HEADS, ML_DH), F32),
                        pltpu.VMEM((nb, 1, LANES), F32)],
        compiler_params=_cparams(("parallel", "arbitrary")),
        name="mlstm",
    )(*args)


def _rwkv_token_vectors(zs, wa0, w2a, g2, kkp, ka, rk, bd):
    n = len(zs)
    r = [z[:, 0:RW_W] for z in zs]
    kr = [z[:, RW_W:2 * RW_W] for z in zs]
    vr = [z[:, 2 * RW_W:3 * RW_W] for z in zs]
    wa = [z[:, 3 * RW_W:3 * RW_W + RW_LORA_W + RW_LORA_A] for z in zs]
    gl = [z[:, 3 * RW_W + RW_LORA_W + RW_LORA_A:] for z in zs]
    lane = lax.broadcasted_iota(jnp.int32, wa[0].shape, 1)
    lora = [jnp.dot(jnp.where(lane < RW_LORA_W, jnp.tanh(wa[i]), wa[i]).astype(BF16), w2a,
                    preferred_element_type=F32) + wa0 for i in range(n)]
    g = [jnp.dot(_sigmoid(gl[i]).astype(BF16), g2, preferred_element_type=F32) for i in range(n)]
    kk = [kr[i] * kkp for i in range(n)]
    kk_ss = [_dot_sel(kk[i] * kk[i], bd) for i in range(n)]
    a = [_sigmoid(lora[i][:, RW_W:]) for i in range(n)]
    k2 = [kr[i] * (1.0 + (a[i] - 1.0) * ka) for i in range(n)]
    rk_sum = [_dot_sel(r[i] * k2[i] * rk, bd) for i in range(n)]
    out = []
    for i in range(n):
        kk_n = kk[i] / jnp.maximum(jnp.sqrt(kk_ss[i]), 1e-12)
        out.append((r[i], -RW_DECAY_SCALE * _sigmoid(lora[i][:, 0:RW_W]), k2[i], vr[i], kk_n, kk_n * a[i], g[i],
                    rk_sum[i] * vr[i]))
    return out


def _rwkv_chunk_kernel(*refs, C, CB, Lv, nc, has_state):
    z_refs = refs[:CB]
    refs = refs[CB:]
    if has_state:
        (mu_ref, wa0_ref, w2a_ref, g2_ref, kkp_ref, ka_ref, rk_ref, bd_ref, gnw_ref, gnb_ref, prev_ref, s0_ref,
         y_ref, so_ref, shift_ref, s_scr, xs_scr) = refs
    else:
        (mu_ref, wa0_ref, w2a_ref, g2_ref, kkp_ref, ka_ref, rk_ref, bd_ref, gnw_ref, gnb_ref,
         y_ref, so_ref, shift_ref, s_scr, xs_scr) = refs
    c = pl.program_id(1)
    C2 = 2 * C
    chains = [(i, j) for i in range(CB) for j in range(RW_PAIRS)]
    nch = len(chains)
    sl = lambda j: slice(j * LANES, (j + 1) * LANES)

    @pl.when(c == 0)
    def _():
        for i in range(CB):
            xs_scr[i, 0:ROW_PAD, :] = jnp.zeros((ROW_PAD, RW_COLS), F32)
        if has_state:
            z = jnp.zeros((RW_DH, RW_DH), F32)
            for n, (i, j) in enumerate(chains):
                s_scr[n] = jnp.concatenate([jnp.concatenate([s0_ref[i, 2 * j], z], axis=1),
                                            jnp.concatenate([z, s0_ref[i, 2 * j + 1]], axis=1)], axis=0)
            for i in range(CB):
                xs_scr[i, ROW_PAD - 1:ROW_PAD, :] = prev_ref[i]
        else:
            s_scr[...] = jnp.zeros_like(s_scr)

    zs, last = [], []
    for i in range(CB):
        z = z_refs[i][...]
        xs_scr[i, ROW_PAD:ROW_PAD + C, :] = z
        zs.append(z + (xs_scr[i, ROW_PAD - 1:ROW_PAD - 1 + C, :] - z) * mu_ref[...])
        last.append(z[Lv - 1:Lv, :])
        xs_scr[i, ROW_PAD - 1:ROW_PAD, :] = last[i]
    tok = _rwkv_token_vectors(zs, wa0_ref[...], w2a_ref[...], g2_ref[...], kkp_ref[...], ka_ref[...], rk_ref[...],
                              bd_ref[...])

    srow = lax.broadcasted_iota(jnp.int32, (LANES, LANES), 0)
    scol = lax.broadcasted_iota(jnp.int32, (LANES, LANES), 1)
    same_head = jnp.logical_or(jnp.logical_and(srow < RW_DH, scol < RW_DH),
                               jnp.logical_and(srow >= RW_DH, scol >= RW_DH))
    valid = lax.broadcasted_iota(jnp.int32, (C, LANES), 0) < Lv
    row = lax.broadcasted_iota(jnp.int32, (C2, C2), 0)
    col = lax.broadcasted_iota(jnp.int32, (C2, C2), 1)
    tt = row & (C - 1)
    ss = col & (C - 1)
    top = row < C
    bot = row >= C
    strict = ss < tt
    incl = ss <= tt
    keep0 = jnp.logical_or(jnp.logical_and(top, strict), jnp.logical_and(bot, incl))
    keep1 = jnp.logical_or(jnp.logical_and(top, incl), jnp.logical_and(bot, strict))
    tl = jnp.logical_and(top, col < C)
    br = jnp.logical_and(bot, col >= C)
    eye = (row == col).astype(F32)
    tri = lax.broadcasted_iota(jnp.int32, (C, C), 0) >= lax.broadcasted_iota(jnp.int32, (C, C), 1)
    left = lax.broadcasted_iota(jnp.int32, (C, LANES), 1) < RW_DH
    left2 = lax.broadcasted_iota(jnp.int32, (C2, LANES), 1) < RW_DH
    zeros = jnp.zeros((C, LANES), F32)

    def head_mean(x):
        lsum = jnp.sum(jnp.where(left, x, 0.0), axis=1, keepdims=True)
        rsum = jnp.sum(jnp.where(left, 0.0, x), axis=1, keepdims=True)
        return jnp.where(left, lsum, rsum) * (1.0 / RW_DH)

    ar, m0, m1, vs, bk_end, tots = [], [], [], [], [], []
    for i, j in chains:
        r_i, lw_i, k_i, v_i, kk_i, kka_i, _, _ = tok[i]
        lw = jnp.where(valid, lw_i[:, sl(j)], 0.0)
        cum = _sel_dot(tri, lw)
        tot = cum[C - 1:C, :]
        e_inv = jnp.exp(-cum)
        e_end = jnp.exp(tot - cum)
        kka = jnp.where(valid, kka_i[:, sl(j)], 0.0)
        k = jnp.where(valid, k_i[:, sl(j)], 0.0)
        a_t = -jnp.where(valid, kk_i[:, sl(j)], 0.0) * jnp.exp(cum - lw)
        r_t = r_i[:, sl(j)] * jnp.exp(cum)
        b_t = kka * e_inv
        k_t = k * e_inv
        ar_c = jnp.concatenate([a_t, r_t], axis=0)
        g0 = _dot_nt(jnp.where(left2, ar_c, 0.0), jnp.concatenate([b_t, k_t], axis=0))
        g1 = _dot_nt(jnp.where(left2, 0.0, jnp.concatenate([r_t, a_t], axis=0)), jnp.concatenate([k_t, b_t], axis=0))
        ar.append(ar_c.astype(BF16))
        m0.append(jnp.where(keep0, g0, 0.0))
        m1.append(jnp.where(keep1, g1, 0.0))
        vs.append(jnp.where(valid, v_i[:, sl(j)], 0.0))
        bk_end.append(jnp.concatenate([kka * e_end, k * e_end], axis=0).astype(BF16))
        tots.append(tot)

    s_prev = [s_scr[n] for n in range(nch)]
    w_as = [_dot_nt(ar[n], s_prev[n]) for n in range(nch)]
    x0 = [_dot(m0[n][0:C, :], jnp.concatenate([zeros, vs[n]], axis=0)) for n in range(nch)]
    x1 = [_dot(m1[n][C:, :], jnp.concatenate([vs[n], zeros], axis=0)) for n in range(nch)]

    p = [jnp.where(tl, m0[n], jnp.where(br, m1[n], 0.0)) for n in range(nch)]
    t_inv = [eye + p[n] for n in range(nch)]
    step = 1
    while 2 * step < Lv:
        p = [_dot(p[n], p[n]) for n in range(nch)]
        t_inv = [t_inv[n] + _dot(t_inv[n], p[n]) for n in range(nch)]
        step *= 2

    rhs = [w_as[n][0:C, :] + jnp.where(left, x0[n], x1[n]) for n in range(nch)]
    tu = [_dot(t_inv[n], jnp.concatenate([rhs[n], rhs[n]], axis=0)) for n in range(nch)]
    u = [jnp.where(left, tu[n][0:C, :], tu[n][C:, :]) for n in range(nch)]
    uv = [jnp.concatenate([u[n], vs[n]], axis=0).astype(BF16) for n in range(nch)]
    vu = [jnp.concatenate([vs[n], u[n]], axis=0).astype(BF16) for n in range(nch)]
    o0 = [_dot(m0[n][C:, :], uv[n]) for n in range(nch)]
    o1 = [_dot(m1[n][0:C, :], vu[n]) for n in range(nch)]
    s_new = [_dot_tn(uv[n], bk_end[n]) for n in range(nch)]
    for n in range(nch):
        s_scr[n] = jnp.where(same_head, s_prev[n] * jnp.exp(tots[n]) + s_new[n], 0.0)

    for n, (i, j) in enumerate(chains):
        o = w_as[n][C:, :] + jnp.where(left, o0[n], o1[n])
        oc = o - head_mean(o)
        yn = oc * lax.rsqrt(head_mean(oc * oc) + RW_GN_EPS) * gnw_ref[:, sl(j)] + gnb_ref[:, sl(j)]
        y_ref[i, :, sl(j)] = ((yn + tok[i][7][:, sl(j)]) * tok[i][6][:, sl(j)]).astype(BF16)

    @pl.when(c == nc - 1)
    def _():
        for n, (i, j) in enumerate(chains):
            tile = s_scr[n]
            so_ref[i, 2 * j] = tile[0:RW_DH, 0:RW_DH]
            so_ref[i, 2 * j + 1] = tile[RW_DH:, RW_DH:]
        for i in range(CB):
            shift_ref[i] = last[i]


def _rwkv(zrw, weights, states, layer, *, B, T, C, CB, Lv, row0):
    nc = T // C
    has_state = states is not None
    const2 = lambda b, c: (0, 0)
    wspecs = [pl.BlockSpec(w.shape, const2) for w in weights]
    z_specs = [pl.BlockSpec((C, RW_COLS), functools.partial(lambda b, c, i: (row0 + (b * CB + i) * nc + c, 0), i=i))
               for i in range(CB)]
    in_specs = z_specs + wspecs
    args = [zrw] * CB + list(weights)
    if has_state:
        state, shift = states
        in_specs += [pl.BlockSpec((None, CB, 1, RW_COLS), lambda b, c: (layer, b, 0, 0)),
                     pl.BlockSpec((None, CB, RW_HEADS, RW_DH, RW_DH), lambda b, c: (layer, b, 0, 0, 0))]
        args += [shift.reshape(DEPTH, B, 1, RW_COLS), state]
    y, s, sh = pl.pallas_call(
        functools.partial(_rwkv_chunk_kernel, C=C, CB=CB, Lv=Lv, nc=nc, has_state=has_state),
        grid=(B // CB, nc),
        in_specs=in_specs,
        out_specs=[pl.BlockSpec((CB, C, RW_W), lambda b, c: (b, c, 0)),
                   pl.BlockSpec((CB, RW_HEADS, RW_DH, RW_DH), lambda b, c: (b, 0, 0, 0)),
                   pl.BlockSpec((CB, 1, RW_COLS), lambda b, c: (b, 0, 0))],
        out_shape=[jax.ShapeDtypeStruct((B, T, RW_W), BF16),
                   jax.ShapeDtypeStruct((B, RW_HEADS, RW_DH, RW_DH), F32),
                   jax.ShapeDtypeStruct((B, 1, RW_COLS), F32)],
        scratch_shapes=[pltpu.VMEM((CB * RW_PAIRS, LANES, LANES), F32),
                        pltpu.VMEM((CB, ROW_PAD + C, RW_COLS), F32)],
        compiler_params=_cparams(("parallel", "arbitrary")),
        name="rwkv",
    )(*args)
    return y.reshape(B * T, RW_W), s, sh[:, 0]


def _merge_kernel(x_ref, zg_ref, yrp_ref, ymp_ref, ywp_ref, yrs_ref, yms_ref, yws_ref, wb_ref, wo_ref, o_ref, *, ntp):
    def run(branches):
        acc = None
        for n, y_ref in enumerate(branches):
            proj = jnp.dot(y_ref[...], wb_ref[n], preferred_element_type=F32)
            term = _sigmoid(zg_ref[:, n * D_MODEL:(n + 1) * D_MODEL]) * proj
            acc = term if acc is None else acc + term
        o_ref[...] = x_ref[...] + jnp.dot(acc.astype(BF16), wo_ref[...], preferred_element_type=F32)

    i = pl.program_id(0)

    @pl.when(i < ntp)
    def _():
        run((yrp_ref, ymp_ref, ywp_ref))

    @pl.when(i >= ntp)
    def _():
        run((yrs_ref, yms_ref, yws_ref))


def _merge(x, zgate, y_prompt, y_sample, w_branch, w_out, tm=256):
    n = x.shape[0]
    ntp = y_prompt[0].shape[0] // tm
    tokspec = lambda w: pl.BlockSpec((tm, w), lambda i: (i, 0))
    pspec = pl.BlockSpec((tm, RET_W), lambda i: (jnp.minimum(i, ntp - 1), 0))
    sspec = pl.BlockSpec((tm, RET_W), lambda i: (jnp.maximum(i - ntp, 0), 0))
    return pl.pallas_call(
        functools.partial(_merge_kernel, ntp=ntp),
        grid=(n // tm,),
        in_specs=[tokspec(D_MODEL), tokspec(N_BRANCH * D_MODEL), pspec, pspec, pspec, sspec, sspec, sspec,
                  pl.BlockSpec((N_BRANCH, RET_W, D_MODEL), lambda i: (0, 0, 0)),
                  pl.BlockSpec((D_MODEL, D_MODEL), lambda i: (0, 0))],
        out_specs=tokspec(D_MODEL),
        out_shape=jax.ShapeDtypeStruct((n, D_MODEL), F32),
        compiler_params=_cparams(("arbitrary",)),
        name="merge",
    )(x, zgate, *y_prompt, *y_sample, w_branch, w_out)


def _ffn_kernel(x_ref, g_ref, wg_ref, wu_ref, wd_ref, fin_ref, o_ref, h_scr, acc_scr, *, ne, final):
    e = pl.program_id(1)

    @pl.when(e == 0)
    def _():
        h_scr[...] = _rms(x_ref[...], g_ref[...]).astype(BF16)
        acc_scr[...] = jnp.zeros_like(acc_scr)

    h = h_scr[...]
    hg = jnp.dot(h, wg_ref[0], preferred_element_type=F32)
    hu = jnp.dot(h, wu_ref[0], preferred_element_type=F32)
    acc_scr[...] += jnp.dot((_silu(hg) * hu).astype(BF16), wd_ref[0], preferred_element_type=F32)

    @pl.when(e == ne - 1)
    def _():
        out = x_ref[...] + acc_scr[...]
        if final:
            out = _rms(out, fin_ref[...])
        o_ref[...] = out


def _ffn(x, g, wg, wu, wd, fin, final, tm=512):
    n = x.shape[0]
    ne, _, f = wg.shape
    return pl.pallas_call(
        functools.partial(_ffn_kernel, ne=ne, final=final),
        grid=(n // tm, ne),
        in_specs=[pl.BlockSpec((tm, D_MODEL), lambda i, e: (i, 0)),
                  pl.BlockSpec((1, D_MODEL), lambda i, e: (0, 0)),
                  pl.BlockSpec((1, D_MODEL, f), lambda i, e: (e, 0, 0)),
                  pl.BlockSpec((1, D_MODEL, f), lambda i, e: (e, 0, 0)),
                  pl.BlockSpec((1, f, D_MODEL), lambda i, e: (e, 0, 0)),
                  pl.BlockSpec((1, D_MODEL), lambda i, e: (0, 0))],
        out_specs=pl.BlockSpec((tm, D_MODEL), lambda i, e: (i, 0)),
        out_shape=jax.ShapeDtypeStruct((n, D_MODEL), F32),
        scratch_shapes=[pltpu.VMEM((tm, D_MODEL), BF16), pltpu.VMEM((tm, D_MODEL), F32)],
        compiler_params=_cparams(("parallel", "arbitrary")),
        name="ffn",
    )(x, g, wg, wu, wd, fin)


MOE_TM = 1024
MOE_SUB = LANES
MOE_CAP = 48
MOE_NSUB = MOE_TM // MOE_SUB


def _moe_kernel(x_ref, g_ref, router_ref, wg_ref, wu_ref, wd_ref, fin_ref, o_ref,
                h_scr, acc_scr, comb_scr, combt_scr, xe_scr, *, ne, final):
    e = pl.program_id(1)
    lane = lax.broadcasted_iota(jnp.int32, (MOE_TM, LANES), 1)

    @pl.when(e == 0)
    def _():
        h = _rms(x_ref[...], g_ref[...])
        h_scr[...] = h.astype(BF16)
        acc_scr[...] = jnp.zeros_like(acc_scr)
        logits = jnp.where(lane < ne, _dot_hi(h, router_ref[...]), NEG_BIG)
        m1 = jnp.max(logits, axis=1, keepdims=True)
        i1 = jnp.min(jnp.where(logits == m1, lane, LANES), axis=1, keepdims=True)
        rest = jnp.where(lane == i1, NEG_BIG, logits)
        m2 = jnp.max(rest, axis=1, keepdims=True)
        i2 = jnp.min(jnp.where(rest == m2, lane, LANES), axis=1, keepdims=True)
        e2 = jnp.exp(m2 - m1)
        p1 = 1.0 / (1.0 + e2)
        comb = jnp.where(lane == i1, p1, 0.0) + jnp.where(lane == i2, e2 * p1, 0.0)
        comb_scr[...] = comb
        for s in range(MOE_NSUB):
            combt_scr[s] = comb[s * MOE_SUB:(s + 1) * MOE_SUB, :].T

    sub = lambda s: slice(s * MOE_SUB, (s + 1) * MOE_SUB)
    slot = lambda s: slice(s * MOE_CAP, (s + 1) * MOE_CAP)
    r_i = lax.broadcasted_iota(jnp.int32, (MOE_SUB, MOE_SUB), 0)
    c_i = lax.broadcasted_iota(jnp.int32, (MOE_SUB, MOE_SUB), 1)
    before_row = (c_i < r_i).astype(F32).astype(BF16)
    before_col = (r_i < c_i).astype(F32).astype(BF16)
    slot_row = lax.broadcasted_iota(jnp.int32, (MOE_CAP, MOE_SUB), 0).astype(F32)
    slot_col = lax.broadcasted_iota(jnp.int32, (MOE_SUB, MOE_CAP), 1).astype(F32)

    w_col = jnp.sum(jnp.where(lane == e, comb_scr[...], 0.0), axis=1, keepdims=True)
    hit_col = [(w_col[sub(s), :] > 0.0).astype(F32) for s in range(MOE_NSUB)]
    hit_row = [(combt_scr[s, pl.ds(e, 1), :] > 0.0).astype(F32) for s in range(MOE_NSUB)]
    rank_col = [jnp.dot(before_row, jnp.broadcast_to(hit_col[s], (MOE_SUB, MOE_CAP)).astype(BF16),
                        preferred_element_type=F32) for s in range(MOE_NSUB)]
    rank_row = [jnp.dot(jnp.broadcast_to(hit_row[s], (SUBLANES, MOE_SUB)).astype(BF16), before_col,
                        preferred_element_type=F32)[0:1, :] for s in range(MOE_NSUB)]
    count = jnp.sum(hit_row[0], axis=1, keepdims=True)
    for s in range(1, MOE_NSUB):
        count = jnp.maximum(count, jnp.sum(hit_row[s], axis=1, keepdims=True))
    n_pass = lax.div(jnp.max(count).astype(jnp.int32) + (MOE_CAP - 1), MOE_CAP)

    def one_pass(p, carry):
        base = (p * MOE_CAP).astype(F32)
        for s in range(MOE_NSUB):
            pick = jnp.logical_and(rank_row[s] - base == slot_row, hit_row[s] > 0.0).astype(F32).astype(BF16)
            xe_scr[slot(s), :] = jnp.dot(pick, h_scr[sub(s), :], preferred_element_type=F32).astype(BF16)
        xe = xe_scr[...]
        hg = jnp.dot(xe, wg_ref[0], preferred_element_type=F32)
        hu = jnp.dot(xe, wu_ref[0], preferred_element_type=F32)
        y = jnp.dot((_silu(hg) * hu).astype(BF16), wd_ref[0], preferred_element_type=F32)
        for s in range(MOE_NSUB):
            put = jnp.logical_and(rank_col[s] - base == slot_col, hit_col[s] > 0.0).astype(F32).astype(BF16)
            hi, lo = _split(y[slot(s), :])
            back = jnp.dot(put, hi, preferred_element_type=F32) + jnp.dot(put, lo, preferred_element_type=F32)
            acc_scr[sub(s), :] += w_col[sub(s), :] * back
        return carry

    lax.fori_loop(0, n_pass, one_pass, 0)

    @pl.when(e == ne - 1)
    def _():
        out = x_ref[...] + acc_scr[...]
        if final:
            out = _rms(out, fin_ref[...])
        o_ref[...] = out


def _moe(x, g, wg, wu, wd, router, fin, final):
    n = x.shape[0]
    ne, _, f = wg.shape
    return pl.pallas_call(
        functools.partial(_moe_kernel, ne=ne, final=final),
        grid=(n // MOE_TM, ne),
        in_specs=[pl.BlockSpec((MOE_TM, D_MODEL), lambda i, e: (i, 0)),
                  pl.BlockSpec((1, D_MODEL), lambda i, e: (0, 0)),
                  pl.BlockSpec((D_MODEL, LANES), lambda i, e: (0, 0)),
                  pl.BlockSpec((1, D_MODEL, f), lambda i, e: (e, 0, 0)),
                  pl.BlockSpec((1, D_MODEL, f), lambda i, e: (e, 0, 0)),
                  pl.BlockSpec((1, f, D_MODEL), lambda i, e: (e, 0, 0)),
                  pl.BlockSpec((1, D_MODEL), lambda i, e: (0, 0))],
        out_specs=pl.BlockSpec((MOE_TM, D_MODEL), lambda i, e: (i, 0)),
        out_shape=jax.ShapeDtypeStruct((n, D_MODEL), F32),
        scratch_shapes=[pltpu.VMEM((MOE_TM, D_MODEL), BF16), pltpu.VMEM((MOE_TM, D_MODEL), F32),
                        pltpu.VMEM((MOE_TM, LANES), F32), pltpu.VMEM((MOE_NSUB, LANES, MOE_SUB), F32),
                        pltpu.VMEM((MOE_NSUB * MOE_CAP, D_MODEL), BF16)],
        compiler_params=_cparams(("parallel", "arbitrary")),
        name="moe",
    )(x, g, router, wg, wu, wd, fin)


def _row(v):
    return v.reshape(1, -1)


def kernel(x_prompt, x_sample, state_ret, state_mlstm_C, state_mlstm_n, state_mlstm_m, state_mlstm_conv, state_rwkv, state_rwkv_shift, norm_mix, w_in, ret_gn, ml_conv_w, ml_conv_b, ml_wq, ml_wk, ml_bi, ml_bf, ml_gn, ml_skip, rw_mu, rw_w0, rw_w2, rw_a0, rw_a2, rw_g2, rw_kk, rw_ka, rw_rk, rw_gn_w, rw_gn_b, w_branch, w_out, norm_ffn, ffn_w_gate, ffn_w_up, ffn_w_down, moe_router, moe_w_gate, moe_w_up, moe_w_down, final_norm):
    nc_p = SEQ // CHUNK
    xs = jnp.pad(x_sample, ((0, 0), (0, S_PAD - DEC_SEQ), (0, 0)))
    x = jnp.concatenate([x_prompt.reshape(N_PROMPT, D_MODEL), xs.reshape(N_SAMPLE, D_MODEL)], axis=0)

    pos_p = jnp.arange(SEQ, dtype=jnp.int32)
    pos_s = PAST_LEN + jnp.arange(S_PAD, dtype=jnp.int32)
    ret_tab_p = _retention_tables(CHUNK, CHUNK, pos_p, 1)
    ret_tab_s = _retention_tables(S_PAD, DEC_SEQ, pos_s, RET_NB_SAMPLE)
    head_of = jnp.arange(RW_W) // RW_DH
    bd64 = (head_of[:, None] == head_of[None, :]).astype(BF16)
    m0_all = jnp.pad(state_mlstm_m, ((0, 0), (0, 0), (0, LANES - ML_HEADS))).reshape(DEPTH, DEC_BATCH, 1, LANES)

    o_rq = 0
    o_mx = o_rq + RET_COLS
    o_mi = o_mx + 2 * ML_W
    o_mo = o_mi + 2 * ML_HEADS
    o_rw = o_mo + ML_W
    o_gate = o_rw + RW_COLS

    new_p, new_s = [], []
    for l in range(DEPTH):
        w = w_in[l]
        w_ret = w[:, o_rq:o_mx].astype(BF16)
        w_ml = jnp.concatenate([w[:, o_mx:o_mi], w[:, o_mo:o_rw], w[:, o_mi:o_mo],
                                jnp.zeros((D_MODEL, ML_GATE_W - 2 * ML_HEADS), F32)], axis=1).astype(BF16)
        w_rw = w[:, o_rw:o_gate].astype(BF16)
        w_gate = w[:, o_gate:].astype(BF16)
        g_mix = _row(norm_mix[l])
        zret = _norm_matmul(x, g_mix, w_ret)
        zml = _norm_matmul(x, g_mix, w_ml)
        zrw = _norm_matmul(x, g_mix, w_rw)
        zgate = _norm_matmul(x, g_mix, w_gate)

        gn = _row(ret_gn[l])
        yr_p, ret_p = _retention(zret, ret_tab_p, gn, None, l, B=BATCH, nc=nc_p, L=CHUNK, nb=1, row0=0)
        yr_s, ret_s = _retention(zret, ret_tab_s, gn, state_ret, l, B=DEC_BATCH, nc=1, L=S_PAD, nb=RET_NB_SAMPLE,
                                 row0=N_PROMPT // (RET_NB_SAMPLE * S_PAD))

        gate_bias = jnp.concatenate([ml_bi[l], ml_bf[l], jnp.zeros((ML_GATE_W - 2 * ML_HEADS,), F32)])
        ml_weights = (ml_conv_w[l], _row(ml_conv_b[l]), ml_wq[l].astype(BF16), ml_wk[l].astype(BF16),
                      _row(gate_bias), _row(ml_gn[l]), _row(ml_skip[l]))
        ym_p, c_p, n_p, m_p, buf_p = _mlstm(zml, ml_weights, None, l, B=BATCH, nc=nc_p, L=CHUNK, Lv=CHUNK, nb=1,
                                            row0=0)
        ym_s, c_s, n_s, m_s, buf_s = _mlstm(zml, ml_weights, (state_mlstm_C, state_mlstm_n, m0_all, state_mlstm_conv),
                                            l, B=DEC_BATCH, nc=1, L=S_PAD, Lv=DEC_SEQ, nb=ML_NB_SAMPLE,
                                            row0=N_PROMPT // (ML_NB_SAMPLE * S_PAD))

        w2a = jnp.zeros((RW_LORA_W + RW_LORA_A, 2 * RW_W), F32)
        w2a = w2a.at[:RW_LORA_W, :RW_W].set(rw_w2[l]).at[RW_LORA_W:, RW_W:].set(rw_a2[l])
        rw_weights = (_row(rw_mu[l]), _row(jnp.concatenate([rw_w0[l], rw_a0[l]])), w2a.astype(BF16),
                      rw_g2[l].astype(BF16), _row(rw_kk[l]), _row(rw_ka[l]), _row(rw_rk[l]), bd64,
                      _row(rw_gn_w[l]), _row(rw_gn_b[l]))
        yw_p, rws_p, shift_p = _rwkv(zrw, rw_weights, None, l, B=BATCH, T=SEQ, C=RW_C, CB=RW_CB_PROMPT, Lv=RW_C,
                                     row0=0)
        yw_s, rws_s, shift_s = _rwkv(zrw, rw_weights, (state_rwkv, state_rwkv_shift), l, B=DEC_BATCH, T=S_PAD,
                                     C=S_PAD, CB=RW_CB_SAMPLE, Lv=DEC_SEQ, row0=N_PROMPT // S_PAD)

        x = _merge(x, zgate, (yr_p, ym_p, yw_p), (yr_s, ym_s, yw_s), w_branch[l].astype(BF16), w_out[l].astype(BF16))

        i = l // 2
        g_ffn = _row(norm_ffn[l])
        fin = _row(final_norm)
        final = l == DEPTH - 1
        if l % 2 == 0:
            halves = D_FF // D_FF_EXPERT
            wg = ffn_w_gate[i].reshape(D_MODEL, halves, D_FF_EXPERT).transpose(1, 0, 2).astype(BF16)
            wu = ffn_w_up[i].reshape(D_MODEL, halves, D_FF_EXPERT).transpose(1, 0, 2).astype(BF16)
            wd = ffn_w_down[i].reshape(halves, D_FF_EXPERT, D_MODEL).astype(BF16)
            x = _ffn(x, g_ffn, wg, wu, wd, fin, final)
        else:
            router = jnp.pad(moe_router[i], ((0, 0), (0, LANES - N_EXPERTS)))
            x = _moe(x, g_ffn, moe_w_gate[i].astype(BF16), moe_w_up[i].astype(BF16), moe_w_down[i].astype(BF16),
                     router, fin, final)

        new_p.append((ret_p, c_p, n_p, m_p[:, 0, :ML_HEADS], buf_p, rws_p, shift_p))
        new_s.append((ret_s, c_s, n_s, m_s[:, 0, :ML_HEADS], buf_s, rws_s, shift_s))

    y_prompt = x[:N_PROMPT].reshape(BATCH, SEQ, D_MODEL)
    y_sample = x[N_PROMPT:].reshape(DEC_BATCH, S_PAD, D_MODEL)[:, :DEC_SEQ]
    st_p = tuple(jnp.stack([st[j] for st in new_p]) for j in range(7))
    st_s = tuple(jnp.stack([st[j] for st in new_s]) for j in range(7))
    return (y_prompt, y_sample) + st_p + st_s
```

```python
import functools

import jax
import jax.numpy as jnp
from jax import lax
from jax.experimental import pallas as pl
from jax.experimental.pallas import tpu as pltpu

D_MODEL = 1024
BATCH = 8
SEQ = 2048
DEPTH = 2
DEC_BATCH = 128
DEC_SEQ = 4
PAST_LEN = 16384
RET_HEADS = 4
RET_DK = 64
RET_DV = 128
ML_HEADS = 4
ML_DH = 128
CONV_W = 4
RW_HEADS = 8
RW_DH = 64
RW_LORA_W = 64
RW_LORA_A = 64
RW_LORA_G = 128
RET_W = RET_HEADS * RET_DV
ML_W = ML_HEADS * ML_DH
RW_W = RW_HEADS * RW_DH
N_BRANCH = 3
RW_COLS = 3 * RW_W + RW_LORA_W + RW_LORA_A + RW_LORA_G
D_FF = 2816
N_EXPERTS = 8
D_FF_EXPERT = 1408
CHUNK = 128
NORM_EPS = 1e-6
GN_EPS = 1e-5
RW_GN_EPS = 64e-5
ROPE_BASE = 10000.0
RW_DECAY_SCALE = 0.606531

LANES = 128
SUBLANES = 8
S_PAD = 16
N_PROMPT = BATCH * SEQ
N_SAMPLE = DEC_BATCH * S_PAD
N_TOK = N_PROMPT + N_SAMPLE
ML_GATE_W = LANES
ML_COLS = 3 * ML_W + ML_GATE_W
RET_COLS = 2 * RET_HEADS * RET_DK + 2 * RET_W
NEG_BIG = -1e30
VMEM_LIMIT = 56 * 1024 * 1024

RW_C = 64
RW_PAIRS = RW_HEADS // 2
RW_CB_PROMPT = 4
RW_CB_SAMPLE = 4
RET_NB_SAMPLE = 8
ML_NB_SAMPLE = 4

F32 = jnp.float32
BF16 = jnp.bfloat16
HI = lax.Precision.HIGHEST


def _cparams(sem):
    return pltpu.CompilerParams(dimension_semantics=sem, vmem_limit_bytes=VMEM_LIMIT)


def _sigmoid(x):
    return 1.0 / (1.0 + jnp.exp(-x))


def _silu(x):
    return x * _sigmoid(x)


def _rms(x, g):
    return x * lax.rsqrt(jnp.mean(x * x, axis=-1, keepdims=True) + NORM_EPS) * g


def _dot(a, b):
    return jnp.dot(a.astype(BF16), b.astype(BF16), preferred_element_type=F32)


def _dot_nt(a, b):
    return lax.dot_general(a.astype(BF16), b.astype(BF16), (((1,), (1,)), ((), ())), preferred_element_type=F32)


def _dot_tn(a, b):
    return lax.dot_general(a.astype(BF16), b.astype(BF16), (((0,), (0,)), ((), ())), preferred_element_type=F32)


def _dot_hi(a, b):
    return jnp.dot(a, b, preferred_element_type=F32, precision=HI)


def _split(x):
    hi = x.astype(BF16)
    return hi, (x - hi.astype(F32)).astype(BF16)


def _as_bf16_mask(m):
    return m if m.dtype == BF16 else m.astype(F32).astype(BF16)


def _sel_dot(m, x):
    hi, lo = _split(x)
    m = _as_bf16_mask(m)
    return jnp.dot(m, hi, preferred_element_type=F32) + jnp.dot(m, lo, preferred_element_type=F32)


def _dot_sel(x, m):
    hi, lo = _split(x)
    m = _as_bf16_mask(m)
    return jnp.dot(hi, m, preferred_element_type=F32) + jnp.dot(lo, m, preferred_element_type=F32)


def _sel_dot_nt(m, x):
    hi, lo = _split(x)
    m = _as_bf16_mask(m)
    dn = (((1,), (1,)), ((), ()))
    return (lax.dot_general(m, hi, dn, preferred_element_type=F32)
            + lax.dot_general(m, lo, dn, preferred_element_type=F32))


def _norm_matmul_kernel(x_ref, g_ref, w_ref, *o_refs, bounds):
    h = _rms(x_ref[...], g_ref[...])
    z = jnp.dot(h.astype(BF16), w_ref[...], preferred_element_type=F32)
    for o_ref, (lo, hi) in zip(o_refs, bounds):
        o_ref[...] = z[:, lo:hi].astype(o_ref.dtype)


def _norm_matmul(x, g, w, outs, tm=512):
    n, d = x.shape
    c = w.shape[1]
    bounds, lo = [], 0
    for width, _ in outs:
        bounds.append((lo, lo + width))
        lo += width
    assert lo == c
    return pl.pallas_call(
        functools.partial(_norm_matmul_kernel, bounds=tuple(bounds)),
        grid=(n // tm,),
        in_specs=[pl.BlockSpec((tm, d), lambda i: (i, 0)),
                  pl.BlockSpec((1, d), lambda i: (0, 0)),
                  pl.BlockSpec((d, c), lambda i: (0, 0))],
        out_specs=[pl.BlockSpec((tm, width), lambda i: (i, 0)) for width, _ in outs],
        out_shape=[jax.ShapeDtypeStruct((n, width), dt) for width, dt in outs],
        compiler_params=_cparams(("parallel",)),
        name="norm_matmul",
    )(x, g, w)


def _retention_kernel(*refs, L, nc, nb, has_state):
    if has_state:
        (q_ref, k_ref, v_ref, g_ref, cos_ref, sin_ref, di_ref, dq_ref, dk_ref, dc_ref, gn_ref, s0_ref,
         y_ref, so_ref, s_scr) = refs
    else:
        (q_ref, k_ref, v_ref, g_ref, cos_ref, sin_ref, di_ref, dq_ref, dk_ref, dc_ref, gn_ref,
         y_ref, so_ref, s_scr) = refs
    c = pl.program_id(1)

    @pl.when(c == 0)
    def _():
        if has_state:
            s_scr[...] = s0_ref[...]
        else:
            s_scr[...] = jnp.zeros_like(s_scr)

    qk_w = RET_HEADS * RET_DK
    half = RET_DK // 2
    lane = lax.broadcasted_iota(jnp.int32, (nb * L, qk_w), 1)
    first_half = (lane % RET_DK) < half
    cos = cos_ref[...]
    sin = sin_ref[...]

    def rot(x):
        swapped = jnp.where(first_half, pltpu.roll(x, qk_w - half, 1), pltpu.roll(x, half, 1))
        return x * cos + swapped * sin

    q = rot(q_ref[...].astype(F32))
    k = rot(k_ref[...].astype(F32)) * (RET_DK ** -0.5)
    dq = dq_ref[...]
    dk = dk_ref[...]
    dc = dc_ref[...]
    chains = [(i, h) for i in range(nb) for h in range(RET_HEADS)]
    rows = lambda i: slice(i * L, (i + 1) * L)
    kcols = lambda h: slice(h * RET_DK, (h + 1) * RET_DK)
    vcols = lambda h: slice(h * RET_DV, (h + 1) * RET_DV)

    qh = [q[rows(i), kcols(h)].astype(BF16) for i, h in chains]
    kh = [k[rows(i), kcols(h)] for i, h in chains]
    vh = [v_ref[rows(i), vcols(h)].astype(BF16) for i, h in chains]
    s = [_dot_nt(qh[n], kh[n]) * di_ref[h] for n, (i, h) in enumerate(chains)]
    s_prev = [s_scr[i, h] for i, h in chains]
    cross = [_dot(qh[n], s_prev[n]) for n in range(len(chains))]
    upd = [_dot_tn(kh[n] * dk[:, h:h + 1], vh[n]) for n, (i, h) in enumerate(chains)]
    inner = [_dot(s[n], vh[n]) for n in range(len(chains))]
    for n, (i, h) in enumerate(chains):
        s_scr[i, h] = s_prev[n] * dc[:, h:h + 1] + upd[n]
        o = inner[n] + cross[n] * dq[:, h:h + 1]
        oc = o - jnp.mean(o, axis=-1, keepdims=True)
        yn = oc * lax.rsqrt(jnp.mean(oc * oc, axis=-1, keepdims=True) + GN_EPS) * gn_ref[:, vcols(h)]
        y_ref[rows(i), vcols(h)] = (_silu(g_ref[rows(i), vcols(h)].astype(F32)) * yn).astype(BF16)

    @pl.when(c == nc - 1)
    def _():
        so_ref[...] = s_scr[...]


def _retention(zret, tables, gn, state, layer, *, B, nc, L, nb, row0):
    cos, sin, di, dq, dk, dc = tables
    has_state = state is not None
    qk_w = RET_HEADS * RET_DK
    R = nb * L
    row = lambda b, c: row0 + b * nc + c
    st = pl.BlockSpec((nb, RET_HEADS, RET_DK, RET_DV), lambda b, c: (b, 0, 0, 0))
    in_specs = [pl.BlockSpec((R, qk_w), lambda b, c: (row(b, c), 0)),
                pl.BlockSpec((R, qk_w), lambda b, c: (row(b, c), 1)),
                pl.BlockSpec((R, RET_W), lambda b, c: (row(b, c), 1)),
                pl.BlockSpec((R, RET_W), lambda b, c: (row(b, c), 2)),
                pl.BlockSpec((R, qk_w), lambda b, c: (c, 0)),
                pl.BlockSpec((R, qk_w), lambda b, c: (c, 0)),
                pl.BlockSpec((RET_HEADS, L, L), lambda b, c: (0, 0, 0)),
                pl.BlockSpec((L, RET_HEADS), lambda b, c: (0, 0)),
                pl.BlockSpec((L, RET_HEADS), lambda b, c: (0, 0)),
                pl.BlockSpec((1, RET_HEADS), lambda b, c: (0, 0)),
                pl.BlockSpec((1, RET_W), lambda b, c: (0, 0))]
    args = [zret, zret, zret, zret, cos, sin, di, dq, dk, dc, gn]
    if has_state:
        in_specs.append(pl.BlockSpec((None, nb, RET_HEADS, RET_DK, RET_DV), lambda b, c: (layer, b, 0, 0, 0)))
        args.append(state)
    return pl.pallas_call(
        functools.partial(_retention_kernel, L=L, nc=nc, nb=nb, has_state=has_state),
        grid=(B // nb, nc),
        in_specs=in_specs,
        out_specs=[pl.BlockSpec((R, RET_W), lambda b, c: (b * nc + c, 0)), st],
        out_shape=[jax.ShapeDtypeStruct((B * nc * L, RET_W), BF16),
                   jax.ShapeDtypeStruct((B, RET_HEADS, RET_DK, RET_DV), F32)],
        scratch_shapes=[pltpu.VMEM((nb, RET_HEADS, RET_DK, RET_DV), F32)],
        compiler_params=_cparams(("parallel", "arbitrary")),
        name="retention",
    )(*args)


def _retention_tables(L, Lv, pos, nb):
    half = RET_DK // 2
    inv = ROPE_BASE ** (-jnp.arange(half, dtype=F32) / half)
    ang = pos.astype(F32)[:, None] * inv[None, :]
    cos = jnp.tile(jnp.concatenate([jnp.cos(ang), jnp.cos(ang)], axis=1), (nb, RET_HEADS))
    sin = jnp.tile(jnp.concatenate([-jnp.sin(ang), jnp.sin(ang)], axis=1), (nb, RET_HEADS))
    log_gamma = jnp.log1p(-(2.0 ** (-5.0 - jnp.arange(RET_HEADS, dtype=F32))))
    idx = jnp.arange(L, dtype=F32)
    diff = idx[:, None] - idx[None, :]
    causal = diff >= 0
    di = jnp.where(causal[None], jnp.exp(jnp.where(causal, diff, 0.0)[None] * log_gamma[:, None, None]), 0.0)
    dq = jnp.exp((idx[:, None] + 1.0) * log_gamma[None, :])
    dk = jnp.where((idx < Lv)[:, None], jnp.exp((Lv - 1.0 - idx)[:, None] * log_gamma[None, :]), 0.0)
    dc = jnp.exp(Lv * log_gamma)[None, :]
    return cos, sin, di, dq, dk, dc


ROW_PAD = SUBLANES


def _mlstm_kernel(*refs, L, Lv, nc, nb, has_state):
    if has_state:
        (mx_ref, mv_ref, mo_ref, gz_ref, cw_ref, cb_ref, wq_ref, wk_ref, gb_ref, gn_ref, skip_ref,
         c0_ref, n0_ref, m0_ref, buf0_ref,
         y_ref, co_ref, no_ref, mout_ref, bufo_ref, xp_scr, c_scr, n_scr, m_scr) = refs
    else:
        (mx_ref, mv_ref, mo_ref, gz_ref, cw_ref, cb_ref, wq_ref, wk_ref, gb_ref, gn_ref, skip_ref,
         y_ref, co_ref, no_ref, mout_ref, bufo_ref, xp_scr, c_scr, n_scr, m_scr) = refs
    c = pl.program_id(1)
    tail = CONV_W - 1

    @pl.when(c == 0)
    def _():
        for i in range(nb):
            xp_scr[i, 0:ROW_PAD, :] = jnp.zeros((ROW_PAD, ML_W), F32)
        if has_state:
            c_scr[...] = c0_ref[...]
            n_scr[...] = n0_ref[...]
            m_scr[...] = m0_ref[...]
            for i in range(nb):
                xp_scr[i, ROW_PAD - tail:ROW_PAD, :] = buf0_ref[i]
        else:
            c_scr[...] = jnp.zeros_like(c_scr)
            n_scr[...] = jnp.zeros_like(n_scr)
            m_scr[...] = jnp.zeros_like(m_scr)

    rows = lambda i: slice(i * L, (i + 1) * L)
    cols = lambda h: slice(h * ML_DH, (h + 1) * ML_DH)
    lane = lax.broadcasted_iota(jnp.int32, (L, ML_GATE_W), 1)
    rowi = lax.broadcasted_iota(jnp.int32, (L, ML_GATE_W), 0)
    tri = lax.broadcasted_iota(jnp.int32, (L, L), 0) >= lax.broadcasted_iota(jnp.int32, (L, L), 1)
    eye = (lax.broadcasted_iota(jnp.int32, (SUBLANES, ML_GATE_W), 0)
           == lax.broadcasted_iota(jnp.int32, (SUBLANES, ML_GATE_W), 1))

    xc, new_tail, gc, fc, g_rows, f_rows = [], [], [], [], [], []
    for i in range(nb):
        xp_scr[i, ROW_PAD:ROW_PAD + L, :] = mx_ref[rows(i), :].astype(F32)
        acc = cb_ref[...]
        for j in range(CONV_W):
            acc = acc + cw_ref[j:j + 1, :] * xp_scr[i, ROW_PAD - tail + j:ROW_PAD - tail + j + L, :]
        xc.append(_silu(acc))
        new_tail.append(xp_scr[i, ROW_PAD + Lv - tail:ROW_PAD + Lv, :])
        xp_scr[i, ROW_PAD - tail:ROW_PAD, :] = new_tail[i]
        gz = gz_ref[rows(i), :] + gb_ref[...]
        logsig = jnp.minimum(gz, 0.0) - jnp.log1p(jnp.exp(-jnp.abs(gz)))
        g = jnp.where(lane < ML_HEADS, gz, logsig)
        if Lv < L:
            g = jnp.where(rowi < Lv, g, jnp.where(lane < ML_HEADS, NEG_BIG, 0.0))
        gc.append(g)
        fc.append(_sel_dot(tri, g))
        g_rows.append(_sel_dot_nt(eye, g))
        f_rows.append(_sel_dot_nt(eye, fc[i]))

    chains = [(i, h) for i in range(nb) for h in range(ML_HEADS)]
    nch = len(chains)
    xh = [xc[i][:, cols(h)] for i, h in chains]
    xh_b = [x.astype(BF16) for x in xh]
    q = [jnp.dot(xh_b[n], wq_ref[h], preferred_element_type=F32) for n, (i, h) in enumerate(chains)]
    k = [jnp.dot(xh_b[n], wk_ref[h], preferred_element_type=F32) * (ML_DH ** -0.5) for n, (i, h) in enumerate(chains)]
    q_b = [x.astype(BF16) for x in q]
    v_b = [mv_ref[rows(i), cols(h)].astype(BF16) for i, h in chains]
    qk = [_dot_nt(q_b[n], k[n]) for n in range(nch)]
    c_prev = [c_scr[i, h] for i, h in chains]
    qc = [_dot(q_b[n], c_prev[n]) for n in range(nch)]

    s, a_inter, m_t, kw, a_old, m_new = [], [], [], [], [], []
    for n, (i, h) in enumerate(chains):
        f_col = fc[i][:, ML_HEADS + h:ML_HEADS + h + 1]
        i_col = gc[i][:, h:h + 1]
        f_row = f_rows[i][ML_HEADS + h:ML_HEADS + h + 1, :]
        i_row = g_rows[i][h:h + 1, :]
        m_prev = m_scr[i, 0:1, h:h + 1]
        dlog = jnp.where(tri, f_col - f_row + i_row, NEG_BIG)
        inter = f_col + m_prev
        m_t.append(jnp.maximum(jnp.max(dlog, axis=1, keepdims=True), inter))
        s.append(qk[n] * jnp.exp(dlog - m_t[n]))
        a_inter.append(jnp.exp(inter - m_t[n]))
        f_last = fc[i][Lv - 1:Lv, ML_HEADS + h:ML_HEADS + h + 1]
        m_new.append(jnp.maximum(f_last + m_prev, jnp.max(f_last - f_row + i_row, axis=1, keepdims=True)))
        kw.append(k[n] * jnp.exp(f_last - f_col + i_col - m_new[n]))
        a_old.append(jnp.exp(f_last + m_prev - m_new[n]))

    sv = [_dot(s[n], v_b[n]) for n in range(nch)]
    ktv = [_dot_tn(kw[n], v_b[n]) for n in range(nch)]
    for n, (i, h) in enumerate(chains):
        n_prev = n_scr[i, h:h + 1, :]
        num = sv[n] + a_inter[n] * qc[n]
        den = jnp.sum(s[n], axis=1, keepdims=True) + a_inter[n] * jnp.sum(q[n] * n_prev, axis=1, keepdims=True)
        hh = num * (1.0 / jnp.maximum(jnp.abs(den), jnp.exp(-m_t[n])))
        c_scr[i, h] = a_old[n] * c_prev[n] + ktv[n]
        n_scr[i, h:h + 1, :] = a_old[n] * n_prev + jnp.sum(kw[n], axis=0, keepdims=True)
        m_scr[i, 0:1, h:h + 1] = m_new[n]
        hc = hh - jnp.mean(hh, axis=-1, keepdims=True)
        hn = hc * lax.rsqrt(jnp.mean(hc * hc, axis=-1, keepdims=True) + GN_EPS) * gn_ref[:, cols(h)]
        gate = _sigmoid(mo_ref[rows(i), cols(h)].astype(F32))
        y_ref[rows(i), cols(h)] = (gate * (hn + skip_ref[:, cols(h)] * xh[n])).astype(BF16)

    @pl.when(c == nc - 1)
    def _():
        co_ref[...] = c_scr[...]
        no_ref[...] = n_scr[...]
        mout_ref[...] = m_scr[...]
        for i in range(nb):
            bufo_ref[i] = new_tail[i]


def _mlstm(zml, zml_gates, weights, state, layer, *, B, nc, L, Lv, nb, row0):
    cw, cb, wq, wk, gb, gn, skip = weights
    has_state = state is not None
    R = nb * L
    row = lambda b, c: row0 + b * nc + c
    const2 = lambda b, c: (0, 0)
    in_specs = [pl.BlockSpec((R, ML_W), lambda b, c: (row(b, c), 0)),
                pl.BlockSpec((R, ML_W), lambda b, c: (row(b, c), 1)),
                pl.BlockSpec((R, ML_W), lambda b, c: (row(b, c), 2)),
                pl.BlockSpec((R, ML_GATE_W), lambda b, c: (row(b, c), 0)),
                pl.BlockSpec((CONV_W, ML_W), const2),
                pl.BlockSpec((1, ML_W), const2),
                pl.BlockSpec((ML_HEADS, ML_DH, ML_DH), lambda b, c: (0, 0, 0)),
                pl.BlockSpec((ML_HEADS, ML_DH, ML_DH), lambda b, c: (0, 0, 0)),
                pl.BlockSpec((1, ML_GATE_W), const2),
                pl.BlockSpec((1, ML_W), const2),
                pl.BlockSpec((1, ML_W), const2)]
    args = [zml, zml, zml, zml_gates, cw, cb, wq, wk, gb, gn, skip]
    st_specs = [pl.BlockSpec((nb, ML_HEADS, ML_DH, ML_DH), lambda b, c: (b, 0, 0, 0)),
                pl.BlockSpec((nb, ML_HEADS, ML_DH), lambda b, c: (b, 0, 0)),
                pl.BlockSpec((nb, 1, LANES), lambda b, c: (b, 0, 0)),
                pl.BlockSpec((nb, CONV_W - 1, ML_W), lambda b, c: (b, 0, 0))]
    if has_state:
        in_specs += [pl.BlockSpec((None, nb, ML_HEADS, ML_DH, ML_DH), lambda b, c: (layer, b, 0, 0, 0)),
                     pl.BlockSpec((None, nb, ML_HEADS, ML_DH), lambda b, c: (layer, b, 0, 0)),
                     pl.BlockSpec((None, nb, 1, LANES), lambda b, c: (layer, b, 0, 0)),
                     pl.BlockSpec((None, nb, CONV_W - 1, ML_W), lambda b, c: (layer, b, 0, 0))]
        args += list(state)
    return pl.pallas_call(
        functools.partial(_mlstm_kernel, L=L, Lv=Lv, nc=nc, nb=nb, has_state=has_state),
        grid=(B // nb, nc),
        in_specs=in_specs,
        out_specs=[pl.BlockSpec((R, ML_W), lambda b, c: (b * nc + c, 0))] + st_specs,
        out_shape=[jax.ShapeDtypeStruct((B * nc * L, ML_W), BF16),
                   jax.ShapeDtypeStruct((B, ML_HEADS, ML_DH, ML_DH), F32),
                   jax.ShapeDtypeStruct((B, ML_HEADS, ML_DH), F32),
                   jax.ShapeDtypeStruct((B, 1, LANES), F32),
                   jax.ShapeDtypeStruct((B, CONV_W - 1, ML_W), F32)],
        scratch_shapes=[pltpu.VMEM((nb, ROW_PAD + L, ML_W), F32),
                        pltpu.VMEM((nb, ML_HEADS, ML_DH, ML_DH), F32),
                        pltpu.VMEM((nb, ML_HEADS, ML_DH), F32),
                        pltpu.VMEM((nb, 1, LANES), F32)],
        compiler_params=_cparams(("parallel", "arbitrary")),
        name="mlstm",
    )(*args)


def _rwkv_token_vectors(zs, wa0, w2a, g2, kkp, ka, rk, bd):
    n = len(zs)
    r = [z[:, 0:RW_W] for z in zs]
    kr = [z[:, RW_W:2 * RW_W] for z in zs]
    vr = [z[:, 2 * RW_W:3 * RW_W] for z in zs]
    wa = [z[:, 3 * RW_W:3 * RW_W + RW_LORA_W + RW_LORA_A] for z in zs]
    gl = [z[:, 3 * RW_W + RW_LORA_W + RW_LORA_A:] for z in zs]
    lane = lax.broadcasted_iota(jnp.int32, wa[0].shape, 1)
    lora = [jnp.dot(jnp.where(lane < RW_LORA_W, jnp.tanh(wa[i]), wa[i]).astype(BF16), w2a,
                    preferred_element_type=F32) + wa0 for i in range(n)]
    g = [jnp.dot(_sigmoid(gl[i]).astype(BF16), g2, preferred_element_type=F32) for i in range(n)]
    kk = [kr[i] * kkp for i in range(n)]
    kk_ss = [_dot_sel(kk[i] * kk[i], bd) for i in range(n)]
    a = [_sigmoid(lora[i][:, RW_W:]) for i in range(n)]
    k2 = [kr[i] * (1.0 + (a[i] - 1.0) * ka) for i in range(n)]
    rk_sum = [_dot_sel(r[i] * k2[i] * rk, bd) for i in range(n)]
    out = []
    for i in range(n):
        kk_n = kk[i] / jnp.maximum(jnp.sqrt(kk_ss[i]), 1e-12)
        out.append((r[i], -RW_DECAY_SCALE * _sigmoid(lora[i][:, 0:RW_W]), k2[i], vr[i], kk_n, kk_n * a[i], g[i],
                    rk_sum[i] * vr[i]))
    return out


def _rwkv_chunk_kernel(*refs, C, CB, Lv, nc, has_state):
    z_refs = refs[:CB]
    refs = refs[CB:]
    if has_state:
        (mu_ref, wa0_ref, w2a_ref, g2_ref, kkp_ref, ka_ref, rk_ref, bd_ref, gnw_ref, gnb_ref, prev_ref, s0_ref,
         y_ref, so_ref, shift_ref, s_scr, xs_scr) = refs
    else:
        (mu_ref, wa0_ref, w2a_ref, g2_ref, kkp_ref, ka_ref, rk_ref, bd_ref, gnw_ref, gnb_ref,
         y_ref, so_ref, shift_ref, s_scr, xs_scr) = refs
    c = pl.program_id(1)
    C2 = 2 * C
    chains = [(i, j) for i in range(CB) for j in range(RW_PAIRS)]
    nch = len(chains)
    sl = lambda j: slice(j * LANES, (j + 1) * LANES)

    @pl.when(c == 0)
    def _():
        for i in range(CB):
            xs_scr[i, 0:ROW_PAD, :] = jnp.zeros((ROW_PAD, RW_COLS), F32)
        if has_state:
            z = jnp.zeros((RW_DH, RW_DH), F32)
            for n, (i, j) in enumerate(chains):
                s_scr[n] = jnp.concatenate([jnp.concatenate([s0_ref[i, 2 * j], z], axis=1),
                                            jnp.concatenate([z, s0_ref[i, 2 * j + 1]], axis=1)], axis=0)
            for i in range(CB):
                xs_scr[i, ROW_PAD - 1:ROW_PAD, :] = prev_ref[i]
        else:
            s_scr[...] = jnp.zeros_like(s_scr)

    zs, last = [], []
    for i in range(CB):
        z = z_refs[i][...].astype(F32)
        xs_scr[i, ROW_PAD:ROW_PAD + C, :] = z
        zs.append(z + (xs_scr[i, ROW_PAD - 1:ROW_PAD - 1 + C, :] - z) * mu_ref[...])
        last.append(z[Lv - 1:Lv, :])
        xs_scr[i, ROW_PAD - 1:ROW_PAD, :] = last[i]
    tok = _rwkv_token_vectors(zs, wa0_ref[...], w2a_ref[...], g2_ref[...], kkp_ref[...], ka_ref[...], rk_ref[...],
                              bd_ref[...])

    srow = lax.broadcasted_iota(jnp.int32, (LANES, LANES), 0)
    scol = lax.broadcasted_iota(jnp.int32, (LANES, LANES), 1)
    same_head = jnp.logical_or(jnp.logical_and(srow < RW_DH, scol < RW_DH),
                               jnp.logical_and(srow >= RW_DH, scol >= RW_DH))
    valid = lax.broadcasted_iota(jnp.int32, (C, LANES), 0) < Lv
    row = lax.broadcasted_iota(jnp.int32, (C2, C2), 0)
    col = lax.broadcasted_iota(jnp.int32, (C2, C2), 1)
    tt = row & (C - 1)
    ss = col & (C - 1)
    top = row < C
    bot = row >= C
    strict = ss < tt
    incl = ss <= tt
    keep0 = jnp.logical_or(jnp.logical_and(top, strict), jnp.logical_and(bot, incl))
    keep1 = jnp.logical_or(jnp.logical_and(top, incl), jnp.logical_and(bot, strict))
    tl = jnp.logical_and(top, col < C)
    br = jnp.logical_and(bot, col >= C)
    eye = (row == col).astype(F32)
    tri = lax.broadcasted_iota(jnp.int32, (C, C), 0) >= lax.broadcasted_iota(jnp.int32, (C, C), 1)
    left = lax.broadcasted_iota(jnp.int32, (C, LANES), 1) < RW_DH
    left2 = lax.broadcasted_iota(jnp.int32, (C2, LANES), 1) < RW_DH
    zeros = jnp.zeros((C, LANES), F32)

    def head_mean(x):
        lsum = jnp.sum(jnp.where(left, x, 0.0), axis=1, keepdims=True)
        rsum = jnp.sum(jnp.where(left, 0.0, x), axis=1, keepdims=True)
        return jnp.where(left, lsum, rsum) * (1.0 / RW_DH)

    ar, m0, m1, vs, bk_end, tots = [], [], [], [], [], []
    for i, j in chains:
        r_i, lw_i, k_i, v_i, kk_i, kka_i, _, _ = tok[i]
        lw = jnp.where(valid, lw_i[:, sl(j)], 0.0)
        cum = _sel_dot(tri, lw)
        tot = cum[C - 1:C, :]
        e_inv = jnp.exp(-cum)
        e_end = jnp.exp(tot - cum)
        kka = jnp.where(valid, kka_i[:, sl(j)], 0.0)
        k = jnp.where(valid, k_i[:, sl(j)], 0.0)
        a_t = -jnp.where(valid, kk_i[:, sl(j)], 0.0) * jnp.exp(cum - lw)
        r_t = r_i[:, sl(j)] * jnp.exp(cum)
        b_t = kka * e_inv
        k_t = k * e_inv
        ar_c = jnp.concatenate([a_t, r_t], axis=0)
        g0 = _dot_nt(jnp.where(left2, ar_c, 0.0), jnp.concatenate([b_t, k_t], axis=0))
        g1 = _dot_nt(jnp.where(left2, 0.0, jnp.concatenate([r_t, a_t], axis=0)), jnp.concatenate([k_t, b_t], axis=0))
        ar.append(ar_c.astype(BF16))
        m0.append(jnp.where(keep0, g0, 0.0))
        m1.append(jnp.where(keep1, g1, 0.0))
        vs.append(jnp.where(valid, v_i[:, sl(j)], 0.0))
        bk_end.append(jnp.concatenate([kka * e_end, k * e_end], axis=0).astype(BF16))
        tots.append(tot)

    s_prev = [s_scr[n] for n in range(nch)]
    w_as = [_dot_nt(ar[n], s_prev[n]) for n in range(nch)]
    x0 = [_dot(m0[n][0:C, :], jnp.concatenate([zeros, vs[n]], axis=0)) for n in range(nch)]
    x1 = [_dot(m1[n][C:, :], jnp.concatenate([vs[n], zeros], axis=0)) for n in range(nch)]

    p = [jnp.where(tl, m0[n], jnp.where(br, m1[n], 0.0)) for n in range(nch)]
    t_inv = [eye + p[n] for n in range(nch)]
    step = 1
    while 2 * step < Lv:
        p = [_dot(p[n], p[n]) for n in range(nch)]
        t_inv = [t_inv[n] + _dot(t_inv[n], p[n]) for n in range(nch)]
        step *= 2

    rhs = [w_as[n][0:C, :] + jnp.where(left, x0[n], x1[n]) for n in range(nch)]
    tu = [_dot(t_inv[n], jnp.concatenate([rhs[n], rhs[n]], axis=0)) for n in range(nch)]
    u = [jnp.where(left, tu[n][0:C, :], tu[n][C:, :]) for n in range(nch)]
    uv = [jnp.concatenate([u[n], vs[n]], axis=0).astype(BF16) for n in range(nch)]
    vu = [jnp.concatenate([vs[n], u[n]], axis=0).astype(BF16) for n in range(nch)]
    o0 = [_dot(m0[n][C:, :], uv[n]) for n in range(nch)]
    o1 = [_dot(m1[n][0:C, :], vu[n]) for n in range(nch)]
    s_new = [_dot_tn(uv[n], bk_end[n]) for n in range(nch)]
    for n in range(nch):
        s_scr[n] = jnp.where(same_head, s_prev[n] * jnp.exp(tots[n]) + s_new[n], 0.0)

    for n, (i, j) in enumerate(chains):
        o = w_as[n][C:, :] + jnp.where(left, o0[n], o1[n])
        oc = o - head_mean(o)
        yn = oc * lax.rsqrt(head_mean(oc * oc) + RW_GN_EPS) * gnw_ref[:, sl(j)] + gnb_ref[:, sl(j)]
        y_ref[i, :, sl(j)] = ((yn + tok[i][7][:, sl(j)]) * tok[i][6][:, sl(j)]).astype(BF16)

    @pl.when(c == nc - 1)
    def _():
        for n, (i, j) in enumerate(chains):
            tile = s_scr[n]
            so_ref[i, 2 * j] = tile[0:RW_DH, 0:RW_DH]
            so_ref[i, 2 * j + 1] = tile[RW_DH:, RW_DH:]
        for i in range(CB):
            shift_ref[i] = last[i]


def _rwkv(zrw, weights, states, layer, *, B, T, C, CB, Lv, row0):
    nc = T // C
    has_state = states is not None
    const2 = lambda b, c: (0, 0)
    wspecs = [pl.BlockSpec(w.shape, const2) for w in weights]
    z_specs = [pl.BlockSpec((C, RW_COLS), functools.partial(lambda b, c, i: (row0 + (b * CB + i) * nc + c, 0), i=i))
               for i in range(CB)]
    in_specs = z_specs + wspecs
    args = [zrw] * CB + list(weights)
    if has_state:
        state, shift = states
        in_specs += [pl.BlockSpec((None, CB, 1, RW_COLS), lambda b, c: (layer, b, 0, 0)),
                     pl.BlockSpec((None, CB, RW_HEADS, RW_DH, RW_DH), lambda b, c: (layer, b, 0, 0, 0))]
        args += [shift.reshape(DEPTH, B, 1, RW_COLS), state]
    y, s, sh = pl.pallas_call(
        functools.partial(_rwkv_chunk_kernel, C=C, CB=CB, Lv=Lv, nc=nc, has_state=has_state),
        grid=(B // CB, nc),
        in_specs=in_specs,
        out_specs=[pl.BlockSpec((CB, C, RW_W), lambda b, c: (b, c, 0)),
                   pl.BlockSpec((CB, RW_HEADS, RW_DH, RW_DH), lambda b, c: (b, 0, 0, 0)),
                   pl.BlockSpec((CB, 1, RW_COLS), lambda b, c: (b, 0, 0))],
        out_shape=[jax.ShapeDtypeStruct((B, T, RW_W), BF16),
                   jax.ShapeDtypeStruct((B, RW_HEADS, RW_DH, RW_DH), F32),
                   jax.ShapeDtypeStruct((B, 1, RW_COLS), F32)],
        scratch_shapes=[pltpu.VMEM((CB * RW_PAIRS, LANES, LANES), F32),
                        pltpu.VMEM((CB, ROW_PAD + C, RW_COLS), F32)],
        compiler_params=_cparams(("parallel", "arbitrary")),
        name="rwkv",
    )(*args)
    return y.reshape(B * T, RW_W), s, sh[:, 0]


def _merge_kernel(x_ref, zg_ref, yrp_ref, ymp_ref, ywp_ref, yrs_ref, yms_ref, yws_ref, wb_ref, wo_ref, o_ref, *, ntp):
    def run(branches):
        acc = None
        for n, y_ref in enumerate(branches):
            proj = jnp.dot(y_ref[...], wb_ref[n], preferred_element_type=F32)
            term = _sigmoid(zg_ref[:, n * D_MODEL:(n + 1) * D_MODEL].astype(F32)) * proj
            acc = term if acc is None else acc + term
        o_ref[...] = x_ref[...] + jnp.dot(acc.astype(BF16), wo_ref[...], preferred_element_type=F32)

    i = pl.program_id(0)

    @pl.when(i < ntp)
    def _():
        run((yrp_ref, ymp_ref, ywp_ref))

    @pl.when(i >= ntp)
    def _():
        run((yrs_ref, yms_ref, yws_ref))


def _merge(x, zgate, y_prompt, y_sample, w_branch, w_out, tm=256):
    n = x.shape[0]
    ntp = y_prompt[0].shape[0] // tm
    tokspec = lambda w: pl.BlockSpec((tm, w), lambda i: (i, 0))
    pspec = pl.BlockSpec((tm, RET_W), lambda i: (jnp.minimum(i, ntp - 1), 0))
    sspec = pl.BlockSpec((tm, RET_W), lambda i: (jnp.maximum(i - ntp, 0), 0))
    return pl.pallas_call(
        functools.partial(_merge_kernel, ntp=ntp),
        grid=(n // tm,),
        in_specs=[tokspec(D_MODEL), tokspec(N_BRANCH * D_MODEL), pspec, pspec, pspec, sspec, sspec, sspec,
                  pl.BlockSpec((N_BRANCH, RET_W, D_MODEL), lambda i: (0, 0, 0)),
                  pl.BlockSpec((D_MODEL, D_MODEL), lambda i: (0, 0))],
        out_specs=tokspec(D_MODEL),
        out_shape=jax.ShapeDtypeStruct((n, D_MODEL), F32),
        compiler_params=_cparams(("arbitrary",)),
        name="merge",
    )(x, zgate, *y_prompt, *y_sample, w_branch, w_out)


def _ffn_kernel(x_ref, g_ref, wg_ref, wu_ref, wd_ref, fin_ref, o_ref, h_scr, acc_scr, *, ne, final):
    e = pl.program_id(1)

    @pl.when(e == 0)
    def _():
        h_scr[...] = _rms(x_ref[...], g_ref[...]).astype(BF16)
        acc_scr[...] = jnp.zeros_like(acc_scr)

    h = h_scr[...]
    hg = jnp.dot(h, wg_ref[0], preferred_element_type=F32)
    hu = jnp.dot(h, wu_ref[0], preferred_element_type=F32)
    acc_scr[...] += jnp.dot((_silu(hg) * hu).astype(BF16), wd_ref[0], preferred_element_type=F32)

    @pl.when(e == ne - 1)
    def _():
        out = x_ref[...] + acc_scr[...]
        if final:
            out = _rms(out, fin_ref[...])
        o_ref[...] = out


def _ffn(x, g, wg, wu, wd, fin, final, tm=512):
    n = x.shape[0]
    ne, _, f = wg.shape
    return pl.pallas_call(
        functools.partial(_ffn_kernel, ne=ne, final=final),
        grid=(n // tm, ne),
        in_specs=[pl.BlockSpec((tm, D_MODEL), lambda i, e: (i, 0)),
                  pl.BlockSpec((1, D_MODEL), lambda i, e: (0, 0)),
                  pl.BlockSpec((1, D_MODEL, f), lambda i, e: (e, 0, 0)),
                  pl.BlockSpec((1, D_MODEL, f), lambda i, e: (e, 0, 0)),
                  pl.BlockSpec((1, f, D_MODEL), lambda i, e: (e, 0, 0)),
                  pl.BlockSpec((1, D_MODEL), lambda i, e: (0, 0))],
        out_specs=pl.BlockSpec((tm, D_MODEL), lambda i, e: (i, 0)),
        out_shape=jax.ShapeDtypeStruct((n, D_MODEL), F32),
        scratch_shapes=[pltpu.VMEM((tm, D_MODEL), BF16), pltpu.VMEM((tm, D_MODEL), F32)],
        compiler_params=_cparams(("parallel", "arbitrary")),
        name="ffn",
    )(x, g, wg, wu, wd, fin)


MOE_TM = 1024
MOE_SUB = LANES
MOE_CAP = 48
MOE_NSUB = MOE_TM // MOE_SUB


def _moe_kernel(x_ref, g_ref, router_ref, wg_ref, wu_ref, wd_ref, fin_ref, o_ref,
                h_scr, acc_scr, comb_scr, combt_scr, xe_scr, *, ne, final):
    e = pl.program_id(1)
    lane = lax.broadcasted_iota(jnp.int32, (MOE_TM, LANES), 1)

    @pl.when(e == 0)
    def _():
        h = _rms(x_ref[...], g_ref[...])
        h_scr[...] = h.astype(BF16)
        acc_scr[...] = jnp.zeros_like(acc_scr)
        logits = jnp.where(lane < ne, _dot_hi(h, router_ref[...]), NEG_BIG)
        m1 = jnp.max(logits, axis=1, keepdims=True)
        i1 = jnp.min(jnp.where(logits == m1, lane, LANES), axis=1, keepdims=True)
        rest = jnp.where(lane == i1, NEG_BIG, logits)
        m2 = jnp.max(rest, axis=1, keepdims=True)
        i2 = jnp.min(jnp.where(rest == m2, lane, LANES), axis=1, keepdims=True)
        e2 = jnp.exp(m2 - m1)
        p1 = 1.0 / (1.0 + e2)
        comb = jnp.where(lane == i1, p1, 0.0) + jnp.where(lane == i2, e2 * p1, 0.0)
        comb_scr[...] = comb
        for s in range(MOE_NSUB):
            combt_scr[s] = comb[s * MOE_SUB:(s + 1) * MOE_SUB, :].T

    sub = lambda s: slice(s * MOE_SUB, (s + 1) * MOE_SUB)
    slot = lambda s: slice(s * MOE_CAP, (s + 1) * MOE_CAP)
    r_i = lax.broadcasted_iota(jnp.int32, (MOE_SUB, MOE_SUB), 0)
    c_i = lax.broadcasted_iota(jnp.int32, (MOE_SUB, MOE_SUB), 1)
    before_row = (c_i < r_i).astype(F32).astype(BF16)
    before_col = (r_i < c_i).astype(F32).astype(BF16)
    slot_row = lax.broadcasted_iota(jnp.int32, (MOE_CAP, MOE_SUB), 0).astype(F32)
    slot_col = lax.broadcasted_iota(jnp.int32, (MOE_SUB, MOE_CAP), 1).astype(F32)

    w_col = jnp.sum(jnp.where(lane == e, comb_scr[...], 0.0), axis=1, keepdims=True)
    hit_col = [(w_col[sub(s), :] > 0.0).astype(F32) for s in range(MOE_NSUB)]
    hit_row = [(combt_scr[s, pl.ds(e, 1), :] > 0.0).astype(F32) for s in range(MOE_NSUB)]
    rank_col = [jnp.dot(before_row, jnp.broadcast_to(hit_col[s], (MOE_SUB, MOE_CAP)).astype(BF16),
                        preferred_element_type=F32) for s in range(MOE_NSUB)]
    rank_row = [jnp.dot(jnp.broadcast_to(hit_row[s], (SUBLANES, MOE_SUB)).astype(BF16), before_col,
                        preferred_element_type=F32)[0:1, :] for s in range(MOE_NSUB)]
    count = jnp.sum(hit_row[0], axis=1, keepdims=True)
    for s in range(1, MOE_NSUB):
        count = jnp.maximum(count, jnp.sum(hit_row[s], axis=1, keepdims=True))
    n_pass = lax.div(jnp.max(count).astype(jnp.int32) + (MOE_CAP - 1), MOE_CAP)

    def one_pass(p, carry):
        base = (p * MOE_CAP).astype(F32)
        for s in range(MOE_NSUB):
            pick = jnp.logical_and(rank_row[s] - base == slot_row, hit_row[s] > 0.0).astype(F32).astype(BF16)
            xe_scr[slot(s), :] = jnp.dot(pick, h_scr[sub(s), :], preferred_element_type=F32).astype(BF16)
        xe = xe_scr[...]
        hg = jnp.dot(xe, wg_ref[0], preferred_element_type=F32)
        hu = jnp.dot(xe, wu_ref[0], preferred_element_type=F32)
        y = jnp.dot((_silu(hg) * hu).astype(BF16), wd_ref[0], preferred_element_type=F32)
        for s in range(MOE_NSUB):
            put = jnp.logical_and(rank_col[s] - base == slot_col, hit_col[s] > 0.0).astype(F32).astype(BF16)
            hi, lo = _split(y[slot(s), :])
            back = jnp.dot(put, hi, preferred_element_type=F32) + jnp.dot(put, lo, preferred_element_type=F32)
            acc_scr[sub(s), :] += w_col[sub(s), :] * back
        return carry

    lax.fori_loop(0, n_pass, one_pass, 0)

    @pl.when(e == ne - 1)
    def _():
        out = x_ref[...] + acc_scr[...]
        if final:
            out = _rms(out, fin_ref[...])
        o_ref[...] = out


def _moe(x, g, wg, wu, wd, router, fin, final):
    n = x.shape[0]
    ne, _, f = wg.shape
    return pl.pallas_call(
        functools.partial(_moe_kernel, ne=ne, final=final),
        grid=(n // MOE_TM, ne),
        in_specs=[pl.BlockSpec((MOE_TM, D_MODEL), lambda i, e: (i, 0)),
                  pl.BlockSpec((1, D_MODEL), lambda i, e: (0, 0)),
                  pl.BlockSpec((D_MODEL, LANES), lambda i, e: (0, 0)),
                  pl.BlockSpec((1, D_MODEL, f), lambda i, e: (e, 0, 0)),
                  pl.BlockSpec((1, D_MODEL, f), lambda i, e: (e, 0, 0)),
                  pl.BlockSpec((1, f, D_MODEL), lambda i, e: (e, 0, 0)),
                  pl.BlockSpec((1, D_MODEL), lambda i, e: (0, 0))],
        out_specs=pl.BlockSpec((MOE_TM, D_MODEL), lambda i, e: (i, 0)),
        out_shape=jax.ShapeDtypeStruct((n, D_MODEL), F32),
        scratch_shapes=[pltpu.VMEM((MOE_TM, D_MODEL), BF16), pltpu.VMEM((MOE_TM, D_MODEL), F32),
                        pltpu.VMEM((MOE_TM, LANES), F32), pltpu.VMEM((MOE_NSUB, LANES, MOE_SUB), F32),
                        pltpu.VMEM((MOE_NSUB * MOE_CAP, D_MODEL), BF16)],
        compiler_params=_cparams(("parallel", "arbitrary")),
        name="moe",
    )(x, g, router, wg, wu, wd, fin)


def _row(v):
    return v.reshape(1, -1)


def kernel(x_prompt, x_sample, state_ret, state_mlstm_C, state_mlstm_n, state_mlstm_m, state_mlstm_conv, state_rwkv, state_rwkv_shift, norm_mix, w_in, ret_gn, ml_conv_w, ml_conv_b, ml_wq, ml_wk, ml_bi, ml_bf, ml_gn, ml_skip, rw_mu, rw_w0, rw_w2, rw_a0, rw_a2, rw_g2, rw_kk, rw_ka, rw_rk, rw_gn_w, rw_gn_b, w_branch, w_out, norm_ffn, ffn_w_gate, ffn_w_up, ffn_w_down, moe_router, moe_w_gate, moe_w_up, moe_w_down, final_norm):
    nc_p = SEQ // CHUNK
    xs = jnp.pad(x_sample, ((0, 0), (0, S_PAD - DEC_SEQ), (0, 0)))
    x = jnp.concatenate([x_prompt.reshape(N_PROMPT, D_MODEL), xs.reshape(N_SAMPLE, D_MODEL)], axis=0)

    pos_p = jnp.arange(SEQ, dtype=jnp.int32)
    pos_s = PAST_LEN + jnp.arange(S_PAD, dtype=jnp.int32)
    ret_tab_p = _retention_tables(CHUNK, CHUNK, pos_p, 1)
    ret_tab_s = _retention_tables(S_PAD, DEC_SEQ, pos_s, RET_NB_SAMPLE)
    head_of = jnp.arange(RW_W) // RW_DH
    bd64 = (head_of[:, None] == head_of[None, :]).astype(BF16)
    m0_all = jnp.pad(state_mlstm_m, ((0, 0), (0, 0), (0, LANES - ML_HEADS))).reshape(DEPTH, DEC_BATCH, 1, LANES)

    o_rq = 0
    o_mx = o_rq + RET_COLS
    o_mi = o_mx + 2 * ML_W
    o_mo = o_mi + 2 * ML_HEADS
    o_rw = o_mo + ML_W
    o_gate = o_rw + RW_COLS

    new_p, new_s = [], []
    for l in range(DEPTH):
        w = w_in[l]
        w_ret = w[:, o_rq:o_mx].astype(BF16)
        w_ml = jnp.concatenate([w[:, o_mx:o_mi], w[:, o_mo:o_rw], w[:, o_mi:o_mo],
                                jnp.zeros((D_MODEL, ML_GATE_W - 2 * ML_HEADS), F32)], axis=1).astype(BF16)
        w_rw = w[:, o_rw:o_gate].astype(BF16)
        w_gate = w[:, o_gate:].astype(BF16)
        g_mix = _row(norm_mix[l])
        zret, = _norm_matmul(x, g_mix, w_ret, [(RET_COLS, BF16)])
        zml, zml_gates = _norm_matmul(x, g_mix, w_ml, [(3 * ML_W, BF16), (ML_GATE_W, F32)])
        zrw, = _norm_matmul(x, g_mix, w_rw, [(RW_COLS, BF16)])
        zgate, = _norm_matmul(x, g_mix, w_gate, [(N_BRANCH * D_MODEL, BF16)])

        gn = _row(ret_gn[l])
        yr_p, ret_p = _retention(zret, ret_tab_p, gn, None, l, B=BATCH, nc=nc_p, L=CHUNK, nb=1, row0=0)
        yr_s, ret_s = _retention(zret, ret_tab_s, gn, state_ret, l, B=DEC_BATCH, nc=1, L=S_PAD, nb=RET_NB_SAMPLE,
                                 row0=N_PROMPT // (RET_NB_SAMPLE * S_PAD))

        gate_bias = jnp.concatenate([ml_bi[l], ml_bf[l], jnp.zeros((ML_GATE_W - 2 * ML_HEADS,), F32)])
        ml_weights = (ml_conv_w[l], _row(ml_conv_b[l]), ml_wq[l].astype(BF16), ml_wk[l].astype(BF16),
                      _row(gate_bias), _row(ml_gn[l]), _row(ml_skip[l]))
        ym_p, c_p, n_p, m_p, buf_p = _mlstm(zml, zml_gates, ml_weights, None, l, B=BATCH, nc=nc_p, L=CHUNK, Lv=CHUNK,
                                            nb=1, row0=0)
        ym_s, c_s, n_s, m_s, buf_s = _mlstm(zml, zml_gates, ml_weights,
                                            (state_mlstm_C, state_mlstm_n, m0_all, state_mlstm_conv),
                                            l, B=DEC_BATCH, nc=1, L=S_PAD, Lv=DEC_SEQ, nb=ML_NB_SAMPLE,
                                            row0=N_PROMPT // (ML_NB_SAMPLE * S_PAD))

        w2a = jnp.zeros((RW_LORA_W + RW_LORA_A, 2 * RW_W), F32)
        w2a = w2a.at[:RW_LORA_W, :RW_W].set(rw_w2[l]).at[RW_LORA_W:, RW_W:].set(rw_a2[l])
        rw_weights = (_row(rw_mu[l]), _row(jnp.concatenate([rw_w0[l], rw_a0[l]])), w2a.astype(BF16),
                      rw_g2[l].astype(BF16), _row(rw_kk[l]), _row(rw_ka[l]), _row(rw_rk[l]), bd64,
                      _row(rw_gn_w[l]), _row(rw_gn_b[l]))
        yw_p, rws_p, shift_p = _rwkv(zrw, rw_weights, None, l, B=BATCH, T=SEQ, C=RW_C, CB=RW_CB_PROMPT, Lv=RW_C,
                                     row0=0)
        yw_s, rws_s, shift_s = _rwkv(zrw, rw_weights, (state_rwkv, state_rwkv_shift), l, B=DEC_BATCH, T=S_PAD,
                                     C=S_PAD, CB=RW_CB_SAMPLE, Lv=DEC_SEQ, row0=N_PROMPT // S_PAD)

        x = _merge(x, zgate, (yr_p, ym_p, yw_p), (yr_s, ym_s, yw_s), w_branch[l].astype(BF16), w_out[l].astype(BF16))

        i = l // 2
        g_ffn = _row(norm_ffn[l])
        fin = _row(final_norm)
        final = l == DEPTH - 1
        if l % 2 == 0:
            halves = D_FF // D_FF_EXPERT
            wg = ffn_w_gate[i].reshape(D_MODEL, halves, D_FF_EXPERT).transpose(1, 0, 2).astype(BF16)
            wu = ffn_w_up[i].reshape(D_MODEL, halves, D_FF_EXPERT).transpose(1, 0, 2).astype(BF16)
            wd = ffn_w_down[i].reshape(halves, D_FF_EXPERT, D_MODEL).astype(BF16)
            x = _ffn(x, g_ffn, wg, wu, wd, fin, final)
        else:
            router = jnp.pad(moe_router[i], ((0, 0), (0, LANES - N_EXPERTS)))
            x = _moe(x, g_ffn, moe_w_gate[i].astype(BF16), moe_w_up[i].astype(BF16), moe_w_down[i].astype(BF16),
                     router, fin, final)

        new_p.append((ret_p, c_p, n_p, m_p[:, 0, :ML_HEADS], buf_p, rws_p, shift_p))
        new_s.append((ret_s, c_s, n_s, m_s[:, 0, :ML_HEADS], buf_s, rws_s, shift_s))

    y_prompt = x[:N_PROMPT].reshape(BATCH, SEQ, D_MODEL)
    y_sample = x[N_PROMPT:].reshape(DEC_BATCH, S_PAD, D_MODEL)[:, :DEC_SEQ]
    st_p = tuple(jnp.stack([st[j] for st in new_p]) for j in range(7))
    st_s = tuple(jnp.stack([st[j] for st in new_s]) for j in range(7))
    return (y_prompt, y_sample) + st_p + st_s
```

```python
import functools

import jax
import jax.numpy as jnp
from jax import lax
from jax.experimental import pallas as pl
from jax.experimental.pallas import tpu as pltpu

D_MODEL = 1024
BATCH = 8
SEQ = 2048
DEPTH = 2
DEC_BATCH = 128
DEC_SEQ = 4
PAST_LEN = 16384
RET_HEADS = 4
RET_DK = 64
RET_DV = 128
ML_HEADS = 4
ML_DH = 128
CONV_W = 4
RW_HEADS = 8
RW_DH = 64
RW_LORA_W = 64
RW_LORA_A = 64
RW_LORA_G = 128
RET_W = RET_HEADS * RET_DV
ML_W = ML_HEADS * ML_DH
RW_W = RW_HEADS * RW_DH
N_BRANCH = 3
RW_COLS = 3 * RW_W + RW_LORA_W + RW_LORA_A + RW_LORA_G
D_FF = 2816
N_EXPERTS = 8
D_FF_EXPERT = 1408
CHUNK = 128
NORM_EPS = 1e-6
GN_EPS = 1e-5
RW_GN_EPS = 64e-5
ROPE_BASE = 10000.0
RW_DECAY_SCALE = 0.606531

LANES = 128
SUBLANES = 8
S_PAD = 16
N_PROMPT = BATCH * SEQ
N_SAMPLE = DEC_BATCH * S_PAD
N_TOK = N_PROMPT + N_SAMPLE
ML_GATE_W = LANES
ML_COLS = 3 * ML_W + ML_GATE_W
RET_COLS = 2 * RET_HEADS * RET_DK + 2 * RET_W
NEG_BIG = -1e30
VMEM_LIMIT = 56 * 1024 * 1024

RW_C = 64
RW_PAIRS = RW_HEADS // 2
RW_CB_PROMPT = 4
RW_CB_SAMPLE = 4
RET_NB_PROMPT = 4
RET_NB_SAMPLE = 8
ML_NB_PROMPT = 1
ML_NB_SAMPLE = 4

F32 = jnp.float32
BF16 = jnp.bfloat16


def _cparams(sem):
    return pltpu.CompilerParams(dimension_semantics=sem, vmem_limit_bytes=VMEM_LIMIT)


def _sigmoid(x):
    return 1.0 / (1.0 + jnp.exp(-x))


def _silu(x):
    return x * _sigmoid(x)


def _rms(x, g):
    return x * lax.rsqrt(jnp.mean(x * x, axis=-1, keepdims=True) + NORM_EPS) * g


def _dot(a, b):
    return jnp.dot(a.astype(BF16), b.astype(BF16), preferred_element_type=F32)


def _dot_nt(a, b):
    return lax.dot_general(a.astype(BF16), b.astype(BF16), (((1,), (1,)), ((), ())), preferred_element_type=F32)


def _dot_tn(a, b):
    return lax.dot_general(a.astype(BF16), b.astype(BF16), (((0,), (0,)), ((), ())), preferred_element_type=F32)


def _split(x):
    hi = x.astype(BF16)
    return hi, (x - hi.astype(F32)).astype(BF16)


def _as_bf16_mask(m):
    return m if m.dtype == BF16 else m.astype(F32).astype(BF16)


def _sel_dot(m, x):
    hi, lo = _split(x)
    m = _as_bf16_mask(m)
    return jnp.dot(m, hi, preferred_element_type=F32) + jnp.dot(m, lo, preferred_element_type=F32)


def _dot_sel(x, m):
    hi, lo = _split(x)
    m = _as_bf16_mask(m)
    return jnp.dot(hi, m, preferred_element_type=F32) + jnp.dot(lo, m, preferred_element_type=F32)


def _sel_dot_nt(m, x):
    hi, lo = _split(x)
    m = _as_bf16_mask(m)
    dn = (((1,), (1,)), ((), ()))
    return (lax.dot_general(m, hi, dn, preferred_element_type=F32)
            + lax.dot_general(m, lo, dn, preferred_element_type=F32))


def _norm_matmul_kernel(x_ref, g_ref, w_ref, *o_refs, bounds):
    h = _rms(x_ref[...], g_ref[...])
    z = jnp.dot(h.astype(BF16), w_ref[...], preferred_element_type=F32)
    for o_ref, (lo, hi) in zip(o_refs, bounds):
        o_ref[...] = z[:, lo:hi].astype(o_ref.dtype)


def _norm_matmul(x, g, w, outs, tm=512):
    n, d = x.shape
    c = w.shape[1]
    bounds, lo = [], 0
    for width, _ in outs:
        bounds.append((lo, lo + width))
        lo += width
    assert lo == c
    return pl.pallas_call(
        functools.partial(_norm_matmul_kernel, bounds=tuple(bounds)),
        grid=(n // tm,),
        in_specs=[pl.BlockSpec((tm, d), lambda i: (i, 0)),
                  pl.BlockSpec((1, d), lambda i: (0, 0)),
                  pl.BlockSpec((d, c), lambda i: (0, 0))],
        out_specs=[pl.BlockSpec((tm, width), lambda i: (i, 0)) for width, _ in outs],
        out_shape=[jax.ShapeDtypeStruct((n, width), dt) for width, dt in outs],
        compiler_params=_cparams(("parallel",)),
        name="norm_matmul",
    )(x, g, w)


def _seq_views(refs, nb, L, split):
    if split:
        return list(refs)
    return [refs[0].at[pl.ds(i * L, L)] for i in range(nb)]


def _retention_kernel(*refs, L, nc, nb, split, has_state):
    nv = nb if split else 1
    q_v, k_v, v_v, g_v = (_seq_views(refs[j * nv:(j + 1) * nv], nb, L, split) for j in range(4))
    refs = refs[4 * nv:]
    if has_state:
        cos_ref, sin_ref, di_ref, dq_ref, dk_ref, dc_ref, gn_ref, s0_ref, y_ref, so_ref, s_scr = refs
    else:
        cos_ref, sin_ref, di_ref, dq_ref, dk_ref, dc_ref, gn_ref, y_ref, so_ref, s_scr = refs
    y_v = [y_ref.at[i] for i in range(nb)]
    c = pl.program_id(1)

    @pl.when(c == 0)
    def _():
        if has_state:
            s_scr[...] = s0_ref[...]
        else:
            s_scr[...] = jnp.zeros_like(s_scr)

    qk_w = RET_HEADS * RET_DK
    half = RET_DK // 2
    lane = lax.broadcasted_iota(jnp.int32, (L, qk_w), 1)
    first_half = (lane % RET_DK) < half
    cos = cos_ref[...]
    sin = sin_ref[...]

    def rot(x):
        swapped = jnp.where(first_half, pltpu.roll(x, qk_w - half, 1), pltpu.roll(x, half, 1))
        return x * cos + swapped * sin

    q = [rot(q_v[i][...].astype(F32)) for i in range(nb)]
    k = [rot(k_v[i][...].astype(F32)) * (RET_DK ** -0.5) for i in range(nb)]
    dq = dq_ref[...]
    dk = dk_ref[...]
    dc = dc_ref[...]
    chains = [(i, h) for i in range(nb) for h in range(RET_HEADS)]
    kcols = lambda h: slice(h * RET_DK, (h + 1) * RET_DK)
    vcols = lambda h: slice(h * RET_DV, (h + 1) * RET_DV)

    qh = [q[i][:, kcols(h)].astype(BF16) for i, h in chains]
    kh = [k[i][:, kcols(h)] for i, h in chains]
    vh = [v_v[i][:, vcols(h)].astype(BF16) for i, h in chains]
    s = [_dot_nt(qh[n], kh[n]) * di_ref[h] for n, (i, h) in enumerate(chains)]
    s_prev = [s_scr[i, h] for i, h in chains]
    cross = [_dot(qh[n], s_prev[n]) for n in range(len(chains))]
    upd = [_dot_tn(kh[n] * dk[:, h:h + 1], vh[n]) for n, (i, h) in enumerate(chains)]
    inner = [_dot(s[n], vh[n]) for n in range(len(chains))]
    for n, (i, h) in enumerate(chains):
        s_scr[i, h] = s_prev[n] * dc[:, h:h + 1] + upd[n]
        o = inner[n] + cross[n] * dq[:, h:h + 1]
        oc = o - jnp.mean(o, axis=-1, keepdims=True)
        yn = oc * lax.rsqrt(jnp.mean(oc * oc, axis=-1, keepdims=True) + GN_EPS) * gn_ref[:, vcols(h)]
        y_v[i][:, vcols(h)] = (_silu(g_v[i][:, vcols(h)].astype(F32)) * yn).astype(BF16)

    @pl.when(c == nc - 1)
    def _():
        so_ref[...] = s_scr[...]


def _token_specs(cols, col_block, *, L, nb, nc, row0, split):
    if split:
        return [pl.BlockSpec((L, cols), functools.partial(
            lambda b, c, i: (row0 + (b * nb + i) * nc + c, col_block), i=i)) for i in range(nb)]
    assert nc == 1
    return [pl.BlockSpec((nb * L, cols), lambda b, c: (row0 + b, col_block))]


def _retention(zret, tables, gn, state, layer, *, B, nc, L, nb, split, row0):
    cos, sin, di, dq, dk, dc = tables
    has_state = state is not None
    qk_w = RET_HEADS * RET_DK
    nv = nb if split else 1
    tok = functools.partial(_token_specs, L=L, nb=nb, nc=nc, row0=row0, split=split)
    st = pl.BlockSpec((nb, RET_HEADS, RET_DK, RET_DV), lambda b, c: (b, 0, 0, 0))
    in_specs = (tok(qk_w, 0) + tok(qk_w, 1) + tok(RET_W, 1) + tok(RET_W, 2)
                + [pl.BlockSpec((L, qk_w), lambda b, c: (c, 0)),
                   pl.BlockSpec((L, qk_w), lambda b, c: (c, 0)),
                   pl.BlockSpec((RET_HEADS, L, L), lambda b, c: (0, 0, 0)),
                   pl.BlockSpec((L, RET_HEADS), lambda b, c: (0, 0)),
                   pl.BlockSpec((L, RET_HEADS), lambda b, c: (0, 0)),
                   pl.BlockSpec((1, RET_HEADS), lambda b, c: (0, 0)),
                   pl.BlockSpec((1, RET_W), lambda b, c: (0, 0))])
    args = [zret] * (4 * nv) + [cos, sin, di, dq, dk, dc, gn]
    if has_state:
        in_specs.append(pl.BlockSpec((None, nb, RET_HEADS, RET_DK, RET_DV), lambda b, c: (layer, b, 0, 0, 0)))
        args.append(state)
    y, s = pl.pallas_call(
        functools.partial(_retention_kernel, L=L, nc=nc, nb=nb, split=split, has_state=has_state),
        grid=(B // nb, nc),
        in_specs=in_specs,
        out_specs=[pl.BlockSpec((nb, L, RET_W), lambda b, c: (b, c, 0)), st],
        out_shape=[jax.ShapeDtypeStruct((B, nc * L, RET_W), BF16),
                   jax.ShapeDtypeStruct((B, RET_HEADS, RET_DK, RET_DV), F32)],
        scratch_shapes=[pltpu.VMEM((nb, RET_HEADS, RET_DK, RET_DV), F32)],
        compiler_params=_cparams(("parallel", "arbitrary")),
        name="retention",
    )(*args)
    return y.reshape(B * nc * L, RET_W), s


def _retention_tables(L, Lv, pos):
    half = RET_DK // 2
    inv = ROPE_BASE ** (-jnp.arange(half, dtype=F32) / half)
    ang = pos.astype(F32)[:, None] * inv[None, :]
    cos = jnp.tile(jnp.concatenate([jnp.cos(ang), jnp.cos(ang)], axis=1), (1, RET_HEADS))
    sin = jnp.tile(jnp.concatenate([-jnp.sin(ang), jnp.sin(ang)], axis=1), (1, RET_HEADS))
    log_gamma = jnp.log1p(-(2.0 ** (-5.0 - jnp.arange(RET_HEADS, dtype=F32))))
    idx = jnp.arange(L, dtype=F32)
    diff = idx[:, None] - idx[None, :]
    causal = diff >= 0
    di = jnp.where(causal[None], jnp.exp(jnp.where(causal, diff, 0.0)[None] * log_gamma[:, None, None]), 0.0)
    dq = jnp.exp((idx[:, None] + 1.0) * log_gamma[None, :])
    dk = jnp.where((idx < Lv)[:, None], jnp.exp((Lv - 1.0 - idx)[:, None] * log_gamma[None, :]), 0.0)
    dc = jnp.exp(Lv * log_gamma)[None, :]
    return cos, sin, di, dq, dk, dc


ROW_PAD = SUBLANES


def _mlstm_kernel(*refs, L, Lv, nc, nb, split, has_state):
    nv = nb if split else 1
    mx_v, mv_v, mo_v, gz_v = (_seq_views(refs[j * nv:(j + 1) * nv], nb, L, split) for j in range(4))
    refs = refs[4 * nv:]
    if has_state:
        (cw_ref, cb_ref, wq_ref, wk_ref, gb_ref, gn_ref, skip_ref, c0_ref, n0_ref, m0_ref, buf0_ref,
         y_ref, co_ref, no_ref, mout_ref, bufo_ref, xp_scr, c_scr, n_scr, m_scr) = refs
    else:
        (cw_ref, cb_ref, wq_ref, wk_ref, gb_ref, gn_ref, skip_ref,
         y_ref, co_ref, no_ref, mout_ref, bufo_ref, xp_scr, c_scr, n_scr, m_scr) = refs
    y_v = [y_ref.at[i] for i in range(nb)]
    c = pl.program_id(1)
    tail = CONV_W - 1

    @pl.when(c == 0)
    def _():
        for i in range(nb):
            xp_scr[i, 0:ROW_PAD, :] = jnp.zeros((ROW_PAD, ML_W), F32)
        if has_state:
            c_scr[...] = c0_ref[...]
            n_scr[...] = n0_ref[...]
            m_scr[...] = m0_ref[...]
            for i in range(nb):
                xp_scr[i, ROW_PAD - tail:ROW_PAD, :] = buf0_ref[i]
        else:
            c_scr[...] = jnp.zeros_like(c_scr)
            n_scr[...] = jnp.zeros_like(n_scr)
            m_scr[...] = jnp.zeros_like(m_scr)

    cols = lambda h: slice(h * ML_DH, (h + 1) * ML_DH)
    lane = lax.broadcasted_iota(jnp.int32, (L, ML_GATE_W), 1)
    rowi = lax.broadcasted_iota(jnp.int32, (L, ML_GATE_W), 0)
    tri = lax.broadcasted_iota(jnp.int32, (L, L), 0) >= lax.broadcasted_iota(jnp.int32, (L, L), 1)
    eye = (lax.broadcasted_iota(jnp.int32, (SUBLANES, ML_GATE_W), 0)
           == lax.broadcasted_iota(jnp.int32, (SUBLANES, ML_GATE_W), 1))

    xc, new_tail, gc, fc, g_rows, f_rows = [], [], [], [], [], []
    for i in range(nb):
        xp_scr[i, ROW_PAD:ROW_PAD + L, :] = mx_v[i][...].astype(F32)
        acc = cb_ref[...]
        for j in range(CONV_W):
            acc = acc + cw_ref[j:j + 1, :] * xp_scr[i, ROW_PAD - tail + j:ROW_PAD - tail + j + L, :]
        xc.append(_silu(acc))
        new_tail.append(xp_scr[i, ROW_PAD + Lv - tail:ROW_PAD + Lv, :])
        xp_scr[i, ROW_PAD - tail:ROW_PAD, :] = new_tail[i]
        gz = gz_v[i][...] + gb_ref[...]
        logsig = jnp.minimum(gz, 0.0) - jnp.log1p(jnp.exp(-jnp.abs(gz)))
        g = jnp.where(lane < ML_HEADS, gz, logsig)
        if Lv < L:
            g = jnp.where(rowi < Lv, g, jnp.where(lane < ML_HEADS, NEG_BIG, 0.0))
        gc.append(g)
        fc.append(_sel_dot(tri, g))
        g_rows.append(_sel_dot_nt(eye, g))
        f_rows.append(_sel_dot_nt(eye, fc[i]))

    chains = [(i, h) for i in range(nb) for h in range(ML_HEADS)]
    nch = len(chains)
    xh = [xc[i][:, cols(h)] for i, h in chains]
    xh_b = [x.astype(BF16) for x in xh]
    q = [jnp.dot(xh_b[n], wq_ref[h], preferred_element_type=F32) for n, (i, h) in enumerate(chains)]
    k = [jnp.dot(xh_b[n], wk_ref[h], preferred_element_type=F32) * (ML_DH ** -0.5) for n, (i, h) in enumerate(chains)]
    q_b = [x.astype(BF16) for x in q]
    v_b = [mv_v[i][:, cols(h)].astype(BF16) for i, h in chains]
    qk = [_dot_nt(q_b[n], k[n]) for n in range(nch)]
    c_prev = [c_scr[i, h] for i, h in chains]
    qc = [_dot(q_b[n], c_prev[n]) for n in range(nch)]

    s, a_inter, m_t, kw, a_old, m_new = [], [], [], [], [], []
    for n, (i, h) in enumerate(chains):
        f_col = fc[i][:, ML_HEADS + h:ML_HEADS + h + 1]
        i_col = gc[i][:, h:h + 1]
        f_row = f_rows[i][ML_HEADS + h:ML_HEADS + h + 1, :]
        i_row = g_rows[i][h:h + 1, :]
        m_prev = m_scr[i, 0:1, h:h + 1]
        dlog = jnp.where(tri, f_col - f_row + i_row, NEG_BIG)
        inter = f_col + m_prev
        m_t.append(jnp.maximum(jnp.max(dlog, axis=1, keepdims=True), inter))
        s.append(qk[n] * jnp.exp(dlog - m_t[n]))
        a_inter.append(jnp.exp(inter - m_t[n]))
        f_last = fc[i][Lv - 1:Lv, ML_HEADS + h:ML_HEADS + h + 1]
        m_new.append(jnp.maximum(f_last + m_prev, jnp.max(f_last - f_row + i_row, axis=1, keepdims=True)))
        kw.append(k[n] * jnp.exp(f_last - f_col + i_col - m_new[n]))
        a_old.append(jnp.exp(f_last + m_prev - m_new[n]))

    sv = [_dot(s[n], v_b[n]) for n in range(nch)]
    ktv = [_dot_tn(kw[n], v_b[n]) for n in range(nch)]
    for n, (i, h) in enumerate(chains):
        n_prev = n_scr[i, h:h + 1, :]
        num = sv[n] + a_inter[n] * qc[n]
        den = jnp.sum(s[n], axis=1, keepdims=True) + a_inter[n] * jnp.sum(q[n] * n_prev, axis=1, keepdims=True)
        hh = num * (1.0 / jnp.maximum(jnp.abs(den), jnp.exp(-m_t[n])))
        c_scr[i, h] = a_old[n] * c_prev[n] + ktv[n]
        n_scr[i, h:h + 1, :] = a_old[n] * n_prev + jnp.sum(kw[n], axis=0, keepdims=True)
        m_scr[i, 0:1, h:h + 1] = m_new[n]
        hc = hh - jnp.mean(hh, axis=-1, keepdims=True)
        hn = hc * lax.rsqrt(jnp.mean(hc * hc, axis=-1, keepdims=True) + GN_EPS) * gn_ref[:, cols(h)]
        gate = _sigmoid(mo_v[i][:, cols(h)].astype(F32))
        y_v[i][:, cols(h)] = (gate * (hn + skip_ref[:, cols(h)] * xh[n])).astype(BF16)

    @pl.when(c == nc - 1)
    def _():
        co_ref[...] = c_scr[...]
        no_ref[...] = n_scr[...]
        mout_ref[...] = m_scr[...]
        for i in range(nb):
            bufo_ref[i] = new_tail[i]


def _mlstm(zml, zml_gates, weights, state, layer, *, B, nc, L, Lv, nb, split, row0):
    cw, cb, wq, wk, gb, gn, skip = weights
    has_state = state is not None
    nv = nb if split else 1
    const2 = lambda b, c: (0, 0)
    tok = functools.partial(_token_specs, L=L, nb=nb, nc=nc, row0=row0, split=split)
    in_specs = (tok(ML_W, 0) + tok(ML_W, 1) + tok(ML_W, 2) + tok(ML_GATE_W, 0)
                + [pl.BlockSpec((CONV_W, ML_W), const2),
                   pl.BlockSpec((1, ML_W), const2),
                   pl.BlockSpec((ML_HEADS, ML_DH, ML_DH), lambda b, c: (0, 0, 0)),
                   pl.BlockSpec((ML_HEADS, ML_DH, ML_DH), lambda b, c: (0, 0, 0)),
                   pl.BlockSpec((1, ML_GATE_W), const2),
                   pl.BlockSpec((1, ML_W), const2),
                   pl.BlockSpec((1, ML_W), const2)])
    args = [zml] * (3 * nv) + [zml_gates] * nv + [cw, cb, wq, wk, gb, gn, skip]
    st_specs = [pl.BlockSpec((nb, ML_HEADS, ML_DH, ML_DH), lambda b, c: (b, 0, 0, 0)),
                pl.BlockSpec((nb, ML_HEADS, ML_DH), lambda b, c: (b, 0, 0)),
                pl.BlockSpec((nb, 1, LANES), lambda b, c: (b, 0, 0)),
                pl.BlockSpec((nb, CONV_W - 1, ML_W), lambda b, c: (b, 0, 0))]
    if has_state:
        in_specs += [pl.BlockSpec((None, nb, ML_HEADS, ML_DH, ML_DH), lambda b, c: (layer, b, 0, 0, 0)),
                     pl.BlockSpec((None, nb, ML_HEADS, ML_DH), lambda b, c: (layer, b, 0, 0)),
                     pl.BlockSpec((None, nb, 1, LANES), lambda b, c: (layer, b, 0, 0)),
                     pl.BlockSpec((None, nb, CONV_W - 1, ML_W), lambda b, c: (layer, b, 0, 0))]
        args += list(state)
    y, *new_state = pl.pallas_call(
        functools.partial(_mlstm_kernel, L=L, Lv=Lv, nc=nc, nb=nb, split=split, has_state=has_state),
        grid=(B // nb, nc),
        in_specs=in_specs,
        out_specs=[pl.BlockSpec((nb, L, ML_W), lambda b, c: (b, c, 0))] + st_specs,
        out_shape=[jax.ShapeDtypeStruct((B, nc * L, ML_W), BF16),
                   jax.ShapeDtypeStruct((B, ML_HEADS, ML_DH, ML_DH), F32),
                   jax.ShapeDtypeStruct((B, ML_HEADS, ML_DH), F32),
                   jax.ShapeDtypeStruct((B, 1, LANES), F32),
                   jax.ShapeDtypeStruct((B, CONV_W - 1, ML_W), F32)],
        scratch_shapes=[pltpu.VMEM((nb, ROW_PAD + L, ML_W), F32),
                        pltpu.VMEM((nb, ML_HEADS, ML_DH, ML_DH), F32),
                        pltpu.VMEM((nb, ML_HEADS, ML_DH), F32),
                        pltpu.VMEM((nb, 1, LANES), F32)],
        compiler_params=_cparams(("parallel", "arbitrary")),
        name="mlstm",
    )(*args)
    return (y.reshape(B * nc * L, ML_W), *new_state)


def _rwkv_token_vectors(zs, wa0, w2a, g2, kkp, ka, rk, bd):
    r = zs[:, 0:RW_W]
    kr = zs[:, RW_W:2 * RW_W]
    vr = zs[:, 2 * RW_W:3 * RW_W]
    wa = zs[:, 3 * RW_W:3 * RW_W + RW_LORA_W + RW_LORA_A]
    gl = zs[:, 3 * RW_W + RW_LORA_W + RW_LORA_A:]
    lane = lax.broadcasted_iota(jnp.int32, wa.shape, 1)
    lora = jnp.dot(jnp.where(lane < RW_LORA_W, jnp.tanh(wa), wa).astype(BF16), w2a, preferred_element_type=F32) + wa0
    g = jnp.dot(_sigmoid(gl).astype(BF16), g2, preferred_element_type=F32)
    kk = kr * kkp
    kk = kk / jnp.maximum(jnp.sqrt(_dot_sel(kk * kk, bd)), 1e-12)
    a = _sigmoid(lora[:, RW_W:])
    k2 = kr * (1.0 + (a - 1.0) * ka)
    bonus = _dot_sel(r * k2 * rk, bd) * vr
    return r, -RW_DECAY_SCALE * _sigmoid(lora[:, 0:RW_W]), k2, vr, kk, kk * a, g, bonus


def _rwkv_chunk_kernel(*refs, C, CB, Lv, nc, has_state):
    z_refs = refs[:CB]
    refs = refs[CB:]
    if has_state:
        (mu_ref, wa0_ref, w2a_ref, g2_ref, kkp_ref, ka_ref, rk_ref, bd_ref, gnw_ref, gnb_ref, prev_ref, s0_ref,
         y_ref, so_ref, shift_ref, s_scr, xs_scr) = refs
    else:
        (mu_ref, wa0_ref, w2a_ref, g2_ref, kkp_ref, ka_ref, rk_ref, bd_ref, gnw_ref, gnb_ref,
         y_ref, so_ref, shift_ref, s_scr, xs_scr) = refs
    c = pl.program_id(1)
    C2 = 2 * C
    chains = [(i, j) for i in range(CB) for j in range(RW_PAIRS)]
    nch = len(chains)
    sl = lambda j: slice(j * LANES, (j + 1) * LANES)

    @pl.when(c == 0)
    def _():
        for i in range(CB):
            xs_scr[i, 0:ROW_PAD, :] = jnp.zeros((ROW_PAD, RW_COLS), F32)
        if has_state:
            z = jnp.zeros((RW_DH, RW_DH), F32)
            for n, (i, j) in enumerate(chains):
                s_scr[n] = jnp.concatenate([jnp.concatenate([s0_ref[i, 2 * j], z], axis=1),
                                            jnp.concatenate([z, s0_ref[i, 2 * j + 1]], axis=1)], axis=0)
            for i in range(CB):
                xs_scr[i, ROW_PAD - 1:ROW_PAD, :] = prev_ref[i]
        else:
            s_scr[...] = jnp.zeros_like(s_scr)

    zs, last = [], []
    for i in range(CB):
        z = z_refs[i][...].astype(F32)
        xs_scr[i, ROW_PAD:ROW_PAD + C, :] = z
        zs.append(z + (xs_scr[i, ROW_PAD - 1:ROW_PAD - 1 + C, :] - z) * mu_ref[...])
        last.append(z[Lv - 1:Lv, :])
        xs_scr[i, ROW_PAD - 1:ROW_PAD, :] = last[i]
    stacked = _rwkv_token_vectors(jnp.concatenate(zs, axis=0), wa0_ref[...], w2a_ref[...], g2_ref[...], kkp_ref[...],
                                  ka_ref[...], rk_ref[...], bd_ref[...])
    tok = [tuple(t[i * C:(i + 1) * C, :] for t in stacked) for i in range(CB)]

    srow = lax.broadcasted_iota(jnp.int32, (LANES, LANES), 0)
    scol = lax.broadcasted_iota(jnp.int32, (LANES, LANES), 1)
    same_head = jnp.logical_or(jnp.logical_and(srow < RW_DH, scol < RW_DH),
                               jnp.logical_and(srow >= RW_DH, scol >= RW_DH))
    valid = lax.broadcasted_iota(jnp.int32, (C, LANES), 0) < Lv
    row = lax.broadcasted_iota(jnp.int32, (C2, C2), 0)
    col = lax.broadcasted_iota(jnp.int32, (C2, C2), 1)
    tt = row & (C - 1)
    ss = col & (C - 1)
    top = row < C
    bot = row >= C
    strict = ss < tt
    incl = ss <= tt
    keep0 = jnp.logical_or(jnp.logical_and(top, strict), jnp.logical_and(bot, incl))
    keep1 = jnp.logical_or(jnp.logical_and(top, incl), jnp.logical_and(bot, strict))
    tl = jnp.logical_and(top, col < C)
    br = jnp.logical_and(bot, col >= C)
    eye = (row == col).astype(F32)
    tri = lax.broadcasted_iota(jnp.int32, (C, C), 0) >= lax.broadcasted_iota(jnp.int32, (C, C), 1)
    left = lax.broadcasted_iota(jnp.int32, (C, LANES), 1) < RW_DH
    left2 = lax.broadcasted_iota(jnp.int32, (C2, LANES), 1) < RW_DH
    zeros = jnp.zeros((C, LANES), F32)

    def head_mean(x):
        lsum = jnp.sum(jnp.where(left, x, 0.0), axis=1, keepdims=True)
        rsum = jnp.sum(jnp.where(left, 0.0, x), axis=1, keepdims=True)
        return jnp.where(left, lsum, rsum) * (1.0 / RW_DH)

    ar, m0, m1, vs, bk_end, tots = [], [], [], [], [], []
    for i, j in chains:
        r_i, lw_i, k_i, v_i, kk_i, kka_i, _, _ = tok[i]
        lw = jnp.where(valid, lw_i[:, sl(j)], 0.0)
        cum = _sel_dot(tri, lw)
        tot = cum[C - 1:C, :]
        e_inv = jnp.exp(-cum)
        e_end = jnp.exp(tot - cum)
        kka = jnp.where(valid, kka_i[:, sl(j)], 0.0)
        k = jnp.where(valid, k_i[:, sl(j)], 0.0)
        a_t = -jnp.where(valid, kk_i[:, sl(j)], 0.0) * jnp.exp(cum - lw)
        r_t = r_i[:, sl(j)] * jnp.exp(cum)
        b_t = kka * e_inv
        k_t = k * e_inv
        ar_c = jnp.concatenate([a_t, r_t], axis=0)
        g0 = _dot_nt(jnp.where(left2, ar_c, 0.0), jnp.concatenate([b_t, k_t], axis=0))
        g1 = _dot_nt(jnp.where(left2, 0.0, jnp.concatenate([r_t, a_t], axis=0)), jnp.concatenate([k_t, b_t], axis=0))
        ar.append(ar_c.astype(BF16))
        m0.append(jnp.where(keep0, g0, 0.0))
        m1.append(jnp.where(keep1, g1, 0.0))
        vs.append(jnp.where(valid, v_i[:, sl(j)], 0.0))
        bk_end.append(jnp.concatenate([kka * e_end, k * e_end], axis=0).astype(BF16))
        tots.append(tot)

    s_prev = [s_scr[n] for n in range(nch)]
    w_as = [_dot_nt(ar[n], s_prev[n]) for n in range(nch)]
    x0 = [_dot(m0[n][0:C, :], jnp.concatenate([zeros, vs[n]], axis=0)) for n in range(nch)]
    x1 = [_dot(m1[n][C:, :], jnp.concatenate([vs[n], zeros], axis=0)) for n in range(nch)]

    p = [jnp.where(tl, m0[n], jnp.where(br, m1[n], 0.0)) for n in range(nch)]
    t_inv = [eye + p[n] for n in range(nch)]
    if Lv > 2:
        p = [_dot(p[n], p[n]) for n in range(nch)]
    step = 2
    while step < Lv:
        if 2 * step < Lv:
            both = [_dot(jnp.concatenate([t_inv[n], p[n]], axis=0), p[n]) for n in range(nch)]
            t_inv = [t_inv[n] + both[n][0:C2, :] for n in range(nch)]
            p = [both[n][C2:, :] for n in range(nch)]
        else:
            t_inv = [t_inv[n] + _dot(t_inv[n], p[n]) for n in range(nch)]
        step *= 2

    rhs = [w_as[n][0:C, :] + jnp.where(left, x0[n], x1[n]) for n in range(nch)]
    tu = [_dot(t_inv[n], jnp.concatenate([rhs[n], rhs[n]], axis=0)) for n in range(nch)]
    u = [jnp.where(left, tu[n][0:C, :], tu[n][C:, :]) for n in range(nch)]
    uv = [jnp.concatenate([u[n], vs[n]], axis=0).astype(BF16) for n in range(nch)]
    vu = [jnp.concatenate([vs[n], u[n]], axis=0).astype(BF16) for n in range(nch)]
    o0 = [_dot(m0[n][C:, :], uv[n]) for n in range(nch)]
    o1 = [_dot(m1[n][0:C, :], vu[n]) for n in range(nch)]
    s_new = [_dot_tn(uv[n], bk_end[n]) for n in range(nch)]
    for n in range(nch):
        s_scr[n] = jnp.where(same_head, s_prev[n] * jnp.exp(tots[n]) + s_new[n], 0.0)

    for n, (i, j) in enumerate(chains):
        o = w_as[n][C:, :] + jnp.where(left, o0[n], o1[n])
        oc = o - head_mean(o)
        yn = oc * lax.rsqrt(head_mean(oc * oc) + RW_GN_EPS) * gnw_ref[:, sl(j)] + gnb_ref[:, sl(j)]
        y_ref[i, :, sl(j)] = ((yn + tok[i][7][:, sl(j)]) * tok[i][6][:, sl(j)]).astype(BF16)

    @pl.when(c == nc - 1)
    def _():
        for n, (i, j) in enumerate(chains):
            tile = s_scr[n]
            so_ref[i, 2 * j] = tile[0:RW_DH, 0:RW_DH]
            so_ref[i, 2 * j + 1] = tile[RW_DH:, RW_DH:]
        for i in range(CB):
            shift_ref[i] = last[i]


def _rwkv(zrw, weights, states, layer, *, B, T, C, CB, Lv, row0):
    nc = T // C
    has_state = states is not None
    const2 = lambda b, c: (0, 0)
    wspecs = [pl.BlockSpec(w.shape, const2) for w in weights]
    z_specs = [pl.BlockSpec((C, RW_COLS), functools.partial(lambda b, c, i: (row0 + (b * CB + i) * nc + c, 0), i=i))
               for i in range(CB)]
    in_specs = z_specs + wspecs
    args = [zrw] * CB + list(weights)
    if has_state:
        state, shift = states
        in_specs += [pl.BlockSpec((None, CB, 1, RW_COLS), lambda b, c: (layer, b, 0, 0)),
                     pl.BlockSpec((None, CB, RW_HEADS, RW_DH, RW_DH), lambda b, c: (layer, b, 0, 0, 0))]
        args += [shift.reshape(DEPTH, B, 1, RW_COLS), state]
    y, s, sh = pl.pallas_call(
        functools.partial(_rwkv_chunk_kernel, C=C, CB=CB, Lv=Lv, nc=nc, has_state=has_state),
        grid=(B // CB, nc),
        in_specs=in_specs,
        out_specs=[pl.BlockSpec((CB, C, RW_W), lambda b, c: (b, c, 0)),
                   pl.BlockSpec((CB, RW_HEADS, RW_DH, RW_DH), lambda b, c: (b, 0, 0, 0)),
                   pl.BlockSpec((CB, 1, RW_COLS), lambda b, c: (b, 0, 0))],
        out_shape=[jax.ShapeDtypeStruct((B, T, RW_W), BF16),
                   jax.ShapeDtypeStruct((B, RW_HEADS, RW_DH, RW_DH), F32),
                   jax.ShapeDtypeStruct((B, 1, RW_COLS), F32)],
        scratch_shapes=[pltpu.VMEM((CB * RW_PAIRS, LANES, LANES), F32),
                        pltpu.VMEM((CB, ROW_PAD + C, RW_COLS), F32)],
        compiler_params=_cparams(("parallel", "arbitrary")),
        name="rwkv",
    )(*args)
    return y.reshape(B * T, RW_W), s, sh[:, 0]


def _merge_kernel(x_ref, zg_ref, yrp_ref, ymp_ref, ywp_ref, yrs_ref, yms_ref, yws_ref, wb_ref, wo_ref, o_ref, *, ntp):
    def run(branches):
        acc = None
        for n, y_ref in enumerate(branches):
            proj = jnp.dot(y_ref[...], wb_ref[n], preferred_element_type=F32)
            term = _sigmoid(zg_ref[:, n * D_MODEL:(n + 1) * D_MODEL].astype(F32)) * proj
            acc = term if acc is None else acc + term
        o_ref[...] = x_ref[...] + jnp.dot(acc.astype(BF16), wo_ref[...], preferred_element_type=F32)

    i = pl.program_id(0)

    @pl.when(i < ntp)
    def _():
        run((yrp_ref, ymp_ref, ywp_ref))

    @pl.when(i >= ntp)
    def _():
        run((yrs_ref, yms_ref, yws_ref))


def _merge(x, zgate, y_prompt, y_sample, w_branch, w_out, tm=256):
    n = x.shape[0]
    ntp = y_prompt[0].shape[0] // tm
    tokspec = lambda w: pl.BlockSpec((tm, w), lambda i: (i, 0))
    pspec = pl.BlockSpec((tm, RET_W), lambda i: (jnp.minimum(i, ntp - 1), 0))
    sspec = pl.BlockSpec((tm, RET_W), lambda i: (jnp.maximum(i - ntp, 0), 0))
    return pl.pallas_call(
        functools.partial(_merge_kernel, ntp=ntp),
        grid=(n // tm,),
        in_specs=[tokspec(D_MODEL), tokspec(N_BRANCH * D_MODEL), pspec, pspec, pspec, sspec, sspec, sspec,
                  pl.BlockSpec((N_BRANCH, RET_W, D_MODEL), lambda i: (0, 0, 0)),
                  pl.BlockSpec((D_MODEL, D_MODEL), lambda i: (0, 0))],
        out_specs=tokspec(D_MODEL),
        out_shape=jax.ShapeDtypeStruct((n, D_MODEL), F32),
        compiler_params=_cparams(("arbitrary",)),
        name="merge",
    )(x, zgate, *y_prompt, *y_sample, w_branch, w_out)


def _ffn_kernel(x_ref, g_ref, wg_ref, wu_ref, wd_ref, fin_ref, o_ref, h_scr, acc_scr, *, ne, final):
    e = pl.program_id(1)

    @pl.when(e == 0)
    def _():
        h_scr[...] = _rms(x_ref[...], g_ref[...]).astype(BF16)
        acc_scr[...] = jnp.zeros_like(acc_scr)

    h = h_scr[...]
    hg = jnp.dot(h, wg_ref[0], preferred_element_type=F32)
    hu = jnp.dot(h, wu_ref[0], preferred_element_type=F32)
    acc_scr[...] += jnp.dot((_silu(hg) * hu).astype(BF16), wd_ref[0], preferred_element_type=F32)

    @pl.when(e == ne - 1)
    def _():
        out = x_ref[...] + acc_scr[...]
        if final:
            out = _rms(out, fin_ref[...])
        o_ref[...] = out


def _ffn(x, g, wg, wu, wd, fin, final, tm=512):
    n = x.shape[0]
    ne, _, f = wg.shape
    return pl.pallas_call(
        functools.partial(_ffn_kernel, ne=ne, final=final),
        grid=(n // tm, ne),
        in_specs=[pl.BlockSpec((tm, D_MODEL), lambda i, e: (i, 0)),
                  pl.BlockSpec((1, D_MODEL), lambda i, e: (0, 0)),
                  pl.BlockSpec((1, D_MODEL, f), lambda i, e: (e, 0, 0)),
                  pl.BlockSpec((1, D_MODEL, f), lambda i, e: (e, 0, 0)),
                  pl.BlockSpec((1, f, D_MODEL), lambda i, e: (e, 0, 0)),
                  pl.BlockSpec((1, D_MODEL), lambda i, e: (0, 0))],
        out_specs=pl.BlockSpec((tm, D_MODEL), lambda i, e: (i, 0)),
        out_shape=jax.ShapeDtypeStruct((n, D_MODEL), F32),
        scratch_shapes=[pltpu.VMEM((tm, D_MODEL), BF16), pltpu.VMEM((tm, D_MODEL), F32)],
        compiler_params=_cparams(("parallel", "arbitrary")),
        name="ffn",
    )(x, g, wg, wu, wd, fin)


MOE_TM = 1024
MOE_SUB = LANES
MOE_CAP = 48
MOE_NSUB = MOE_TM // MOE_SUB


def _moe_kernel(x_ref, g_ref, router_ref, wg_ref, wu_ref, wd_ref, fin_ref, o_ref,
                h_scr, acc_scr, comb_scr, combt_scr, xe_scr, *, ne, final):
    e = pl.program_id(1)
    lane = lax.broadcasted_iota(jnp.int32, (MOE_TM, LANES), 1)

    @pl.when(e == 0)
    def _():
        h = _rms(x_ref[...], g_ref[...])
        h_scr[...] = h.astype(BF16)
        acc_scr[...] = jnp.zeros_like(acc_scr)
        h_hi, h_lo = _split(h)
        r_hi, r_lo = _split(router_ref[...])
        logits = (jnp.dot(h_hi, r_hi, preferred_element_type=F32) + jnp.dot(h_lo, r_hi, preferred_element_type=F32)
                  + jnp.dot(h_hi, r_lo, preferred_element_type=F32))
        logits = jnp.where(lane < ne, logits, NEG_BIG)
        m1 = jnp.max(logits, axis=1, keepdims=True)
        i1 = jnp.min(jnp.where(logits == m1, lane, LANES), axis=1, keepdims=True)
        rest = jnp.where(lane == i1, NEG_BIG, logits)
        m2 = jnp.max(rest, axis=1, keepdims=True)
        i2 = jnp.min(jnp.where(rest == m2, lane, LANES), axis=1, keepdims=True)
        e2 = jnp.exp(m2 - m1)
        p1 = 1.0 / (1.0 + e2)
        comb = jnp.where(lane == i1, p1, 0.0) + jnp.where(lane == i2, e2 * p1, 0.0)
        comb_scr[...] = comb
        for s in range(MOE_NSUB):
            combt_scr[s] = comb[s * MOE_SUB:(s + 1) * MOE_SUB, :].T

    sub = lambda s: slice(s * MOE_SUB, (s + 1) * MOE_SUB)
    slot = lambda s: slice(s * MOE_CAP, (s + 1) * MOE_CAP)
    r_i = lax.broadcasted_iota(jnp.int32, (MOE_SUB, MOE_SUB), 0)
    c_i = lax.broadcasted_iota(jnp.int32, (MOE_SUB, MOE_SUB), 1)
    before_row = (c_i < r_i).astype(F32).astype(BF16)
    before_col = (r_i < c_i).astype(F32).astype(BF16)
    slot_row = lax.broadcasted_iota(jnp.int32, (MOE_CAP, MOE_SUB), 0).astype(F32)
    slot_col = lax.broadcasted_iota(jnp.int32, (MOE_SUB, MOE_CAP), 1).astype(F32)

    w_col = jnp.sum(jnp.where(lane == e, comb_scr[...], 0.0), axis=1, keepdims=True)
    hit_col = [(w_col[sub(s), :] > 0.0).astype(F32) for s in range(MOE_NSUB)]
    hit_row = [(combt_scr[s, pl.ds(e, 1), :] > 0.0).astype(F32) for s in range(MOE_NSUB)]
    rank_col = [jnp.dot(before_row, jnp.broadcast_to(hit_col[s], (MOE_SUB, MOE_CAP)).astype(BF16),
                        preferred_element_type=F32) for s in range(MOE_NSUB)]
    rank_row = [jnp.dot(jnp.broadcast_to(hit_row[s], (SUBLANES, MOE_SUB)).astype(BF16), before_col,
                        preferred_element_type=F32)[0:1, :] for s in range(MOE_NSUB)]
    count = jnp.sum(hit_row[0], axis=1, keepdims=True)
    for s in range(1, MOE_NSUB):
        count = jnp.maximum(count, jnp.sum(hit_row[s], axis=1, keepdims=True))
    n_pass = lax.div(jnp.max(count).astype(jnp.int32) + (MOE_CAP - 1), MOE_CAP)

    def one_pass(p, carry):
        base = (p * MOE_CAP).astype(F32)
        for s in range(MOE_NSUB):
            pick = jnp.logical_and(rank_row[s] - base == slot_row, hit_row[s] > 0.0).astype(F32).astype(BF16)
            xe_scr[slot(s), :] = jnp.dot(pick, h_scr[sub(s), :], preferred_element_type=F32).astype(BF16)
        xe = xe_scr[...]
        hg = jnp.dot(xe, wg_ref[0], preferred_element_type=F32)
        hu = jnp.dot(xe, wu_ref[0], preferred_element_type=F32)
        y = jnp.dot((_silu(hg) * hu).astype(BF16), wd_ref[0], preferred_element_type=F32)
        for s in range(MOE_NSUB):
            put = jnp.logical_and(rank_col[s] - base == slot_col, hit_col[s] > 0.0).astype(F32).astype(BF16)
            hi, lo = _split(y[slot(s), :])
            back = jnp.dot(put, hi, preferred_element_type=F32) + jnp.dot(put, lo, preferred_element_type=F32)
            acc_scr[sub(s), :] += w_col[sub(s), :] * back
        return carry

    lax.fori_loop(0, n_pass, one_pass, 0)

    @pl.when(e == ne - 1)
    def _():
        out = x_ref[...] + acc_scr[...]
        if final:
            out = _rms(out, fin_ref[...])
        o_ref[...] = out


def _moe(x, g, wg, wu, wd, router, fin, final):
    n = x.shape[0]
    ne, _, f = wg.shape
    return pl.pallas_call(
        functools.partial(_moe_kernel, ne=ne, final=final),
        grid=(n // MOE_TM, ne),
        in_specs=[pl.BlockSpec((MOE_TM, D_MODEL), lambda i, e: (i, 0)),
                  pl.BlockSpec((1, D_MODEL), lambda i, e: (0, 0)),
                  pl.BlockSpec((D_MODEL, LANES), lambda i, e: (0, 0)),
                  pl.BlockSpec((1, D_MODEL, f), lambda i, e: (e, 0, 0)),
                  pl.BlockSpec((1, D_MODEL, f), lambda i, e: (e, 0, 0)),
                  pl.BlockSpec((1, f, D_MODEL), lambda i, e: (e, 0, 0)),
                  pl.BlockSpec((1, D_MODEL), lambda i, e: (0, 0))],
        out_specs=pl.BlockSpec((MOE_TM, D_MODEL), lambda i, e: (i, 0)),
        out_shape=jax.ShapeDtypeStruct((n, D_MODEL), F32),
        scratch_shapes=[pltpu.VMEM((MOE_TM, D_MODEL), BF16), pltpu.VMEM((MOE_TM, D_MODEL), F32),
                        pltpu.VMEM((MOE_TM, LANES), F32), pltpu.VMEM((MOE_NSUB, LANES, MOE_SUB), F32),
                        pltpu.VMEM((MOE_NSUB * MOE_CAP, D_MODEL), BF16)],
        compiler_params=_cparams(("parallel", "arbitrary")),
        name="moe",
    )(x, g, router, wg, wu, wd, fin)


def _row(v):
    return v.reshape(1, -1)


def kernel(x_prompt, x_sample, state_ret, state_mlstm_C, state_mlstm_n, state_mlstm_m, state_mlstm_conv, state_rwkv, state_rwkv_shift, norm_mix, w_in, ret_gn, ml_conv_w, ml_conv_b, ml_wq, ml_wk, ml_bi, ml_bf, ml_gn, ml_skip, rw_mu, rw_w0, rw_w2, rw_a0, rw_a2, rw_g2, rw_kk, rw_ka, rw_rk, rw_gn_w, rw_gn_b, w_branch, w_out, norm_ffn, ffn_w_gate, ffn_w_up, ffn_w_down, moe_router, moe_w_gate, moe_w_up, moe_w_down, final_norm):
    nc_p = SEQ // CHUNK
    xs = jnp.pad(x_sample, ((0, 0), (0, S_PAD - DEC_SEQ), (0, 0)))
    x = jnp.concatenate([x_prompt.reshape(N_PROMPT, D_MODEL), xs.reshape(N_SAMPLE, D_MODEL)], axis=0)

    pos_p = jnp.arange(SEQ, dtype=jnp.int32)
    pos_s = PAST_LEN + jnp.arange(S_PAD, dtype=jnp.int32)
    ret_tab_p = _retention_tables(CHUNK, CHUNK, pos_p)
    ret_tab_s = _retention_tables(S_PAD, DEC_SEQ, pos_s)
    head_of = jnp.arange(RW_W) // RW_DH
    bd64 = (head_of[:, None] == head_of[None, :]).astype(BF16)
    m0_all = jnp.pad(state_mlstm_m, ((0, 0), (0, 0), (0, LANES - ML_HEADS))).reshape(DEPTH, DEC_BATCH, 1, LANES)

    o_rq = 0
    o_mx = o_rq + RET_COLS
    o_mi = o_mx + 2 * ML_W
    o_mo = o_mi + 2 * ML_HEADS
    o_rw = o_mo + ML_W
    o_gate = o_rw + RW_COLS

    new_p, new_s = [], []
    for l in range(DEPTH):
        w = w_in[l]
        w_ret = w[:, o_rq:o_mx].astype(BF16)
        w_ml = jnp.concatenate([w[:, o_mx:o_mi], w[:, o_mo:o_rw], w[:, o_mi:o_mo],
                                jnp.zeros((D_MODEL, ML_GATE_W - 2 * ML_HEADS), F32)], axis=1).astype(BF16)
        w_rw = w[:, o_rw:o_gate].astype(BF16)
        w_gate = w[:, o_gate:].astype(BF16)
        g_mix = _row(norm_mix[l])
        zret, = _norm_matmul(x, g_mix, w_ret, [(RET_COLS, BF16)])
        zml, zml_gates = _norm_matmul(x, g_mix, w_ml, [(3 * ML_W, BF16), (ML_GATE_W, F32)])
        zrw, = _norm_matmul(x, g_mix, w_rw, [(RW_COLS, BF16)])
        zgate, = _norm_matmul(x, g_mix, w_gate, [(N_BRANCH * D_MODEL, BF16)])

        gn = _row(ret_gn[l])
        yr_p, ret_p = _retention(zret, ret_tab_p, gn, None, l, B=BATCH, nc=nc_p, L=CHUNK, nb=RET_NB_PROMPT,
                                 split=True, row0=0)
        yr_s, ret_s = _retention(zret, ret_tab_s, gn, state_ret, l, B=DEC_BATCH, nc=1, L=S_PAD, nb=RET_NB_SAMPLE,
                                 split=False, row0=N_PROMPT // (RET_NB_SAMPLE * S_PAD))

        gate_bias = jnp.concatenate([ml_bi[l], ml_bf[l], jnp.zeros((ML_GATE_W - 2 * ML_HEADS,), F32)])
        ml_weights = (ml_conv_w[l], _row(ml_conv_b[l]), ml_wq[l].astype(BF16), ml_wk[l].astype(BF16),
                      _row(gate_bias), _row(ml_gn[l]), _row(ml_skip[l]))
        ym_p, c_p, n_p, m_p, buf_p = _mlstm(zml, zml_gates, ml_weights, None, l, B=BATCH, nc=nc_p, L=CHUNK, Lv=CHUNK,
                                            nb=ML_NB_PROMPT, split=True, row0=0)
        ym_s, c_s, n_s, m_s, buf_s = _mlstm(zml, zml_gates, ml_weights,
                                            (state_mlstm_C, state_mlstm_n, m0_all, state_mlstm_conv),
                                            l, B=DEC_BATCH, nc=1, L=S_PAD, Lv=DEC_SEQ, nb=ML_NB_SAMPLE, split=False,
                                            row0=N_PROMPT // (ML_NB_SAMPLE * S_PAD))

        w2a = jnp.zeros((RW_LORA_W + RW_LORA_A, 2 * RW_W), F32)
        w2a = w2a.at[:RW_LORA_W, :RW_W].set(rw_w2[l]).at[RW_LORA_W:, RW_W:].set(rw_a2[l])
        rw_weights = (_row(rw_mu[l]), _row(jnp.concatenate([rw_w0[l], rw_a0[l]])), w2a.astype(BF16),
                      rw_g2[l].astype(BF16), _row(rw_kk[l]), _row(rw_ka[l]), _row(rw_rk[l]), bd64,
                      _row(rw_gn_w[l]), _row(rw_gn_b[l]))
        yw_p, rws_p, shift_p = _rwkv(zrw, rw_weights, None, l, B=BATCH, T=SEQ, C=RW_C, CB=RW_CB_PROMPT, Lv=RW_C,
                                     row0=0)
        yw_s, rws_s, shift_s = _rwkv(zrw, rw_weights, (state_rwkv, state_rwkv_shift), l, B=DEC_BATCH, T=S_PAD,
                                     C=S_PAD, CB=RW_CB_SAMPLE, Lv=DEC_SEQ, row0=N_PROMPT // S_PAD)

        x = _merge(x, zgate, (yr_p, ym_p, yw_p), (yr_s, ym_s, yw_s), w_branch[l].astype(BF16), w_out[l].astype(BF16))

        i = l // 2
        g_ffn = _row(norm_ffn[l])
        fin = _row(final_norm)
        final = l == DEPTH - 1
        if l % 2 == 0:
            halves = D_FF // D_FF_EXPERT
            wg = ffn_w_gate[i].reshape(D_MODEL, halves, D_FF_EXPERT).transpose(1, 0, 2).astype(BF16)
            wu = ffn_w_up[i].reshape(D_MODEL, halves, D_FF_EXPERT).transpose(1, 0, 2).astype(BF16)
            wd = ffn_w_down[i].reshape(halves, D_FF_EXPERT, D_MODEL).astype(BF16)
            x = _ffn(x, g_ffn, wg, wu, wd, fin, final)
        else:
            router = jnp.pad(moe_router[i], ((0, 0), (0, LANES - N_EXPERTS)))
            x = _moe(x, g_ffn, moe_w_gate[i].astype(BF16), moe_w_up[i].astype(BF16), moe_w_down[i].astype(BF16),
                     router, fin, final)

        new_p.append((ret_p, c_p, n_p, m_p[:, 0, :ML_HEADS], buf_p, rws_p, shift_p))
        new_s.append((ret_s, c_s, n_s, m_s[:, 0, :ML_HEADS], buf_s, rws_s, shift_s))

    y_prompt = x[:N_PROMPT].reshape(BATCH, SEQ, D_MODEL)
    y_sample = x[N_PROMPT:].reshape(DEC_BATCH, S_PAD, D_MODEL)[:, :DEC_SEQ]
    st_p = tuple(jnp.stack([st[j] for st in new_p]) for j in range(7))
    st_s = tuple(jnp.stack([st[j] for st in new_s]) for j in range(7))
    return (y_prompt, y_sample) + st_p + st_s
```

```python
import functools

import jax
import jax.numpy as jnp
from jax import lax
from jax.experimental import pallas as pl
from jax.experimental.pallas import tpu as pltpu

D_MODEL = 1024
BATCH = 8
SEQ = 2048
DEPTH = 2
DEC_BATCH = 128
DEC_SEQ = 4
PAST_LEN = 16384
RET_HEADS = 4
RET_DK = 64
RET_DV = 128
ML_HEADS = 4
ML_DH = 128
CONV_W = 4
RW_HEADS = 8
RW_DH = 64
RW_LORA_W = 64
RW_LORA_A = 64
RW_LORA_G = 128
RET_W = RET_HEADS * RET_DV
ML_W = ML_HEADS * ML_DH
RW_W = RW_HEADS * RW_DH
N_BRANCH = 3
RW_COLS = 3 * RW_W + RW_LORA_W + RW_LORA_A + RW_LORA_G
D_FF = 2816
N_EXPERTS = 8
D_FF_EXPERT = 1408
CHUNK = 128
NORM_EPS = 1e-6
GN_EPS = 1e-5
RW_GN_EPS = 64e-5
ROPE_BASE = 10000.0
RW_DECAY_SCALE = 0.606531

LANES = 128
SUBLANES = 8
S_PAD = 16
N_PROMPT = BATCH * SEQ
N_SAMPLE = DEC_BATCH * S_PAD
N_TOK = N_PROMPT + N_SAMPLE
ML_GATE_W = LANES
ML_COLS = 3 * ML_W + ML_GATE_W
RET_COLS = 2 * RET_HEADS * RET_DK + 2 * RET_W
NEG_BIG = -1e30
VMEM_LIMIT = 56 * 1024 * 1024

RW_C = 64
RW_PAIRS = RW_HEADS // 2
RW_CB_PROMPT = 4
RW_CB_SAMPLE = 4
RET_NB_PROMPT = 4
RET_NB_SAMPLE = 8
ML_NB_PROMPT = 1
ML_NB_SAMPLE = 4

F32 = jnp.float32
BF16 = jnp.bfloat16


def _cparams(sem):
    return pltpu.CompilerParams(dimension_semantics=sem, vmem_limit_bytes=VMEM_LIMIT)


def _sigmoid(x):
    return 1.0 / (1.0 + jnp.exp(-x))


def _silu(x):
    return x * _sigmoid(x)


def _rms(x, g):
    return x * lax.rsqrt(jnp.mean(x * x, axis=-1, keepdims=True) + NORM_EPS) * g


def _dot(a, b):
    return jnp.dot(a.astype(BF16), b.astype(BF16), preferred_element_type=F32)


def _dot_nt(a, b):
    return lax.dot_general(a.astype(BF16), b.astype(BF16), (((1,), (1,)), ((), ())), preferred_element_type=F32)


def _dot_tn(a, b):
    return lax.dot_general(a.astype(BF16), b.astype(BF16), (((0,), (0,)), ((), ())), preferred_element_type=F32)


def _split(x):
    hi = x.astype(BF16)
    return hi, (x - hi.astype(F32)).astype(BF16)


def _as_bf16_mask(m):
    return m if m.dtype == BF16 else m.astype(F32).astype(BF16)


def _sel_dot(m, x):
    hi, lo = _split(x)
    m = _as_bf16_mask(m)
    return jnp.dot(m, hi, preferred_element_type=F32) + jnp.dot(m, lo, preferred_element_type=F32)


def _dot_sel(x, m):
    hi, lo = _split(x)
    m = _as_bf16_mask(m)
    return jnp.dot(hi, m, preferred_element_type=F32) + jnp.dot(lo, m, preferred_element_type=F32)


def _sel_dot_nt(m, x):
    hi, lo = _split(x)
    m = _as_bf16_mask(m)
    dn = (((1,), (1,)), ((), ()))
    return (lax.dot_general(m, hi, dn, preferred_element_type=F32)
            + lax.dot_general(m, lo, dn, preferred_element_type=F32))


def _norm_matmul_kernel(x_ref, g_ref, w_ref, *o_refs, bounds):
    h = _rms(x_ref[...], g_ref[...])
    z = jnp.dot(h.astype(BF16), w_ref[...], preferred_element_type=F32)
    for o_ref, (lo, hi) in zip(o_refs, bounds):
        o_ref[...] = z[:, lo:hi].astype(o_ref.dtype)


def _norm_matmul(x, g, w, outs, tm=512):
    n, d = x.shape
    c = w.shape[1]
    bounds, lo = [], 0
    for width, _ in outs:
        bounds.append((lo, lo + width))
        lo += width
    assert lo == c
    return pl.pallas_call(
        functools.partial(_norm_matmul_kernel, bounds=tuple(bounds)),
        grid=(n // tm,),
        in_specs=[pl.BlockSpec((tm, d), lambda i: (i, 0)),
                  pl.BlockSpec((1, d), lambda i: (0, 0)),
                  pl.BlockSpec((d, c), lambda i: (0, 0))],
        out_specs=[pl.BlockSpec((tm, width), lambda i: (i, 0)) for width, _ in outs],
        out_shape=[jax.ShapeDtypeStruct((n, width), dt) for width, dt in outs],
        compiler_params=_cparams(("parallel",)),
        name="norm_matmul",
    )(x, g, w)


def _seq_views(refs, nb, L, split):
    if split:
        return list(refs)
    return [refs[0].at[pl.ds(i * L, L)] for i in range(nb)]


def _retention_kernel(*refs, L, nc, nb, split, has_state):
    nv = nb if split else 1
    q_v, k_v, v_v, g_v = (_seq_views(refs[j * nv:(j + 1) * nv], nb, L, split) for j in range(4))
    refs = refs[4 * nv:]
    if has_state:
        cos_ref, sin_ref, di_ref, dq_ref, dk_ref, dc_ref, gn_ref, s0_ref, y_ref, so_ref, s_scr = refs
    else:
        cos_ref, sin_ref, di_ref, dq_ref, dk_ref, dc_ref, gn_ref, y_ref, so_ref, s_scr = refs
    y_v = [y_ref.at[i] for i in range(nb)]
    c = pl.program_id(1)

    @pl.when(c == 0)
    def _():
        if has_state:
            s_scr[...] = s0_ref[...]
        else:
            s_scr[...] = jnp.zeros_like(s_scr)

    qk_w = RET_HEADS * RET_DK
    half = RET_DK // 2
    lane = lax.broadcasted_iota(jnp.int32, (L, qk_w), 1)
    first_half = (lane % RET_DK) < half
    cos = cos_ref[...]
    sin = sin_ref[...]

    def rot(x):
        swapped = jnp.where(first_half, pltpu.roll(x, qk_w - half, 1), pltpu.roll(x, half, 1))
        return x * cos + swapped * sin

    q = [rot(q_v[i][...].astype(F32)) for i in range(nb)]
    k = [rot(k_v[i][...].astype(F32)) * (RET_DK ** -0.5) for i in range(nb)]
    dq = dq_ref[...]
    dk = dk_ref[...]
    dc = dc_ref[...]
    chains = [(i, h) for i in range(nb) for h in range(RET_HEADS)]
    kcols = lambda h: slice(h * RET_DK, (h + 1) * RET_DK)
    vcols = lambda h: slice(h * RET_DV, (h + 1) * RET_DV)

    qh = [q[i][:, kcols(h)].astype(BF16) for i, h in chains]
    kh = [k[i][:, kcols(h)] for i, h in chains]
    vh = [v_v[i][:, vcols(h)].astype(BF16) for i, h in chains]
    s = [_dot_nt(qh[n], kh[n]) * di_ref[h] for n, (i, h) in enumerate(chains)]
    s_prev = [s_scr[i, h] for i, h in chains]
    cross = [_dot(qh[n], s_prev[n]) for n in range(len(chains))]
    upd = [_dot_tn(kh[n] * dk[:, h:h + 1], vh[n]) for n, (i, h) in enumerate(chains)]
    inner = [_dot(s[n], vh[n]) for n in range(len(chains))]
    ones = jnp.ones((RET_DV, RET_DV), BF16)
    if L >= RET_DV:
        row_mean = lambda x: _dot_sel(x, ones) * (1.0 / RET_DV)
    else:
        row_mean = lambda x: jnp.mean(x, axis=-1, keepdims=True)
    o = []
    for n, (i, h) in enumerate(chains):
        s_scr[i, h] = s_prev[n] * dc[:, h:h + 1] + upd[n]
        o.append(inner[n] + cross[n] * dq[:, h:h + 1])
    oc = [o[n] - row_mean(o[n]) for n in range(len(chains))]
    var = [row_mean(oc[n] * oc[n]) for n in range(len(chains))]
    for n, (i, h) in enumerate(chains):
        yn = oc[n] * lax.rsqrt(var[n] + GN_EPS) * gn_ref[:, vcols(h)]
        y_v[i][:, vcols(h)] = (_silu(g_v[i][:, vcols(h)].astype(F32)) * yn).astype(BF16)

    @pl.when(c == nc - 1)
    def _():
        so_ref[...] = s_scr[...]


def _token_specs(cols, col_block, *, L, nb, nc, row0, split):
    if split:
        return [pl.BlockSpec((L, cols), functools.partial(
            lambda b, c, i: (row0 + (b * nb + i) * nc + c, col_block), i=i)) for i in range(nb)]
    assert nc == 1
    return [pl.BlockSpec((nb * L, cols), lambda b, c: (row0 + b, col_block))]


def _retention(zret, tables, gn, state, layer, *, B, nc, L, nb, split, row0):
    cos, sin, di, dq, dk, dc = tables
    has_state = state is not None
    qk_w = RET_HEADS * RET_DK
    nv = nb if split else 1
    tok = functools.partial(_token_specs, L=L, nb=nb, nc=nc, row0=row0, split=split)
    st = pl.BlockSpec((nb, RET_HEADS, RET_DK, RET_DV), lambda b, c: (b, 0, 0, 0))
    in_specs = (tok(qk_w, 0) + tok(qk_w, 1) + tok(RET_W, 1) + tok(RET_W, 2)
                + [pl.BlockSpec((L, qk_w), lambda b, c: (c, 0)),
                   pl.BlockSpec((L, qk_w), lambda b, c: (c, 0)),
                   pl.BlockSpec((RET_HEADS, L, L), lambda b, c: (0, 0, 0)),
                   pl.BlockSpec((L, RET_HEADS), lambda b, c: (0, 0)),
                   pl.BlockSpec((L, RET_HEADS), lambda b, c: (0, 0)),
                   pl.BlockSpec((1, RET_HEADS), lambda b, c: (0, 0)),
                   pl.BlockSpec((1, RET_W), lambda b, c: (0, 0))])
    args = [zret] * (4 * nv) + [cos, sin, di, dq, dk, dc, gn]
    if has_state:
        in_specs.append(pl.BlockSpec((None, nb, RET_HEADS, RET_DK, RET_DV), lambda b, c: (layer, b, 0, 0, 0)))
        args.append(state)
    y, s = pl.pallas_call(
        functools.partial(_retention_kernel, L=L, nc=nc, nb=nb, split=split, has_state=has_state),
        grid=(B // nb, nc),
        in_specs=in_specs,
        out_specs=[pl.BlockSpec((nb, L, RET_W), lambda b, c: (b, c, 0)), st],
        out_shape=[jax.ShapeDtypeStruct((B, nc * L, RET_W), BF16),
                   jax.ShapeDtypeStruct((B, RET_HEADS, RET_DK, RET_DV), F32)],
        scratch_shapes=[pltpu.VMEM((nb, RET_HEADS, RET_DK, RET_DV), F32)],
        compiler_params=_cparams(("parallel", "arbitrary")),
        name="retention",
    )(*args)
    return y.reshape(B * nc * L, RET_W), s


def _retention_tables(L, Lv, pos):
    half = RET_DK // 2
    inv = ROPE_BASE ** (-jnp.arange(half, dtype=F32) / half)
    ang = pos.astype(F32)[:, None] * inv[None, :]
    cos = jnp.tile(jnp.concatenate([jnp.cos(ang), jnp.cos(ang)], axis=1), (1, RET_HEADS))
    sin = jnp.tile(jnp.concatenate([-jnp.sin(ang), jnp.sin(ang)], axis=1), (1, RET_HEADS))
    log_gamma = jnp.log1p(-(2.0 ** (-5.0 - jnp.arange(RET_HEADS, dtype=F32))))
    idx = jnp.arange(L, dtype=F32)
    diff = idx[:, None] - idx[None, :]
    causal = diff >= 0
    di = jnp.where(causal[None], jnp.exp(jnp.where(causal, diff, 0.0)[None] * log_gamma[:, None, None]), 0.0)
    dq = jnp.exp((idx[:, None] + 1.0) * log_gamma[None, :])
    dk = jnp.where((idx < Lv)[:, None], jnp.exp((Lv - 1.0 - idx)[:, None] * log_gamma[None, :]), 0.0)
    dc = jnp.exp(Lv * log_gamma)[None, :]
    return cos, sin, di, dq, dk, dc


ROW_PAD = SUBLANES


def _mlstm_kernel(*refs, L, Lv, nc, nb, split, has_state):
    nv = nb if split else 1
    mx_v, mv_v, mo_v, gz_v = (_seq_views(refs[j * nv:(j + 1) * nv], nb, L, split) for j in range(4))
    refs = refs[4 * nv:]
    if has_state:
        (cw_ref, cb_ref, wq_ref, wk_ref, gb_ref, gn_ref, skip_ref, c0_ref, n0_ref, m0_ref, buf0_ref,
         y_ref, co_ref, no_ref, mout_ref, bufo_ref, xp_scr, c_scr, n_scr, m_scr) = refs
    else:
        (cw_ref, cb_ref, wq_ref, wk_ref, gb_ref, gn_ref, skip_ref,
         y_ref, co_ref, no_ref, mout_ref, bufo_ref, xp_scr, c_scr, n_scr, m_scr) = refs
    y_v = [y_ref.at[i] for i in range(nb)]
    c = pl.program_id(1)
    tail = CONV_W - 1

    @pl.when(c == 0)
    def _():
        for i in range(nb):
            xp_scr[i, 0:ROW_PAD, :] = jnp.zeros((ROW_PAD, ML_W), F32)
        if has_state:
            c_scr[...] = c0_ref[...]
            n_scr[...] = n0_ref[...]
            m_scr[...] = m0_ref[...]
            for i in range(nb):
                xp_scr[i, ROW_PAD - tail:ROW_PAD, :] = buf0_ref[i]
        else:
            c_scr[...] = jnp.zeros_like(c_scr)
            n_scr[...] = jnp.zeros_like(n_scr)
            m_scr[...] = jnp.zeros_like(m_scr)

    cols = lambda h: slice(h * ML_DH, (h + 1) * ML_DH)
    lane = lax.broadcasted_iota(jnp.int32, (L, ML_GATE_W), 1)
    rowi = lax.broadcasted_iota(jnp.int32, (L, ML_GATE_W), 0)
    tri = lax.broadcasted_iota(jnp.int32, (L, L), 0) >= lax.broadcasted_iota(jnp.int32, (L, L), 1)
    eye = (lax.broadcasted_iota(jnp.int32, (SUBLANES, ML_GATE_W), 0)
           == lax.broadcasted_iota(jnp.int32, (SUBLANES, ML_GATE_W), 1))

    xc, new_tail, gc, fc, g_rows, f_rows = [], [], [], [], [], []
    for i in range(nb):
        xp_scr[i, ROW_PAD:ROW_PAD + L, :] = mx_v[i][...].astype(F32)
        acc = cb_ref[...]
        for j in range(CONV_W):
            acc = acc + cw_ref[j:j + 1, :] * xp_scr[i, ROW_PAD - tail + j:ROW_PAD - tail + j + L, :]
        xc.append(_silu(acc))
        new_tail.append(xp_scr[i, ROW_PAD + Lv - tail:ROW_PAD + Lv, :])
        xp_scr[i, ROW_PAD - tail:ROW_PAD, :] = new_tail[i]
        gz = gz_v[i][...] + gb_ref[...]
        logsig = jnp.minimum(gz, 0.0) - jnp.log1p(jnp.exp(-jnp.abs(gz)))
        g = jnp.where(lane < ML_HEADS, gz, logsig)
        if Lv < L:
            g = jnp.where(rowi < Lv, g, jnp.where(lane < ML_HEADS, NEG_BIG, 0.0))
        gc.append(g)
        fc.append(_sel_dot(tri, g))
        g_rows.append(_sel_dot_nt(eye, g))
        f_rows.append(_sel_dot_nt(eye, fc[i]))

    chains = [(i, h) for i in range(nb) for h in range(ML_HEADS)]
    nch = len(chains)
    xh = [xc[i][:, cols(h)] for i, h in chains]
    xh_b = [x.astype(BF16) for x in xh]
    q = [jnp.dot(xh_b[n], wq_ref[h], preferred_element_type=F32) for n, (i, h) in enumerate(chains)]
    k = [jnp.dot(xh_b[n], wk_ref[h], preferred_element_type=F32) * (ML_DH ** -0.5) for n, (i, h) in enumerate(chains)]
    q_b = [x.astype(BF16) for x in q]
    v_b = [mv_v[i][:, cols(h)].astype(BF16) for i, h in chains]
    qk = [_dot_nt(q_b[n], k[n]) for n in range(nch)]
    c_prev = [c_scr[i, h] for i, h in chains]
    qc = [_dot(q_b[n], c_prev[n]) for n in range(nch)]

    s, a_inter, m_t, kw, a_old, m_new = [], [], [], [], [], []
    for n, (i, h) in enumerate(chains):
        f_col = fc[i][:, ML_HEADS + h:ML_HEADS + h + 1]
        i_col = gc[i][:, h:h + 1]
        f_row = f_rows[i][ML_HEADS + h:ML_HEADS + h + 1, :]
        i_row = g_rows[i][h:h + 1, :]
        m_prev = m_scr[i, 0:1, h:h + 1]
        dlog = jnp.where(tri, f_col - f_row + i_row, NEG_BIG)
        inter = f_col + m_prev
        m_t.append(jnp.maximum(jnp.max(dlog, axis=1, keepdims=True), inter))
        s.append(qk[n] * jnp.exp(dlog - m_t[n]))
        a_inter.append(jnp.exp(inter - m_t[n]))
        f_last = fc[i][Lv - 1:Lv, ML_HEADS + h:ML_HEADS + h + 1]
        m_new.append(jnp.maximum(f_last + m_prev, jnp.max(f_last - f_row + i_row, axis=1, keepdims=True)))
        kw.append(k[n] * jnp.exp(f_last - f_col + i_col - m_new[n]))
        a_old.append(jnp.exp(f_last + m_prev - m_new[n]))

    sv = [_dot(s[n], v_b[n]) for n in range(nch)]
    ktv = [_dot_tn(kw[n], v_b[n]) for n in range(nch)]
    hh = []
    for n, (i, h) in enumerate(chains):
        n_prev = n_scr[i, h:h + 1, :]
        num = sv[n] + a_inter[n] * qc[n]
        den = jnp.sum(s[n], axis=1, keepdims=True) + a_inter[n] * jnp.sum(q[n] * n_prev, axis=1, keepdims=True)
        hh.append(num * (1.0 / jnp.maximum(jnp.abs(den), jnp.exp(-m_t[n]))))
        c_scr[i, h] = a_old[n] * c_prev[n] + ktv[n]
        n_scr[i, h:h + 1, :] = a_old[n] * n_prev + jnp.sum(kw[n], axis=0, keepdims=True)
        m_scr[i, 0:1, h:h + 1] = m_new[n]
    ones = jnp.ones((ML_DH, ML_DH), BF16)
    hc = [hh[n] - _dot_sel(hh[n], ones) * (1.0 / ML_DH) for n in range(nch)]
    var = [_dot_sel(hc[n] * hc[n], ones) * (1.0 / ML_DH) for n in range(nch)]
    for n, (i, h) in enumerate(chains):
        hn = hc[n] * lax.rsqrt(var[n] + GN_EPS) * gn_ref[:, cols(h)]
        gate = _sigmoid(mo_v[i][:, cols(h)].astype(F32))
        y_v[i][:, cols(h)] = (gate * (hn + skip_ref[:, cols(h)] * xh[n])).astype(BF16)

    @pl.when(c == nc - 1)
    def _():
        co_ref[...] = c_scr[...]
        no_ref[...] = n_scr[...]
        mout_ref[...] = m_scr[...]
        for i in range(nb):
            bufo_ref[i] = new_tail[i]


def _mlstm(zml, zml_gates, weights, state, layer, *, B, nc, L, Lv, nb, split, row0):
    cw, cb, wq, wk, gb, gn, skip = weights
    has_state = state is not None
    nv = nb if split else 1
    const2 = lambda b, c: (0, 0)
    tok = functools.partial(_token_specs, L=L, nb=nb, nc=nc, row0=row0, split=split)
    in_specs = (tok(ML_W, 0) + tok(ML_W, 1) + tok(ML_W, 2) + tok(ML_GATE_W, 0)
                + [pl.BlockSpec((CONV_W, ML_W), const2),
                   pl.BlockSpec((1, ML_W), const2),
                   pl.BlockSpec((ML_HEADS, ML_DH, ML_DH), lambda b, c: (0, 0, 0)),
                   pl.BlockSpec((ML_HEADS, ML_DH, ML_DH), lambda b, c: (0, 0, 0)),
                   pl.BlockSpec((1, ML_GATE_W), const2),
                   pl.BlockSpec((1, ML_W), const2),
                   pl.BlockSpec((1, ML_W), const2)])
    args = [zml] * (3 * nv) + [zml_gates] * nv + [cw, cb, wq, wk, gb, gn, skip]
    st_specs = [pl.BlockSpec((nb, ML_HEADS, ML_DH, ML_DH), lambda b, c: (b, 0, 0, 0)),
                pl.BlockSpec((nb, ML_HEADS, ML_DH), lambda b, c: (b, 0, 0)),
                pl.BlockSpec((nb, 1, LANES), lambda b, c: (b, 0, 0)),
                pl.BlockSpec((nb, CONV_W - 1, ML_W), lambda b, c: (b, 0, 0))]
    if has_state:
        in_specs += [pl.BlockSpec((None, nb, ML_HEADS, ML_DH, ML_DH), lambda b, c: (layer, b, 0, 0, 0)),
                     pl.BlockSpec((None, nb, ML_HEADS, ML_DH), lambda b, c: (layer, b, 0, 0)),
                     pl.BlockSpec((None, nb, 1, LANES), lambda b, c: (layer, b, 0, 0)),
                     pl.BlockSpec((None, nb, CONV_W - 1, ML_W), lambda b, c: (layer, b, 0, 0))]
        args += list(state)
    y, *new_state = pl.pallas_call(
        functools.partial(_mlstm_kernel, L=L, Lv=Lv, nc=nc, nb=nb, split=split, has_state=has_state),
        grid=(B // nb, nc),
        in_specs=in_specs,
        out_specs=[pl.BlockSpec((nb, L, ML_W), lambda b, c: (b, c, 0))] + st_specs,
        out_shape=[jax.ShapeDtypeStruct((B, nc * L, ML_W), BF16),
                   jax.ShapeDtypeStruct((B, ML_HEADS, ML_DH, ML_DH), F32),
                   jax.ShapeDtypeStruct((B, ML_HEADS, ML_DH), F32),
                   jax.ShapeDtypeStruct((B, 1, LANES), F32),
                   jax.ShapeDtypeStruct((B, CONV_W - 1, ML_W), F32)],
        scratch_shapes=[pltpu.VMEM((nb, ROW_PAD + L, ML_W), F32),
                        pltpu.VMEM((nb, ML_HEADS, ML_DH, ML_DH), F32),
                        pltpu.VMEM((nb, ML_HEADS, ML_DH), F32),
                        pltpu.VMEM((nb, 1, LANES), F32)],
        compiler_params=_cparams(("parallel", "arbitrary")),
        name="mlstm",
    )(*args)
    return (y.reshape(B * nc * L, ML_W), *new_state)


def _rwkv_token_vectors(zs, wa0, w2a, g2, kkp, ka, rk, bd):
    r = zs[:, 0:RW_W]
    kr = zs[:, RW_W:2 * RW_W]
    vr = zs[:, 2 * RW_W:3 * RW_W]
    wa = zs[:, 3 * RW_W:3 * RW_W + RW_LORA_W + RW_LORA_A]
    gl = zs[:, 3 * RW_W + RW_LORA_W + RW_LORA_A:]
    lane = lax.broadcasted_iota(jnp.int32, wa.shape, 1)
    lora = jnp.dot(jnp.where(lane < RW_LORA_W, jnp.tanh(wa), wa).astype(BF16), w2a, preferred_element_type=F32) + wa0
    g = jnp.dot(_sigmoid(gl).astype(BF16), g2, preferred_element_type=F32)
    kk = kr * kkp
    kk = kk / jnp.maximum(jnp.sqrt(_dot_sel(kk * kk, bd)), 1e-12)
    a = _sigmoid(lora[:, RW_W:])
    k2 = kr * (1.0 + (a - 1.0) * ka)
    bonus = _dot_sel(r * k2 * rk, bd) * vr
    return r, -RW_DECAY_SCALE * _sigmoid(lora[:, 0:RW_W]), k2, vr, kk, kk * a, g, bonus


def _rwkv_chunk_kernel(*refs, C, CB, Lv, nc, has_state):
    z_refs = refs[:CB]
    refs = refs[CB:]
    if has_state:
        (mu_ref, wa0_ref, w2a_ref, g2_ref, kkp_ref, ka_ref, rk_ref, bd_ref, gnw_ref, gnb_ref, prev_ref, s0_ref,
         y_ref, so_ref, shift_ref, s_scr, xs_scr) = refs
    else:
        (mu_ref, wa0_ref, w2a_ref, g2_ref, kkp_ref, ka_ref, rk_ref, bd_ref, gnw_ref, gnb_ref,
         y_ref, so_ref, shift_ref, s_scr, xs_scr) = refs
    c = pl.program_id(1)
    C2 = 2 * C
    chains = [(i, j) for i in range(CB) for j in range(RW_PAIRS)]
    nch = len(chains)
    sl = lambda j: slice(j * LANES, (j + 1) * LANES)

    @pl.when(c == 0)
    def _():
        for i in range(CB):
            xs_scr[i, 0:ROW_PAD, :] = jnp.zeros((ROW_PAD, RW_COLS), F32)
        if has_state:
            z = jnp.zeros((RW_DH, RW_DH), F32)
            for n, (i, j) in enumerate(chains):
                s_scr[n] = jnp.concatenate([jnp.concatenate([s0_ref[i, 2 * j], z], axis=1),
                                            jnp.concatenate([z, s0_ref[i, 2 * j + 1]], axis=1)], axis=0)
            for i in range(CB):
                xs_scr[i, ROW_PAD - 1:ROW_PAD, :] = prev_ref[i]
        else:
            s_scr[...] = jnp.zeros_like(s_scr)

    zs, last = [], []
    for i in range(CB):
        z = z_refs[i][...].astype(F32)
        xs_scr[i, ROW_PAD:ROW_PAD + C, :] = z
        zs.append(z + (xs_scr[i, ROW_PAD - 1:ROW_PAD - 1 + C, :] - z) * mu_ref[...])
        last.append(z[Lv - 1:Lv, :])
        xs_scr[i, ROW_PAD - 1:ROW_PAD, :] = last[i]
    stacked = _rwkv_token_vectors(jnp.concatenate(zs, axis=0), wa0_ref[...], w2a_ref[...], g2_ref[...], kkp_ref[...],
                                  ka_ref[...], rk_ref[...], bd_ref[...])
    tok = [tuple(t[i * C:(i + 1) * C, :] for t in stacked) for i in range(CB)]

    srow = lax.broadcasted_iota(jnp.int32, (LANES, LANES), 0)
    scol = lax.broadcasted_iota(jnp.int32, (LANES, LANES), 1)
    same_head = jnp.logical_or(jnp.logical_and(srow < RW_DH, scol < RW_DH),
                               jnp.logical_and(srow >= RW_DH, scol >= RW_DH))
    valid = lax.broadcasted_iota(jnp.int32, (C, LANES), 0) < Lv
    row = lax.broadcasted_iota(jnp.int32, (C2, C2), 0)
    col = lax.broadcasted_iota(jnp.int32, (C2, C2), 1)
    tt = row & (C - 1)
    ss = col & (C - 1)
    top = row < C
    bot = row >= C
    strict = ss < tt
    incl = ss <= tt
    keep0 = jnp.logical_or(jnp.logical_and(top, strict), jnp.logical_and(bot, incl))
    keep1 = jnp.logical_or(jnp.logical_and(top, incl), jnp.logical_and(bot, strict))
    tl = jnp.logical_and(top, col < C)
    br = jnp.logical_and(bot, col >= C)
    eye = (row == col).astype(F32)
    tri = lax.broadcasted_iota(jnp.int32, (C, C), 0) >= lax.broadcasted_iota(jnp.int32, (C, C), 1)
    left = lax.broadcasted_iota(jnp.int32, (C, LANES), 1) < RW_DH
    left2 = lax.broadcasted_iota(jnp.int32, (C2, LANES), 1) < RW_DH
    zeros = jnp.zeros((C, LANES), F32)

    def head_mean(x):
        lsum = jnp.sum(jnp.where(left, x, 0.0), axis=1, keepdims=True)
        rsum = jnp.sum(jnp.where(left, 0.0, x), axis=1, keepdims=True)
        return jnp.where(left, lsum, rsum) * (1.0 / RW_DH)

    ar, m0, m1, vs, bk_end, tots = [], [], [], [], [], []
    for i, j in chains:
        r_i, lw_i, k_i, v_i, kk_i, kka_i, _, _ = tok[i]
        lw = jnp.where(valid, lw_i[:, sl(j)], 0.0)
        cum = _sel_dot(tri, lw)
        tot = cum[C - 1:C, :]
        e_inv = jnp.exp(-cum)
        e_end = jnp.exp(tot - cum)
        kka = jnp.where(valid, kka_i[:, sl(j)], 0.0)
        k = jnp.where(valid, k_i[:, sl(j)], 0.0)
        a_t = -jnp.where(valid, kk_i[:, sl(j)], 0.0) * jnp.exp(cum - lw)
        r_t = r_i[:, sl(j)] * jnp.exp(cum)
        b_t = kka * e_inv
        k_t = k * e_inv
        ar_c = jnp.concatenate([a_t, r_t], axis=0)
        g0 = _dot_nt(jnp.where(left2, ar_c, 0.0), jnp.concatenate([b_t, k_t], axis=0))
        g1 = _dot_nt(jnp.where(left2, 0.0, jnp.concatenate([r_t, a_t], axis=0)), jnp.concatenate([k_t, b_t], axis=0))
        ar.append(ar_c.astype(BF16))
        m0.append(jnp.where(keep0, g0, 0.0))
        m1.append(jnp.where(keep1, g1, 0.0))
        vs.append(jnp.where(valid, v_i[:, sl(j)], 0.0))
        bk_end.append(jnp.concatenate([kka * e_end, k * e_end], axis=0).astype(BF16))
        tots.append(tot)

    s_prev = [s_scr[n] for n in range(nch)]
    w_as = [_dot_nt(ar[n], s_prev[n]) for n in range(nch)]
    x0 = [_dot(m0[n][0:C, :], jnp.concatenate([zeros, vs[n]], axis=0)) for n in range(nch)]
    x1 = [_dot(m1[n][C:, :], jnp.concatenate([vs[n], zeros], axis=0)) for n in range(nch)]

    p = [jnp.where(tl, m0[n], jnp.where(br, m1[n], 0.0)) for n in range(nch)]
    t_inv = [eye + p[n] for n in range(nch)]
    if Lv > 2:
        p = [_dot(p[n], p[n]) for n in range(nch)]
    step = 2
    while step < Lv:
        if 2 * step < Lv:
            both = [_dot(jnp.concatenate([t_inv[n], p[n]], axis=0), p[n]) for n in range(nch)]
            t_inv = [t_inv[n] + both[n][0:C2, :] for n in range(nch)]
            p = [both[n][C2:, :] for n in range(nch)]
        else:
            t_inv = [t_inv[n] + _dot(t_inv[n], p[n]) for n in range(nch)]
        step *= 2

    rhs = [w_as[n][0:C, :] + jnp.where(left, x0[n], x1[n]) for n in range(nch)]
    tu = [_dot(t_inv[n], jnp.concatenate([rhs[n], rhs[n]], axis=0)) for n in range(nch)]
    u = [jnp.where(left, tu[n][0:C, :], tu[n][C:, :]) for n in range(nch)]
    uv = [jnp.concatenate([u[n], vs[n]], axis=0).astype(BF16) for n in range(nch)]
    vu = [jnp.concatenate([vs[n], u[n]], axis=0).astype(BF16) for n in range(nch)]
    o0 = [_dot(m0[n][C:, :], uv[n]) for n in range(nch)]
    o1 = [_dot(m1[n][0:C, :], vu[n]) for n in range(nch)]
    s_new = [_dot_tn(uv[n], bk_end[n]) for n in range(nch)]
    for n in range(nch):
        s_scr[n] = jnp.where(same_head, s_prev[n] * jnp.exp(tots[n]) + s_new[n], 0.0)

    o = [w_as[n][C:, :] + jnp.where(left, o0[n], o1[n]) for n in range(nch)]
    oc = [o[n] - head_mean(o[n]) for n in range(nch)]
    var = [head_mean(oc[n] * oc[n]) for n in range(nch)]
    for n, (i, j) in enumerate(chains):
        yn = oc[n] * lax.rsqrt(var[n] + RW_GN_EPS) * gnw_ref[:, sl(j)] + gnb_ref[:, sl(j)]
        y_ref[i, :, sl(j)] = ((yn + tok[i][7][:, sl(j)]) * tok[i][6][:, sl(j)]).astype(BF16)

    @pl.when(c == nc - 1)
    def _():
        for n, (i, j) in enumerate(chains):
            tile = s_scr[n]
            so_ref[i, 2 * j] = tile[0:RW_DH, 0:RW_DH]
            so_ref[i, 2 * j + 1] = tile[RW_DH:, RW_DH:]
        for i in range(CB):
            shift_ref[i] = last[i]


def _rwkv(zrw, weights, states, layer, *, B, T, C, CB, Lv, row0):
    nc = T // C
    has_state = states is not None
    const2 = lambda b, c: (0, 0)
    wspecs = [pl.BlockSpec(w.shape, const2) for w in weights]
    z_specs = [pl.BlockSpec((C, RW_COLS), functools.partial(lambda b, c, i: (row0 + (b * CB + i) * nc + c, 0), i=i))
               for i in range(CB)]
    in_specs = z_specs + wspecs
    args = [zrw] * CB + list(weights)
    if has_state:
        state, shift = states
        in_specs += [pl.BlockSpec((None, CB, 1, RW_COLS), lambda b, c: (layer, b, 0, 0)),
                     pl.BlockSpec((None, CB, RW_HEADS, RW_DH, RW_DH), lambda b, c: (layer, b, 0, 0, 0))]
        args += [shift.reshape(DEPTH, B, 1, RW_COLS), state]
    y, s, sh = pl.pallas_call(
        functools.partial(_rwkv_chunk_kernel, C=C, CB=CB, Lv=Lv, nc=nc, has_state=has_state),
        grid=(B // CB, nc),
        in_specs=in_specs,
        out_specs=[pl.BlockSpec((CB, C, RW_W), lambda b, c: (b, c, 0)),
                   pl.BlockSpec((CB, RW_HEADS, RW_DH, RW_DH), lambda b, c: (b, 0, 0, 0)),
                   pl.BlockSpec((CB, 1, RW_COLS), lambda b, c: (b, 0, 0))],
        out_shape=[jax.ShapeDtypeStruct((B, T, RW_W), BF16),
                   jax.ShapeDtypeStruct((B, RW_HEADS, RW_DH, RW_DH), F32),
                   jax.ShapeDtypeStruct((B, 1, RW_COLS), F32)],
        scratch_shapes=[pltpu.VMEM((CB * RW_PAIRS, LANES, LANES), F32),
                        pltpu.VMEM((CB, ROW_PAD + C, RW_COLS), F32)],
        compiler_params=_cparams(("parallel", "arbitrary")),
        name="rwkv",
    )(*args)
    return y.reshape(B * T, RW_W), s, sh[:, 0]


def _merge_kernel(x_ref, zg_ref, yrp_ref, ymp_ref, ywp_ref, yrs_ref, yms_ref, yws_ref, wb_ref, wo_ref, o_ref, *, ntp):
    def run(branches):
        acc = None
        for n, y_ref in enumerate(branches):
            proj = jnp.dot(y_ref[...], wb_ref[n], preferred_element_type=F32)
            term = _sigmoid(zg_ref[:, n * D_MODEL:(n + 1) * D_MODEL].astype(F32)) * proj
            acc = term if acc is None else acc + term
        o_ref[...] = x_ref[...] + jnp.dot(acc.astype(BF16), wo_ref[...], preferred_element_type=F32)

    i = pl.program_id(0)

    @pl.when(i < ntp)
    def _():
        run((yrp_ref, ymp_ref, ywp_ref))

    @pl.when(i >= ntp)
    def _():
        run((yrs_ref, yms_ref, yws_ref))


def _merge(x, zgate, y_prompt, y_sample, w_branch, w_out, tm=256):
    n = x.shape[0]
    ntp = y_prompt[0].shape[0] // tm
    tokspec = lambda w: pl.BlockSpec((tm, w), lambda i: (i, 0))
    pspec = pl.BlockSpec((tm, RET_W), lambda i: (jnp.minimum(i, ntp - 1), 0))
    sspec = pl.BlockSpec((tm, RET_W), lambda i: (jnp.maximum(i - ntp, 0), 0))
    return pl.pallas_call(
        functools.partial(_merge_kernel, ntp=ntp),
        grid=(n // tm,),
        in_specs=[tokspec(D_MODEL), tokspec(N_BRANCH * D_MODEL), pspec, pspec, pspec, sspec, sspec, sspec,
                  pl.BlockSpec((N_BRANCH, RET_W, D_MODEL), lambda i: (0, 0, 0)),
                  pl.BlockSpec((D_MODEL, D_MODEL), lambda i: (0, 0))],
        out_specs=tokspec(D_MODEL),
        out_shape=jax.ShapeDtypeStruct((n, D_MODEL), F32),
        compiler_params=_cparams(("arbitrary",)),
        name="merge",
    )(x, zgate, *y_prompt, *y_sample, w_branch, w_out)


def _ffn_kernel(x_ref, g_ref, wg_ref, wu_ref, wd_ref, fin_ref, o_ref, h_scr, acc_scr, *, ne, final):
    e = pl.program_id(1)

    @pl.when(e == 0)
    def _():
        h_scr[...] = _rms(x_ref[...], g_ref[...]).astype(BF16)
        acc_scr[...] = jnp.zeros_like(acc_scr)

    h = h_scr[...]
    hg = jnp.dot(h, wg_ref[0], preferred_element_type=F32)
    hu = jnp.dot(h, wu_ref[0], preferred_element_type=F32)
    acc_scr[...] += jnp.dot((_silu(hg) * hu).astype(BF16), wd_ref[0], preferred_element_type=F32)

    @pl.when(e == ne - 1)
    def _():
        out = x_ref[...] + acc_scr[...]
        if final:
            out = _rms(out, fin_ref[...])
        o_ref[...] = out


def _ffn(x, g, wg, wu, wd, fin, final, tm=512):
    n = x.shape[0]
    ne, _, f = wg.shape
    return pl.pallas_call(
        functools.partial(_ffn_kernel, ne=ne, final=final),
        grid=(n // tm, ne),
        in_specs=[pl.BlockSpec((tm, D_MODEL), lambda i, e: (i, 0)),
                  pl.BlockSpec((1, D_MODEL), lambda i, e: (0, 0)),
                  pl.BlockSpec((1, D_MODEL, f), lambda i, e: (e, 0, 0)),
                  pl.BlockSpec((1, D_MODEL, f), lambda i, e: (e, 0, 0)),
                  pl.BlockSpec((1, f, D_MODEL), lambda i, e: (e, 0, 0)),
                  pl.BlockSpec((1, D_MODEL), lambda i, e: (0, 0))],
        out_specs=pl.BlockSpec((tm, D_MODEL), lambda i, e: (i, 0)),
        out_shape=jax.ShapeDtypeStruct((n, D_MODEL), F32),
        scratch_shapes=[pltpu.VMEM((tm, D_MODEL), BF16), pltpu.VMEM((tm, D_MODEL), F32)],
        compiler_params=_cparams(("parallel", "arbitrary")),
        name="ffn",
    )(x, g, wg, wu, wd, fin)


MOE_TM = 1024
MOE_SUB = LANES
MOE_CAP = 48
MOE_NSUB = MOE_TM // MOE_SUB


def _moe_kernel(x_ref, g_ref, router_ref, wg_ref, wu_ref, wd_ref, fin_ref, o_ref,
                h_scr, acc_scr, comb_scr, combt_scr, xe_scr, *, ne, final):
    e = pl.program_id(1)
    lane = lax.broadcasted_iota(jnp.int32, (MOE_TM, LANES), 1)

    @pl.when(e == 0)
    def _():
        h = _rms(x_ref[...], g_ref[...])
        h_scr[...] = h.astype(BF16)
        acc_scr[...] = jnp.zeros_like(acc_scr)
        h_hi, h_lo = _split(h)
        r_hi, r_lo = _split(router_ref[...])
        logits = (jnp.dot(h_hi, r_hi, preferred_element_type=F32) + jnp.dot(h_lo, r_hi, preferred_element_type=F32)
                  + jnp.dot(h_hi, r_lo, preferred_element_type=F32))
        logits = jnp.where(lane < ne, logits, NEG_BIG)
        m1 = jnp.max(logits, axis=1, keepdims=True)
        i1 = jnp.min(jnp.where(logits == m1, lane, LANES), axis=1, keepdims=True)
        rest = jnp.where(lane == i1, NEG_BIG, logits)
        m2 = jnp.max(rest, axis=1, keepdims=True)
        i2 = jnp.min(jnp.where(rest == m2, lane, LANES), axis=1, keepdims=True)
        e2 = jnp.exp(m2 - m1)
        p1 = 1.0 / (1.0 + e2)
        comb = jnp.where(lane == i1, p1, 0.0) + jnp.where(lane == i2, e2 * p1, 0.0)
        comb_scr[...] = comb
        for s in range(MOE_NSUB):
            combt_scr[s] = comb[s * MOE_SUB:(s + 1) * MOE_SUB, :].T

    sub = lambda s: slice(s * MOE_SUB, (s + 1) * MOE_SUB)
    slot = lambda s: slice(s * MOE_CAP, (s + 1) * MOE_CAP)
    r_i = lax.broadcasted_iota(jnp.int32, (MOE_SUB, MOE_SUB), 0)
    c_i = lax.broadcasted_iota(jnp.int32, (MOE_SUB, MOE_SUB), 1)
    before_row = (c_i < r_i).astype(F32).astype(BF16)
    before_col = (r_i < c_i).astype(F32).astype(BF16)
    slot_row = lax.broadcasted_iota(jnp.int32, (MOE_CAP, MOE_SUB), 0).astype(F32)
    slot_col = lax.broadcasted_iota(jnp.int32, (MOE_SUB, MOE_CAP), 1).astype(F32)

    w_col = jnp.sum(jnp.where(lane == e, comb_scr[...], 0.0), axis=1, keepdims=True)
    hit_col = [(w_col[sub(s), :] > 0.0).astype(F32) for s in range(MOE_NSUB)]
    hit_row = [(combt_scr[s, pl.ds(e, 1), :] > 0.0).astype(F32) for s in range(MOE_NSUB)]
    rank_col = [jnp.dot(before_row, jnp.broadcast_to(hit_col[s], (MOE_SUB, MOE_CAP)).astype(BF16),
                        preferred_element_type=F32) for s in range(MOE_NSUB)]
    rank_row = [jnp.dot(jnp.broadcast_to(hit_row[s], (SUBLANES, MOE_SUB)).astype(BF16), before_col,
                        preferred_element_type=F32)[0:1, :] for s in range(MOE_NSUB)]
    count = jnp.sum(hit_row[0], axis=1, keepdims=True)
    for s in range(1, MOE_NSUB):
        count = jnp.maximum(count, jnp.sum(hit_row[s], axis=1, keepdims=True))
    n_pass = lax.div(jnp.max(count).astype(jnp.int32) + (MOE_CAP - 1), MOE_CAP)

    def one_pass(p, carry):
        base = (p * MOE_CAP).astype(F32)
        for s in range(MOE_NSUB):
            pick = jnp.logical_and(rank_row[s] - base == slot_row, hit_row[s] > 0.0).astype(F32).astype(BF16)
            xe_scr[slot(s), :] = jnp.dot(pick, h_scr[sub(s), :], preferred_element_type=F32).astype(BF16)
        xe = xe_scr[...]
        hg = jnp.dot(xe, wg_ref[0], preferred_element_type=F32)
        hu = jnp.dot(xe, wu_ref[0], preferred_element_type=F32)
        y = jnp.dot((_silu(hg) * hu).astype(BF16), wd_ref[0], preferred_element_type=F32)
        for s in range(MOE_NSUB):
            put = jnp.logical_and(rank_col[s] - base == slot_col, hit_col[s] > 0.0).astype(F32).astype(BF16)
            hi, lo = _split(y[slot(s), :])
            back = jnp.dot(put, hi, preferred_element_type=F32) + jnp.dot(put, lo, preferred_element_type=F32)
            acc_scr[sub(s), :] += w_col[sub(s), :] * back
        return carry

    lax.fori_loop(0, n_pass, one_pass, 0)

    @pl.when(e == ne - 1)
    def _():
        out = x_ref[...] + acc_scr[...]
        if final:
            out = _rms(out, fin_ref[...])
        o_ref[...] = out


def _moe(x, g, wg, wu, wd, router, fin, final):
    n = x.shape[0]
    ne, _, f = wg.shape
    return pl.pallas_call(
        functools.partial(_moe_kernel, ne=ne, final=final),
        grid=(n // MOE_TM, ne),
        in_specs=[pl.BlockSpec((MOE_TM, D_MODEL), lambda i, e: (i, 0)),
                  pl.BlockSpec((1, D_MODEL), lambda i, e: (0, 0)),
                  pl.BlockSpec((D_MODEL, LANES), lambda i, e: (0, 0)),
                  pl.BlockSpec((1, D_MODEL, f), lambda i, e: (e, 0, 0)),
                  pl.BlockSpec((1, D_MODEL, f), lambda i, e: (e, 0, 0)),
                  pl.BlockSpec((1, f, D_MODEL), lambda i, e: (e, 0, 0)),
                  pl.BlockSpec((1, D_MODEL), lambda i, e: (0, 0))],
        out_specs=pl.BlockSpec((MOE_TM, D_MODEL), lambda i, e: (i, 0)),
        out_shape=jax.ShapeDtypeStruct((n, D_MODEL), F32),
        scratch_shapes=[pltpu.VMEM((MOE_TM, D_MODEL), BF16), pltpu.VMEM((MOE_TM, D_MODEL), F32),
                        pltpu.VMEM((MOE_TM, LANES), F32), pltpu.VMEM((MOE_NSUB, LANES, MOE_SUB), F32),
                        pltpu.VMEM((MOE_NSUB * MOE_CAP, D_MODEL), BF16)],
        compiler_params=_cparams(("parallel", "arbitrary")),
        name="moe",
    )(x, g, router, wg, wu, wd, fin)


def _row(v):
    return v.reshape(1, -1)


def kernel(x_prompt, x_sample, state_ret, state_mlstm_C, state_mlstm_n, state_mlstm_m, state_mlstm_conv, state_rwkv, state_rwkv_shift, norm_mix, w_in, ret_gn, ml_conv_w, ml_conv_b, ml_wq, ml_wk, ml_bi, ml_bf, ml_gn, ml_skip, rw_mu, rw_w0, rw_w2, rw_a0, rw_a2, rw_g2, rw_kk, rw_ka, rw_rk, rw_gn_w, rw_gn_b, w_branch, w_out, norm_ffn, ffn_w_gate, ffn_w_up, ffn_w_down, moe_router, moe_w_gate, moe_w_up, moe_w_down, final_norm):
    nc_p = SEQ // CHUNK
    xs = jnp.pad(x_sample, ((0, 0), (0, S_PAD - DEC_SEQ), (0, 0)))
    x = jnp.concatenate([x_prompt.reshape(N_PROMPT, D_MODEL), xs.reshape(N_SAMPLE, D_MODEL)], axis=0)

    pos_p = jnp.arange(SEQ, dtype=jnp.int32)
    pos_s = PAST_LEN + jnp.arange(S_PAD, dtype=jnp.int32)
    ret_tab_p = _retention_tables(CHUNK, CHUNK, pos_p)
    ret_tab_s = _retention_tables(S_PAD, DEC_SEQ, pos_s)
    head_of = jnp.arange(RW_W) // RW_DH
    bd64 = (head_of[:, None] == head_of[None, :]).astype(BF16)
    m0_all = jnp.pad(state_mlstm_m, ((0, 0), (0, 0), (0, LANES - ML_HEADS))).reshape(DEPTH, DEC_BATCH, 1, LANES)

    o_rq = 0
    o_mx = o_rq + RET_COLS
    o_mi = o_mx + 2 * ML_W
    o_mo = o_mi + 2 * ML_HEADS
    o_rw = o_mo + ML_W
    o_gate = o_rw + RW_COLS

    new_p, new_s = [], []
    for l in range(DEPTH):
        w = w_in[l]
        w_ret = w[:, o_rq:o_mx].astype(BF16)
        w_ml = jnp.concatenate([w[:, o_mx:o_mi], w[:, o_mo:o_rw], w[:, o_mi:o_mo],
                                jnp.zeros((D_MODEL, ML_GATE_W - 2 * ML_HEADS), F32)], axis=1).astype(BF16)
        w_rw = w[:, o_rw:o_gate].astype(BF16)
        w_gate = w[:, o_gate:].astype(BF16)
        g_mix = _row(norm_mix[l])
        zret, = _norm_matmul(x, g_mix, w_ret, [(RET_COLS, BF16)])
        zml, zml_gates = _norm_matmul(x, g_mix, w_ml, [(3 * ML_W, BF16), (ML_GATE_W, F32)])
        zrw, = _norm_matmul(x, g_mix, w_rw, [(RW_COLS, BF16)])
        zgate, = _norm_matmul(x, g_mix, w_gate, [(N_BRANCH * D_MODEL, BF16)])

        gn = _row(ret_gn[l])
        yr_p, ret_p = _retention(zret, ret_tab_p, gn, None, l, B=BATCH, nc=nc_p, L=CHUNK, nb=RET_NB_PROMPT,
                                 split=True, row0=0)
        yr_s, ret_s = _retention(zret, ret_tab_s, gn, state_ret, l, B=DEC_BATCH, nc=1, L=S_PAD, nb=RET_NB_SAMPLE,
                                 split=False, row0=N_PROMPT // (RET_NB_SAMPLE * S_PAD))

        gate_bias = jnp.concatenate([ml_bi[l], ml_bf[l], jnp.zeros((ML_GATE_W - 2 * ML_HEADS,), F32)])
        ml_weights = (ml_conv_w[l], _row(ml_conv_b[l]), ml_wq[l].astype(BF16), ml_wk[l].astype(BF16),
                      _row(gate_bias), _row(ml_gn[l]), _row(ml_skip[l]))
        ym_p, c_p, n_p, m_p, buf_p = _mlstm(zml, zml_gates, ml_weights, None, l, B=BATCH, nc=nc_p, L=CHUNK, Lv=CHUNK,
                                            nb=ML_NB_PROMPT, split=True, row0=0)
        ym_s, c_s, n_s, m_s, buf_s = _mlstm(zml, zml_gates, ml_weights,
                                            (state_mlstm_C, state_mlstm_n, m0_all, state_mlstm_conv),
                                            l, B=DEC_BATCH, nc=1, L=S_PAD, Lv=DEC_SEQ, nb=ML_NB_SAMPLE, split=False,
                                            row0=N_PROMPT // (ML_NB_SAMPLE * S_PAD))

        w2a = jnp.zeros((RW_LORA_W + RW_LORA_A, 2 * RW_W), F32)
        w2a = w2a.at[:RW_LORA_W, :RW_W].set(rw_w2[l]).at[RW_LORA_W:, RW_W:].set(rw_a2[l])
        rw_weights = (_row(rw_mu[l]), _row(jnp.concatenate([rw_w0[l], rw_a0[l]])), w2a.astype(BF16),
                      rw_g2[l].astype(BF16), _row(rw_kk[l]), _row(rw_ka[l]), _row(rw_rk[l]), bd64,
                      _row(rw_gn_w[l]), _row(rw_gn_b[l]))
        yw_p, rws_p, shift_p = _rwkv(zrw, rw_weights, None, l, B=BATCH, T=SEQ, C=RW_C, CB=RW_CB_PROMPT, Lv=RW_C,
                                     row0=0)
        yw_s, rws_s, shift_s = _rwkv(zrw, rw_weights, (state_rwkv, state_rwkv_shift), l, B=DEC_BATCH, T=S_PAD,
                                     C=S_PAD, CB=RW_CB_SAMPLE, Lv=DEC_SEQ, row0=N_PROMPT // S_PAD)

        x = _merge(x, zgate, (yr_p, ym_p, yw_p), (yr_s, ym_s, yw_s), w_branch[l].astype(BF16), w_out[l].astype(BF16))

        i = l // 2
        g_ffn = _row(norm_ffn[l])
        fin = _row(final_norm)
        final = l == DEPTH - 1
        if l % 2 == 0:
            halves = D_FF // D_FF_EXPERT
            wg = ffn_w_gate[i].reshape(D_MODEL, halves, D_FF_EXPERT).transpose(1, 0, 2).astype(BF16)
            wu = ffn_w_up[i].reshape(D_MODEL, halves, D_FF_EXPERT).transpose(1, 0, 2).astype(BF16)
            wd = ffn_w_down[i].reshape(halves, D_FF_EXPERT, D_MODEL).astype(BF16)
            x = _ffn(x, g_ffn, wg, wu, wd, fin, final)
        else:
            router = jnp.pad(moe_router[i], ((0, 0), (0, LANES - N_EXPERTS)))
            x = _moe(x, g_ffn, moe_w_gate[i].astype(BF16), moe_w_up[i].astype(BF16), moe_w_down[i].astype(BF16),
                     router, fin, final)

        new_p.append((ret_p, c_p, n_p, m_p[:, 0, :ML_HEADS], buf_p, rws_p, shift_p))
        new_s.append((ret_s, c_s, n_s, m_s[:, 0, :ML_HEADS], buf_s, rws_s, shift_s))

    y_prompt = x[:N_PROMPT].reshape(BATCH, SEQ, D_MODEL)
    y_sample = x[N_PROMPT:].reshape(DEC_BATCH, S_PAD, D_MODEL)[:, :DEC_SEQ]
    st_p = tuple(jnp.stack([st[j] for st in new_p]) for j in range(7))
    st_s = tuple(jnp.stack([st[j] for st in new_s]) for j in range(7))
    return (y_prompt, y_sample) + st_p + st_s
```

```python
import functools

import jax
import jax.numpy as jnp
from jax import lax
from jax.experimental import pallas as pl
from jax.experimental.pallas import tpu as pltpu

D_MODEL = 1024
BATCH = 8
SEQ = 2048
DEPTH = 2
DEC_BATCH = 128
DEC_SEQ = 4
PAST_LEN = 16384
RET_HEADS = 4
RET_DK = 64
RET_DV = 128
ML_HEADS = 4
ML_DH = 128
CONV_W = 4
RW_HEADS = 8
RW_DH = 64
RW_LORA_W = 64
RW_LORA_A = 64
RW_LORA_G = 128
RET_W = RET_HEADS * RET_DV
ML_W = ML_HEADS * ML_DH
RW_W = RW_HEADS * RW_DH
N_BRANCH = 3
RW_COLS = 3 * RW_W + RW_LORA_W + RW_LORA_A + RW_LORA_G
D_FF = 2816
N_EXPERTS = 8
D_FF_EXPERT = 1408
CHUNK = 128
NORM_EPS = 1e-6
GN_EPS = 1e-5
RW_GN_EPS = 64e-5
ROPE_BASE = 10000.0
RW_DECAY_SCALE = 0.606531

LANES = 128
SUBLANES = 8
S_PAD = 16
N_PROMPT = BATCH * SEQ
N_SAMPLE = DEC_BATCH * S_PAD
N_TOK = N_PROMPT + N_SAMPLE
ML_GATE_W = LANES
ML_COLS = 3 * ML_W + ML_GATE_W
RET_COLS = 2 * RET_HEADS * RET_DK + 2 * RET_W
NEG_BIG = -1e30
VMEM_LIMIT = 56 * 1024 * 1024

RW_C = 64
RW_PAIRS = RW_HEADS // 2
RW_CB_PROMPT = 4
RW_CB_SAMPLE = 4
RET_NB_PROMPT = 4
RET_NB_SAMPLE = 8
ML_NB_PROMPT = 1
ML_NB_SAMPLE = 4

F32 = jnp.float32
BF16 = jnp.bfloat16


def _cparams(sem):
    return pltpu.CompilerParams(dimension_semantics=sem, vmem_limit_bytes=VMEM_LIMIT)


def _sigmoid(x):
    return 1.0 / (1.0 + jnp.exp(-x))


def _silu(x):
    return x * _sigmoid(x)


def _rms(x, g):
    return x * lax.rsqrt(jnp.mean(x * x, axis=-1, keepdims=True) + NORM_EPS) * g


def _dot(a, b):
    return jnp.dot(a.astype(BF16), b.astype(BF16), preferred_element_type=F32)


def _dot_nt(a, b):
    return lax.dot_general(a.astype(BF16), b.astype(BF16), (((1,), (1,)), ((), ())), preferred_element_type=F32)


def _dot_tn(a, b):
    return lax.dot_general(a.astype(BF16), b.astype(BF16), (((0,), (0,)), ((), ())), preferred_element_type=F32)


def _split(x):
    hi = x.astype(BF16)
    return hi, (x - hi.astype(F32)).astype(BF16)


def _as_bf16_mask(m):
    return m if m.dtype == BF16 else m.astype(F32).astype(BF16)


def _sel_dot(m, x):
    hi, lo = _split(x)
    m = _as_bf16_mask(m)
    return jnp.dot(m, hi, preferred_element_type=F32) + jnp.dot(m, lo, preferred_element_type=F32)


def _dot_sel(x, m):
    hi, lo = _split(x)
    m = _as_bf16_mask(m)
    return jnp.dot(hi, m, preferred_element_type=F32) + jnp.dot(lo, m, preferred_element_type=F32)


def _sel_dot_nt(m, x):
    hi, lo = _split(x)
    m = _as_bf16_mask(m)
    dn = (((1,), (1,)), ((), ()))
    return (lax.dot_general(m, hi, dn, preferred_element_type=F32)
            + lax.dot_general(m, lo, dn, preferred_element_type=F32))


def _norm_matmul_kernel(x_ref, g_ref, w_ref, *o_refs, bounds):
    h = _rms(x_ref[...], g_ref[...])
    z = jnp.dot(h.astype(BF16), w_ref[...], preferred_element_type=F32)
    for o_ref, (lo, hi) in zip(o_refs, bounds):
        o_ref[...] = z[:, lo:hi].astype(o_ref.dtype)


def _norm_matmul(x, g, w, outs, tm=1024):
    n, d = x.shape
    c = w.shape[1]
    bounds, lo = [], 0
    for width, _ in outs:
        bounds.append((lo, lo + width))
        lo += width
    assert lo == c
    return pl.pallas_call(
        functools.partial(_norm_matmul_kernel, bounds=tuple(bounds)),
        grid=(n // tm,),
        in_specs=[pl.BlockSpec((tm, d), lambda i: (i, 0)),
                  pl.BlockSpec((1, d), lambda i: (0, 0)),
                  pl.BlockSpec((d, c), lambda i: (0, 0))],
        out_specs=[pl.BlockSpec((tm, width), lambda i: (i, 0)) for width, _ in outs],
        out_shape=[jax.ShapeDtypeStruct((n, width), dt) for width, dt in outs],
        compiler_params=_cparams(("parallel",)),
        name="norm_matmul",
    )(x, g, w)


def _seq_views(refs, nb, L, split):
    if split:
        return list(refs)
    return [refs[0].at[pl.ds(i * L, L)] for i in range(nb)]


def _retention_kernel(*refs, L, nc, nb, split, has_state):
    nv = nb if split else 1
    q_v, k_v, v_v, g_v = (_seq_views(refs[j * nv:(j + 1) * nv], nb, L, split) for j in range(4))
    refs = refs[4 * nv:]
    if has_state:
        cos_ref, sin_ref, di_ref, dq_ref, dk_ref, dc_ref, gn_ref, s0_ref, y_ref, so_ref, s_scr = refs
    else:
        cos_ref, sin_ref, di_ref, dq_ref, dk_ref, dc_ref, gn_ref, y_ref, so_ref, s_scr = refs
    y_v = [y_ref.at[i] for i in range(nb)]
    c = pl.program_id(1)

    @pl.when(c == 0)
    def _():
        if has_state:
            s_scr[...] = s0_ref[...]
        else:
            s_scr[...] = jnp.zeros_like(s_scr)

    qk_w = RET_HEADS * RET_DK
    half = RET_DK // 2
    lane = lax.broadcasted_iota(jnp.int32, (L, qk_w), 1)
    first_half = (lane % RET_DK) < half
    cos = cos_ref[...]
    sin = sin_ref[...]

    def rot(x):
        swapped = jnp.where(first_half, pltpu.roll(x, qk_w - half, 1), pltpu.roll(x, half, 1))
        return x * cos + swapped * sin

    q = [rot(q_v[i][...].astype(F32)) for i in range(nb)]
    k = [rot(k_v[i][...].astype(F32)) * (RET_DK ** -0.5) for i in range(nb)]
    dq = dq_ref[...]
    dk = dk_ref[...]
    dc = dc_ref[...]
    chains = [(i, h) for i in range(nb) for h in range(RET_HEADS)]
    kcols = lambda h: slice(h * RET_DK, (h + 1) * RET_DK)
    vcols = lambda h: slice(h * RET_DV, (h + 1) * RET_DV)

    qh = [q[i][:, kcols(h)].astype(BF16) for i, h in chains]
    kh = [k[i][:, kcols(h)] for i, h in chains]
    vh = [v_v[i][:, vcols(h)].astype(BF16) for i, h in chains]
    s = [_dot_nt(qh[n], kh[n]) * di_ref[h] for n, (i, h) in enumerate(chains)]
    s_prev = [s_scr[i, h] for i, h in chains]
    cross = [_dot(qh[n], s_prev[n]) for n in range(len(chains))]
    upd = [_dot_tn(kh[n] * dk[:, h:h + 1], vh[n]) for n, (i, h) in enumerate(chains)]
    inner = [_dot(s[n], vh[n]) for n in range(len(chains))]
    ones = jnp.ones((RET_DV, RET_DV), BF16)
    if L >= RET_DV:
        row_mean = lambda x: _dot_sel(x, ones) * (1.0 / RET_DV)
    else:
        row_mean = lambda x: jnp.mean(x, axis=-1, keepdims=True)
    o = []
    for n, (i, h) in enumerate(chains):
        s_scr[i, h] = s_prev[n] * dc[:, h:h + 1] + upd[n]
        o.append(inner[n] + cross[n] * dq[:, h:h + 1])
    oc = [o[n] - row_mean(o[n]) for n in range(len(chains))]
    var = [row_mean(oc[n] * oc[n]) for n in range(len(chains))]
    for n, (i, h) in enumerate(chains):
        yn = oc[n] * lax.rsqrt(var[n] + GN_EPS) * gn_ref[:, vcols(h)]
        y_v[i][:, vcols(h)] = (_silu(g_v[i][:, vcols(h)].astype(F32)) * yn).astype(BF16)

    @pl.when(c == nc - 1)
    def _():
        so_ref[...] = s_scr[...]


def _token_specs(cols, col_block, *, L, nb, nc, row0, split):
    if split:
        return [pl.BlockSpec((L, cols), functools.partial(
            lambda b, c, i: (row0 + (b * nb + i) * nc + c, col_block), i=i)) for i in range(nb)]
    assert nc == 1
    return [pl.BlockSpec((nb * L, cols), lambda b, c: (row0 + b, col_block))]


def _retention(zret, tables, gn, state, layer, *, B, nc, L, nb, split, row0):
    cos, sin, di, dq, dk, dc = tables
    has_state = state is not None
    qk_w = RET_HEADS * RET_DK
    nv = nb if split else 1
    tok = functools.partial(_token_specs, L=L, nb=nb, nc=nc, row0=row0, split=split)
    st = pl.BlockSpec((nb, RET_HEADS, RET_DK, RET_DV), lambda b, c: (b, 0, 0, 0))
    in_specs = (tok(qk_w, 0) + tok(qk_w, 1) + tok(RET_W, 1) + tok(RET_W, 2)
                + [pl.BlockSpec((L, qk_w), lambda b, c: (c, 0)),
                   pl.BlockSpec((L, qk_w), lambda b, c: (c, 0)),
                   pl.BlockSpec((RET_HEADS, L, L), lambda b, c: (0, 0, 0)),
                   pl.BlockSpec((L, RET_HEADS), lambda b, c: (0, 0)),
                   pl.BlockSpec((L, RET_HEADS), lambda b, c: (0, 0)),
                   pl.BlockSpec((1, RET_HEADS), lambda b, c: (0, 0)),
                   pl.BlockSpec((1, RET_W), lambda b, c: (0, 0))])
    args = [zret] * (4 * nv) + [cos, sin, di, dq, dk, dc, gn]
    if has_state:
        in_specs.append(pl.BlockSpec((None, nb, RET_HEADS, RET_DK, RET_DV), lambda b, c: (layer, b, 0, 0, 0)))
        args.append(state)
    y, s = pl.pallas_call(
        functools.partial(_retention_kernel, L=L, nc=nc, nb=nb, split=split, has_state=has_state),
        grid=(B // nb, nc),
        in_specs=in_specs,
        out_specs=[pl.BlockSpec((nb, L, RET_W), lambda b, c: (b, c, 0)), st],
        out_shape=[jax.ShapeDtypeStruct((B, nc * L, RET_W), BF16),
                   jax.ShapeDtypeStruct((B, RET_HEADS, RET_DK, RET_DV), F32)],
        scratch_shapes=[pltpu.VMEM((nb, RET_HEADS, RET_DK, RET_DV), F32)],
        compiler_params=_cparams(("parallel", "arbitrary")),
        name="retention",
    )(*args)
    return y.reshape(B * nc * L, RET_W), s


def _retention_tables(L, Lv, pos):
    half = RET_DK // 2
    inv = ROPE_BASE ** (-jnp.arange(half, dtype=F32) / half)
    ang = pos.astype(F32)[:, None] * inv[None, :]
    cos = jnp.tile(jnp.concatenate([jnp.cos(ang), jnp.cos(ang)], axis=1), (1, RET_HEADS))
    sin = jnp.tile(jnp.concatenate([-jnp.sin(ang), jnp.sin(ang)], axis=1), (1, RET_HEADS))
    log_gamma = jnp.log1p(-(2.0 ** (-5.0 - jnp.arange(RET_HEADS, dtype=F32))))
    idx = jnp.arange(L, dtype=F32)
    diff = idx[:, None] - idx[None, :]
    causal = diff >= 0
    di = jnp.where(causal[None], jnp.exp(jnp.where(causal, diff, 0.0)[None] * log_gamma[:, None, None]), 0.0)
    dq = jnp.exp((idx[:, None] + 1.0) * log_gamma[None, :])
    dk = jnp.where((idx < Lv)[:, None], jnp.exp((Lv - 1.0 - idx)[:, None] * log_gamma[None, :]), 0.0)
    dc = jnp.exp(Lv * log_gamma)[None, :]
    return cos, sin, di, dq, dk, dc


ROW_PAD = SUBLANES


def _mlstm_kernel(*refs, L, Lv, nc, nb, split, has_state):
    nv = nb if split else 1
    mx_v, mv_v, mo_v, gz_v = (_seq_views(refs[j * nv:(j + 1) * nv], nb, L, split) for j in range(4))
    refs = refs[4 * nv:]
    if has_state:
        (cw_ref, cb_ref, wq_ref, wk_ref, gb_ref, gn_ref, skip_ref, c0_ref, n0_ref, m0_ref, buf0_ref,
         y_ref, co_ref, no_ref, mout_ref, bufo_ref, xp_scr, c_scr, n_scr, m_scr) = refs
    else:
        (cw_ref, cb_ref, wq_ref, wk_ref, gb_ref, gn_ref, skip_ref,
         y_ref, co_ref, no_ref, mout_ref, bufo_ref, xp_scr, c_scr, n_scr, m_scr) = refs
    y_v = [y_ref.at[i] for i in range(nb)]
    c = pl.program_id(1)
    tail = CONV_W - 1

    @pl.when(c == 0)
    def _():
        for i in range(nb):
            xp_scr[i, 0:ROW_PAD, :] = jnp.zeros((ROW_PAD, ML_W), F32)
        if has_state:
            c_scr[...] = c0_ref[...]
            n_scr[...] = n0_ref[...]
            m_scr[...] = m0_ref[...]
            for i in range(nb):
                xp_scr[i, ROW_PAD - tail:ROW_PAD, :] = buf0_ref[i]
        else:
            c_scr[...] = jnp.zeros_like(c_scr)
            n_scr[...] = jnp.zeros_like(n_scr)
            m_scr[...] = jnp.zeros_like(m_scr)

    cols = lambda h: slice(h * ML_DH, (h + 1) * ML_DH)
    lane = lax.broadcasted_iota(jnp.int32, (L, ML_GATE_W), 1)
    rowi = lax.broadcasted_iota(jnp.int32, (L, ML_GATE_W), 0)
    tri = lax.broadcasted_iota(jnp.int32, (L, L), 0) >= lax.broadcasted_iota(jnp.int32, (L, L), 1)
    eye = (lax.broadcasted_iota(jnp.int32, (SUBLANES, ML_GATE_W), 0)
           == lax.broadcasted_iota(jnp.int32, (SUBLANES, ML_GATE_W), 1))

    xc, new_tail, gc, fc, g_rows, f_rows = [], [], [], [], [], []
    for i in range(nb):
        xp_scr[i, ROW_PAD:ROW_PAD + L, :] = mx_v[i][...].astype(F32)
        acc = cb_ref[...]
        for j in range(CONV_W):
            acc = acc + cw_ref[j:j + 1, :] * xp_scr[i, ROW_PAD - tail + j:ROW_PAD - tail + j + L, :]
        xc.append(_silu(acc))
        new_tail.append(xp_scr[i, ROW_PAD + Lv - tail:ROW_PAD + Lv, :])
        xp_scr[i, ROW_PAD - tail:ROW_PAD, :] = new_tail[i]
        gz = gz_v[i][...] + gb_ref[...]
        logsig = jnp.minimum(gz, 0.0) - jnp.log1p(jnp.exp(-jnp.abs(gz)))
        g = jnp.where(lane < ML_HEADS, gz, logsig)
        if Lv < L:
            g = jnp.where(rowi < Lv, g, jnp.where(lane < ML_HEADS, NEG_BIG, 0.0))
        gc.append(g)
        fc.append(_sel_dot(tri, g))
        g_rows.append(_sel_dot_nt(eye, g))
        f_rows.append(_sel_dot_nt(eye, fc[i]))

    chains = [(i, h) for i in range(nb) for h in range(ML_HEADS)]
    nch = len(chains)
    xh = [xc[i][:, cols(h)] for i, h in chains]
    xh_b = [x.astype(BF16) for x in xh]
    q = [jnp.dot(xh_b[n], wq_ref[h], preferred_element_type=F32) for n, (i, h) in enumerate(chains)]
    k = [jnp.dot(xh_b[n], wk_ref[h], preferred_element_type=F32) * (ML_DH ** -0.5) for n, (i, h) in enumerate(chains)]
    q_b = [x.astype(BF16) for x in q]
    v_b = [mv_v[i][:, cols(h)].astype(BF16) for i, h in chains]
    qk = [_dot_nt(q_b[n], k[n]) for n in range(nch)]
    c_prev = [c_scr[i, h] for i, h in chains]
    qc = [_dot(q_b[n], c_prev[n]) for n in range(nch)]

    s, a_inter, m_t, kw, a_old, m_new = [], [], [], [], [], []
    for n, (i, h) in enumerate(chains):
        f_col = fc[i][:, ML_HEADS + h:ML_HEADS + h + 1]
        i_col = gc[i][:, h:h + 1]
        f_row = f_rows[i][ML_HEADS + h:ML_HEADS + h + 1, :]
        i_row = g_rows[i][h:h + 1, :]
        m_prev = m_scr[i, 0:1, h:h + 1]
        dlog = jnp.where(tri, f_col - f_row + i_row, NEG_BIG)
        inter = f_col + m_prev
        m_t.append(jnp.maximum(jnp.max(dlog, axis=1, keepdims=True), inter))
        s.append(qk[n] * jnp.exp(dlog - m_t[n]))
        a_inter.append(jnp.exp(inter - m_t[n]))
        f_last = fc[i][Lv - 1:Lv, ML_HEADS + h:ML_HEADS + h + 1]
        m_new.append(jnp.maximum(f_last + m_prev, jnp.max(f_last - f_row + i_row, axis=1, keepdims=True)))
        kw.append(k[n] * jnp.exp(f_last - f_col + i_col - m_new[n]))
        a_old.append(jnp.exp(f_last + m_prev - m_new[n]))

    sv = [_dot(s[n], v_b[n]) for n in range(nch)]
    ktv = [_dot_tn(kw[n], v_b[n]) for n in range(nch)]
    hh = []
    for n, (i, h) in enumerate(chains):
        n_prev = n_scr[i, h:h + 1, :]
        num = sv[n] + a_inter[n] * qc[n]
        den = jnp.sum(s[n], axis=1, keepdims=True) + a_inter[n] * jnp.sum(q[n] * n_prev, axis=1, keepdims=True)
        hh.append(num * (1.0 / jnp.maximum(jnp.abs(den), jnp.exp(-m_t[n]))))
        c_scr[i, h] = a_old[n] * c_prev[n] + ktv[n]
        n_scr[i, h:h + 1, :] = a_old[n] * n_prev + jnp.sum(kw[n], axis=0, keepdims=True)
        m_scr[i, 0:1, h:h + 1] = m_new[n]
    ones = jnp.ones((ML_DH, ML_DH), BF16)
    hc = [hh[n] - _dot_sel(hh[n], ones) * (1.0 / ML_DH) for n in range(nch)]
    var = [_dot_sel(hc[n] * hc[n], ones) * (1.0 / ML_DH) for n in range(nch)]
    for n, (i, h) in enumerate(chains):
        hn = hc[n] * lax.rsqrt(var[n] + GN_EPS) * gn_ref[:, cols(h)]
        gate = _sigmoid(mo_v[i][:, cols(h)].astype(F32))
        y_v[i][:, cols(h)] = (gate * (hn + skip_ref[:, cols(h)] * xh[n])).astype(BF16)

    @pl.when(c == nc - 1)
    def _():
        co_ref[...] = c_scr[...]
        no_ref[...] = n_scr[...]
        mout_ref[...] = m_scr[...]
        for i in range(nb):
            bufo_ref[i] = new_tail[i]


def _mlstm(zml, zml_gates, weights, state, layer, *, B, nc, L, Lv, nb, split, row0):
    cw, cb, wq, wk, gb, gn, skip = weights
    has_state = state is not None
    nv = nb if split else 1
    const2 = lambda b, c: (0, 0)
    tok = functools.partial(_token_specs, L=L, nb=nb, nc=nc, row0=row0, split=split)
    in_specs = (tok(ML_W, 0) + tok(ML_W, 1) + tok(ML_W, 2) + tok(ML_GATE_W, 0)
                + [pl.BlockSpec((CONV_W, ML_W), const2),
                   pl.BlockSpec((1, ML_W), const2),
                   pl.BlockSpec((ML_HEADS, ML_DH, ML_DH), lambda b, c: (0, 0, 0)),
                   pl.BlockSpec((ML_HEADS, ML_DH, ML_DH), lambda b, c: (0, 0, 0)),
                   pl.BlockSpec((1, ML_GATE_W), const2),
                   pl.BlockSpec((1, ML_W), const2),
                   pl.BlockSpec((1, ML_W), const2)])
    args = [zml] * (3 * nv) + [zml_gates] * nv + [cw, cb, wq, wk, gb, gn, skip]
    st_specs = [pl.BlockSpec((nb, ML_HEADS, ML_DH, ML_DH), lambda b, c: (b, 0, 0, 0)),
                pl.BlockSpec((nb, ML_HEADS, ML_DH), lambda b, c: (b, 0, 0)),
                pl.BlockSpec((nb, 1, LANES), lambda b, c: (b, 0, 0)),
                pl.BlockSpec((nb, CONV_W - 1, ML_W), lambda b, c: (b, 0, 0))]
    if has_state:
        in_specs += [pl.BlockSpec((None, nb, ML_HEADS, ML_DH, ML_DH), lambda b, c: (layer, b, 0, 0, 0)),
                     pl.BlockSpec((None, nb, ML_HEADS, ML_DH), lambda b, c: (layer, b, 0, 0)),
                     pl.BlockSpec((None, nb, 1, LANES), lambda b, c: (layer, b, 0, 0)),
                     pl.BlockSpec((None, nb, CONV_W - 1, ML_W), lambda b, c: (layer, b, 0, 0))]
        args += list(state)
    y, *new_state = pl.pallas_call(
        functools.partial(_mlstm_kernel, L=L, Lv=Lv, nc=nc, nb=nb, split=split, has_state=has_state),
        grid=(B // nb, nc),
        in_specs=in_specs,
        out_specs=[pl.BlockSpec((nb, L, ML_W), lambda b, c: (b, c, 0))] + st_specs,
        out_shape=[jax.ShapeDtypeStruct((B, nc * L, ML_W), BF16),
                   jax.ShapeDtypeStruct((B, ML_HEADS, ML_DH, ML_DH), F32),
                   jax.ShapeDtypeStruct((B, ML_HEADS, ML_DH), F32),
                   jax.ShapeDtypeStruct((B, 1, LANES), F32),
                   jax.ShapeDtypeStruct((B, CONV_W - 1, ML_W), F32)],
        scratch_shapes=[pltpu.VMEM((nb, ROW_PAD + L, ML_W), F32),
                        pltpu.VMEM((nb, ML_HEADS, ML_DH, ML_DH), F32),
                        pltpu.VMEM((nb, ML_HEADS, ML_DH), F32),
                        pltpu.VMEM((nb, 1, LANES), F32)],
        compiler_params=_cparams(("parallel", "arbitrary")),
        name="mlstm",
    )(*args)
    return (y.reshape(B * nc * L, ML_W), *new_state)


def _rwkv_token_vectors(zs, wa0, w2a, g2, kkp, ka, rk, bd):
    r = zs[:, 0:RW_W]
    kr = zs[:, RW_W:2 * RW_W]
    vr = zs[:, 2 * RW_W:3 * RW_W]
    wa = zs[:, 3 * RW_W:3 * RW_W + RW_LORA_W + RW_LORA_A]
    gl = zs[:, 3 * RW_W + RW_LORA_W + RW_LORA_A:]
    lane = lax.broadcasted_iota(jnp.int32, wa.shape, 1)
    lora = jnp.dot(jnp.where(lane < RW_LORA_W, jnp.tanh(wa), wa).astype(BF16), w2a, preferred_element_type=F32) + wa0
    g = jnp.dot(_sigmoid(gl).astype(BF16), g2, preferred_element_type=F32)
    kk = kr * kkp
    kk = kk / jnp.maximum(jnp.sqrt(_dot_sel(kk * kk, bd)), 1e-12)
    a = _sigmoid(lora[:, RW_W:])
    k2 = kr * (1.0 + (a - 1.0) * ka)
    bonus = _dot_sel(r * k2 * rk, bd) * vr
    return r, -RW_DECAY_SCALE * _sigmoid(lora[:, 0:RW_W]), k2, vr, kk, kk * a, g, bonus


def _rwkv_chunk_kernel(*refs, C, CB, Lv, nc, has_state):
    z_refs = refs[:CB]
    refs = refs[CB:]
    if has_state:
        (mu_ref, wa0_ref, w2a_ref, g2_ref, kkp_ref, ka_ref, rk_ref, bd_ref, gnw_ref, gnb_ref, prev_ref, s0_ref,
         y_ref, so_ref, shift_ref, s_scr, xs_scr) = refs
    else:
        (mu_ref, wa0_ref, w2a_ref, g2_ref, kkp_ref, ka_ref, rk_ref, bd_ref, gnw_ref, gnb_ref,
         y_ref, so_ref, shift_ref, s_scr, xs_scr) = refs
    c = pl.program_id(1)
    C2 = 2 * C
    chains = [(i, j) for i in range(CB) for j in range(RW_PAIRS)]
    nch = len(chains)
    sl = lambda j: slice(j * LANES, (j + 1) * LANES)

    @pl.when(c == 0)
    def _():
        for i in range(CB):
            xs_scr[i, 0:ROW_PAD, :] = jnp.zeros((ROW_PAD, RW_COLS), F32)
        if has_state:
            z = jnp.zeros((RW_DH, RW_DH), F32)
            for n, (i, j) in enumerate(chains):
                s_scr[n] = jnp.concatenate([jnp.concatenate([s0_ref[i, 2 * j], z], axis=1),
                                            jnp.concatenate([z, s0_ref[i, 2 * j + 1]], axis=1)], axis=0)
            for i in range(CB):
                xs_scr[i, ROW_PAD - 1:ROW_PAD, :] = prev_ref[i]
        else:
            s_scr[...] = jnp.zeros_like(s_scr)

    zs, last = [], []
    for i in range(CB):
        z = z_refs[i][...].astype(F32)
        xs_scr[i, ROW_PAD:ROW_PAD + C, :] = z
        zs.append(z + (xs_scr[i, ROW_PAD - 1:ROW_PAD - 1 + C, :] - z) * mu_ref[...])
        last.append(z[Lv - 1:Lv, :])
        xs_scr[i, ROW_PAD - 1:ROW_PAD, :] = last[i]
    stacked = _rwkv_token_vectors(jnp.concatenate(zs, axis=0), wa0_ref[...], w2a_ref[...], g2_ref[...], kkp_ref[...],
                                  ka_ref[...], rk_ref[...], bd_ref[...])
    tok = [tuple(t[i * C:(i + 1) * C, :] for t in stacked) for i in range(CB)]

    srow = lax.broadcasted_iota(jnp.int32, (LANES, LANES), 0)
    scol = lax.broadcasted_iota(jnp.int32, (LANES, LANES), 1)
    same_head = jnp.logical_or(jnp.logical_and(srow < RW_DH, scol < RW_DH),
                               jnp.logical_and(srow >= RW_DH, scol >= RW_DH))
    valid = lax.broadcasted_iota(jnp.int32, (C, LANES), 0) < Lv
    row = lax.broadcasted_iota(jnp.int32, (C2, C2), 0)
    col = lax.broadcasted_iota(jnp.int32, (C2, C2), 1)
    tt = row & (C - 1)
    ss = col & (C - 1)
    top = row < C
    bot = row >= C
    strict = ss < tt
    incl = ss <= tt
    keep0 = jnp.logical_or(jnp.logical_and(top, strict), jnp.logical_and(bot, incl))
    keep1 = jnp.logical_or(jnp.logical_and(top, incl), jnp.logical_and(bot, strict))
    tl = jnp.logical_and(top, col < C)
    br = jnp.logical_and(bot, col >= C)
    eye = (row == col).astype(F32)
    tri = lax.broadcasted_iota(jnp.int32, (C, C), 0) >= lax.broadcasted_iota(jnp.int32, (C, C), 1)
    left = lax.broadcasted_iota(jnp.int32, (C, LANES), 1) < RW_DH
    left2 = lax.broadcasted_iota(jnp.int32, (C2, LANES), 1) < RW_DH
    zeros = jnp.zeros((C, LANES), F32)

    def head_mean(x):
        lsum = jnp.sum(jnp.where(left, x, 0.0), axis=1, keepdims=True)
        rsum = jnp.sum(jnp.where(left, 0.0, x), axis=1, keepdims=True)
        return jnp.where(left, lsum, rsum) * (1.0 / RW_DH)

    ar, m0, m1, vs, bk_end, tots = [], [], [], [], [], []
    for i, j in chains:
        r_i, lw_i, k_i, v_i, kk_i, kka_i, _, _ = tok[i]
        lw = jnp.where(valid, lw_i[:, sl(j)], 0.0)
        cum = _sel_dot(tri, lw)
        tot = cum[C - 1:C, :]
        e_inv = jnp.exp(-cum)
        e_end = jnp.exp(tot - cum)
        kka = jnp.where(valid, kka_i[:, sl(j)], 0.0)
        k = jnp.where(valid, k_i[:, sl(j)], 0.0)
        a_t = -jnp.where(valid, kk_i[:, sl(j)], 0.0) * jnp.exp(cum - lw)
        r_t = r_i[:, sl(j)] * jnp.exp(cum)
        b_t = kka * e_inv
        k_t = k * e_inv
        ar_c = jnp.concatenate([a_t, r_t], axis=0)
        g0 = _dot_nt(jnp.where(left2, ar_c, 0.0), jnp.concatenate([b_t, k_t], axis=0))
        g1 = _dot_nt(jnp.where(left2, 0.0, jnp.concatenate([r_t, a_t], axis=0)), jnp.concatenate([k_t, b_t], axis=0))
        ar.append(ar_c.astype(BF16))
        m0.append(jnp.where(keep0, g0, 0.0))
        m1.append(jnp.where(keep1, g1, 0.0))
        vs.append(jnp.where(valid, v_i[:, sl(j)], 0.0))
        bk_end.append(jnp.concatenate([kka * e_end, k * e_end], axis=0).astype(BF16))
        tots.append(tot)

    s_prev = [s_scr[n] for n in range(nch)]
    w_as = [_dot_nt(ar[n], s_prev[n]) for n in range(nch)]
    x0 = [_dot(m0[n][0:C, :], jnp.concatenate([zeros, vs[n]], axis=0)) for n in range(nch)]
    x1 = [_dot(m1[n][C:, :], jnp.concatenate([vs[n], zeros], axis=0)) for n in range(nch)]

    p = [jnp.where(tl, m0[n], jnp.where(br, m1[n], 0.0)) for n in range(nch)]
    t_inv = [eye + p[n] for n in range(nch)]
    if Lv > 2:
        p = [_dot(p[n], p[n]) for n in range(nch)]
    step = 2
    while step < Lv:
        if 2 * step < Lv:
            both = [_dot(jnp.concatenate([t_inv[n], p[n]], axis=0), p[n]) for n in range(nch)]
            t_inv = [t_inv[n] + both[n][0:C2, :] for n in range(nch)]
            p = [both[n][C2:, :] for n in range(nch)]
        else:
            t_inv = [t_inv[n] + _dot(t_inv[n], p[n]) for n in range(nch)]
        step *= 2

    rhs = [w_as[n][0:C, :] + jnp.where(left, x0[n], x1[n]) for n in range(nch)]
    tu = [_dot(t_inv[n], jnp.concatenate([rhs[n], rhs[n]], axis=0)) for n in range(nch)]
    u = [jnp.where(left, tu[n][0:C, :], tu[n][C:, :]) for n in range(nch)]
    uv = [jnp.concatenate([u[n], vs[n]], axis=0).astype(BF16) for n in range(nch)]
    vu = [jnp.concatenate([vs[n], u[n]], axis=0).astype(BF16) for n in range(nch)]
    o0 = [_dot(m0[n][C:, :], uv[n]) for n in range(nch)]
    o1 = [_dot(m1[n][0:C, :], vu[n]) for n in range(nch)]
    s_new = [_dot_tn(uv[n], bk_end[n]) for n in range(nch)]
    for n in range(nch):
        s_scr[n] = jnp.where(same_head, s_prev[n] * jnp.exp(tots[n]) + s_new[n], 0.0)

    o = [w_as[n][C:, :] + jnp.where(left, o0[n], o1[n]) for n in range(nch)]
    oc = [o[n] - head_mean(o[n]) for n in range(nch)]
    var = [head_mean(oc[n] * oc[n]) for n in range(nch)]
    for n, (i, j) in enumerate(chains):
        yn = oc[n] * lax.rsqrt(var[n] + RW_GN_EPS) * gnw_ref[:, sl(j)] + gnb_ref[:, sl(j)]
        y_ref[i, :, sl(j)] = ((yn + tok[i][7][:, sl(j)]) * tok[i][6][:, sl(j)]).astype(BF16)

    @pl.when(c == nc - 1)
    def _():
        for n, (i, j) in enumerate(chains):
            tile = s_scr[n]
            so_ref[i, 2 * j] = tile[0:RW_DH, 0:RW_DH]
            so_ref[i, 2 * j + 1] = tile[RW_DH:, RW_DH:]
        for i in range(CB):
            shift_ref[i] = last[i]


def _rwkv(zrw, weights, states, layer, *, B, T, C, CB, Lv, row0):
    nc = T // C
    has_state = states is not None
    const2 = lambda b, c: (0, 0)
    wspecs = [pl.BlockSpec(w.shape, const2) for w in weights]
    z_specs = [pl.BlockSpec((C, RW_COLS), functools.partial(lambda b, c, i: (row0 + (b * CB + i) * nc + c, 0), i=i))
               for i in range(CB)]
    in_specs = z_specs + wspecs
    args = [zrw] * CB + list(weights)
    if has_state:
        state, shift = states
        in_specs += [pl.BlockSpec((None, CB, 1, RW_COLS), lambda b, c: (layer, b, 0, 0)),
                     pl.BlockSpec((None, CB, RW_HEADS, RW_DH, RW_DH), lambda b, c: (layer, b, 0, 0, 0))]
        args += [shift.reshape(DEPTH, B, 1, RW_COLS), state]
    y, s, sh = pl.pallas_call(
        functools.partial(_rwkv_chunk_kernel, C=C, CB=CB, Lv=Lv, nc=nc, has_state=has_state),
        grid=(B // CB, nc),
        in_specs=in_specs,
        out_specs=[pl.BlockSpec((CB, C, RW_W), lambda b, c: (b, c, 0)),
                   pl.BlockSpec((CB, RW_HEADS, RW_DH, RW_DH), lambda b, c: (b, 0, 0, 0)),
                   pl.BlockSpec((CB, 1, RW_COLS), lambda b, c: (b, 0, 0))],
        out_shape=[jax.ShapeDtypeStruct((B, T, RW_W), BF16),
                   jax.ShapeDtypeStruct((B, RW_HEADS, RW_DH, RW_DH), F32),
                   jax.ShapeDtypeStruct((B, 1, RW_COLS), F32)],
        scratch_shapes=[pltpu.VMEM((CB * RW_PAIRS, LANES, LANES), F32),
                        pltpu.VMEM((CB, ROW_PAD + C, RW_COLS), F32)],
        compiler_params=_cparams(("parallel", "arbitrary")),
        name="rwkv",
    )(*args)
    return y.reshape(B * T, RW_W), s, sh[:, 0]


def _merge_kernel(x_ref, zg_ref, yrp_ref, ymp_ref, ywp_ref, yrs_ref, yms_ref, yws_ref, wb_ref, wo_ref, o_ref, *, ntp):
    def run(branches):
        acc = None
        for n, y_ref in enumerate(branches):
            proj = jnp.dot(y_ref[...], wb_ref[n], preferred_element_type=F32)
            term = _sigmoid(zg_ref[:, n * D_MODEL:(n + 1) * D_MODEL].astype(F32)) * proj
            acc = term if acc is None else acc + term
        o_ref[...] = x_ref[...] + jnp.dot(acc.astype(BF16), wo_ref[...], preferred_element_type=F32)

    i = pl.program_id(0)

    @pl.when(i < ntp)
    def _():
        run((yrp_ref, ymp_ref, ywp_ref))

    @pl.when(i >= ntp)
    def _():
        run((yrs_ref, yms_ref, yws_ref))


def _merge(x, zgate, y_prompt, y_sample, w_branch, w_out, tm=512):
    n = x.shape[0]
    ntp = y_prompt[0].shape[0] // tm
    tokspec = lambda w: pl.BlockSpec((tm, w), lambda i: (i, 0))
    pspec = pl.BlockSpec((tm, RET_W), lambda i: (jnp.minimum(i, ntp - 1), 0))
    sspec = pl.BlockSpec((tm, RET_W), lambda i: (jnp.maximum(i - ntp, 0), 0))
    return pl.pallas_call(
        functools.partial(_merge_kernel, ntp=ntp),
        grid=(n // tm,),
        in_specs=[tokspec(D_MODEL), tokspec(N_BRANCH * D_MODEL), pspec, pspec, pspec, sspec, sspec, sspec,
                  pl.BlockSpec((N_BRANCH, RET_W, D_MODEL), lambda i: (0, 0, 0)),
                  pl.BlockSpec((D_MODEL, D_MODEL), lambda i: (0, 0))],
        out_specs=tokspec(D_MODEL),
        out_shape=jax.ShapeDtypeStruct((n, D_MODEL), F32),
        compiler_params=_cparams(("arbitrary",)),
        name="merge",
    )(x, zgate, *y_prompt, *y_sample, w_branch, w_out)


def _ffn_kernel(x_ref, g_ref, wg_ref, wu_ref, wd_ref, fin_ref, o_ref, h_scr, acc_scr, *, ne, final):
    e = pl.program_id(1)

    @pl.when(e == 0)
    def _():
        h_scr[...] = _rms(x_ref[...], g_ref[...]).astype(BF16)
        acc_scr[...] = jnp.zeros_like(acc_scr)

    h = h_scr[...]
    hg = jnp.dot(h, wg_ref[0], preferred_element_type=F32)
    hu = jnp.dot(h, wu_ref[0], preferred_element_type=F32)
    acc_scr[...] += jnp.dot((_silu(hg) * hu).astype(BF16), wd_ref[0], preferred_element_type=F32)

    @pl.when(e == ne - 1)
    def _():
        out = x_ref[...] + acc_scr[...]
        if final:
            out = _rms(out, fin_ref[...])
        o_ref[...] = out


def _ffn(x, g, wg, wu, wd, fin, final, tm=512):
    n = x.shape[0]
    ne, _, f = wg.shape
    return pl.pallas_call(
        functools.partial(_ffn_kernel, ne=ne, final=final),
        grid=(n // tm, ne),
        in_specs=[pl.BlockSpec((tm, D_MODEL), lambda i, e: (i, 0)),
                  pl.BlockSpec((1, D_MODEL), lambda i, e: (0, 0)),
                  pl.BlockSpec((1, D_MODEL, f), lambda i, e: (e, 0, 0)),
                  pl.BlockSpec((1, D_MODEL, f), lambda i, e: (e, 0, 0)),
                  pl.BlockSpec((1, f, D_MODEL), lambda i, e: (e, 0, 0)),
                  pl.BlockSpec((1, D_MODEL), lambda i, e: (0, 0))],
        out_specs=pl.BlockSpec((tm, D_MODEL), lambda i, e: (i, 0)),
        out_shape=jax.ShapeDtypeStruct((n, D_MODEL), F32),
        scratch_shapes=[pltpu.VMEM((tm, D_MODEL), BF16), pltpu.VMEM((tm, D_MODEL), F32)],
        compiler_params=_cparams(("parallel", "arbitrary")),
        name="ffn",
    )(x, g, wg, wu, wd, fin)


MOE_TM = 1024
MOE_SUB = LANES
MOE_CAP = 48
MOE_NSUB = MOE_TM // MOE_SUB


def _moe_kernel(x_ref, g_ref, router_ref, wg_ref, wu_ref, wd_ref, fin_ref, o_ref,
                h_scr, acc_scr, comb_scr, combt_scr, xe_scr, *, ne, final):
    e = pl.program_id(1)
    lane = lax.broadcasted_iota(jnp.int32, (MOE_TM, LANES), 1)

    @pl.when(e == 0)
    def _():
        h = _rms(x_ref[...], g_ref[...])
        h_scr[...] = h.astype(BF16)
        acc_scr[...] = jnp.zeros_like(acc_scr)
        h_hi, h_lo = _split(h)
        r_hi, r_lo = _split(router_ref[...])
        logits = (jnp.dot(h_hi, r_hi, preferred_element_type=F32) + jnp.dot(h_lo, r_hi, preferred_element_type=F32)
                  + jnp.dot(h_hi, r_lo, preferred_element_type=F32))
        logits = jnp.where(lane < ne, logits, NEG_BIG)
        m1 = jnp.max(logits, axis=1, keepdims=True)
        i1 = jnp.min(jnp.where(logits == m1, lane, LANES), axis=1, keepdims=True)
        rest = jnp.where(lane == i1, NEG_BIG, logits)
        m2 = jnp.max(rest, axis=1, keepdims=True)
        i2 = jnp.min(jnp.where(rest == m2, lane, LANES), axis=1, keepdims=True)
        e2 = jnp.exp(m2 - m1)
        p1 = 1.0 / (1.0 + e2)
        comb = jnp.where(lane == i1, p1, 0.0) + jnp.where(lane == i2, e2 * p1, 0.0)
        comb_scr[...] = comb
        for s in range(MOE_NSUB):
            combt_scr[s] = comb[s * MOE_SUB:(s + 1) * MOE_SUB, :].T

    sub = lambda s: slice(s * MOE_SUB, (s + 1) * MOE_SUB)
    slot = lambda s: slice(s * MOE_CAP, (s + 1) * MOE_CAP)
    r_i = lax.broadcasted_iota(jnp.int32, (MOE_SUB, MOE_SUB), 0)
    c_i = lax.broadcasted_iota(jnp.int32, (MOE_SUB, MOE_SUB), 1)
    before_row = (c_i < r_i).astype(F32).astype(BF16)
    before_col = (r_i < c_i).astype(F32).astype(BF16)
    slot_row = lax.broadcasted_iota(jnp.int32, (MOE_CAP, MOE_SUB), 0).astype(F32)
    slot_col = lax.broadcasted_iota(jnp.int32, (MOE_SUB, MOE_CAP), 1).astype(F32)

    w_col = jnp.sum(jnp.where(lane == e, comb_scr[...], 0.0), axis=1, keepdims=True)
    hit_col = [(w_col[sub(s), :] > 0.0).astype(F32) for s in range(MOE_NSUB)]
    hit_row = [(combt_scr[s, pl.ds(e, 1), :] > 0.0).astype(F32) for s in range(MOE_NSUB)]
    rank_col = [jnp.dot(before_row, jnp.broadcast_to(hit_col[s], (MOE_SUB, MOE_CAP)).astype(BF16),
                        preferred_element_type=F32) for s in range(MOE_NSUB)]
    rank_row = [jnp.dot(jnp.broadcast_to(hit_row[s], (SUBLANES, MOE_SUB)).astype(BF16), before_col,
                        preferred_element_type=F32)[0:1, :] for s in range(MOE_NSUB)]
    count = jnp.sum(hit_row[0], axis=1, keepdims=True)
    for s in range(1, MOE_NSUB):
        count = jnp.maximum(count, jnp.sum(hit_row[s], axis=1, keepdims=True))
    n_pass = lax.div(jnp.max(count).astype(jnp.int32) + (MOE_CAP - 1), MOE_CAP)

    def one_pass(p, carry):
        base = (p * MOE_CAP).astype(F32)
        for s in range(MOE_NSUB):
            pick = jnp.logical_and(rank_row[s] - base == slot_row, hit_row[s] > 0.0).astype(F32).astype(BF16)
            xe_scr[slot(s), :] = jnp.dot(pick, h_scr[sub(s), :], preferred_element_type=F32).astype(BF16)
        xe = xe_scr[...]
        hg = jnp.dot(xe, wg_ref[0], preferred_element_type=F32)
        hu = jnp.dot(xe, wu_ref[0], preferred_element_type=F32)
        y = jnp.dot((_silu(hg) * hu).astype(BF16), wd_ref[0], preferred_element_type=F32)
        for s in range(MOE_NSUB):
            put = jnp.logical_and(rank_col[s] - base == slot_col, hit_col[s] > 0.0).astype(F32).astype(BF16)
            hi, lo = _split(y[slot(s), :])
            back = jnp.dot(put, hi, preferred_element_type=F32) + jnp.dot(put, lo, preferred_element_type=F32)
            acc_scr[sub(s), :] += w_col[sub(s), :] * back
        return carry

    lax.fori_loop(0, n_pass, one_pass, 0)

    @pl.when(e == ne - 1)
    def _():
        out = x_ref[...] + acc_scr[...]
        if final:
            out = _rms(out, fin_ref[...])
        o_ref[...] = out


def _moe(x, g, wg, wu, wd, router, fin, final):
    n = x.shape[0]
    ne, _, f = wg.shape
    return pl.pallas_call(
        functools.partial(_moe_kernel, ne=ne, final=final),
        grid=(n // MOE_TM, ne),
        in_specs=[pl.BlockSpec((MOE_TM, D_MODEL), lambda i, e: (i, 0)),
                  pl.BlockSpec((1, D_MODEL), lambda i, e: (0, 0)),
                  pl.BlockSpec((D_MODEL, LANES), lambda i, e: (0, 0)),
                  pl.BlockSpec((1, D_MODEL, f), lambda i, e: (e, 0, 0)),
                  pl.BlockSpec((1, D_MODEL, f), lambda i, e: (e, 0, 0)),
                  pl.BlockSpec((1, f, D_MODEL), lambda i, e: (e, 0, 0)),
                  pl.BlockSpec((1, D_MODEL), lambda i, e: (0, 0))],
        out_specs=pl.BlockSpec((MOE_TM, D_MODEL), lambda i, e: (i, 0)),
        out_shape=jax.ShapeDtypeStruct((n, D_MODEL), F32),
        scratch_shapes=[pltpu.VMEM((MOE_TM, D_MODEL), BF16), pltpu.VMEM((MOE_TM, D_MODEL), F32),
                        pltpu.VMEM((MOE_TM, LANES), F32), pltpu.VMEM((MOE_NSUB, LANES, MOE_SUB), F32),
                        pltpu.VMEM((MOE_NSUB * MOE_CAP, D_MODEL), BF16)],
        compiler_params=_cparams(("parallel", "arbitrary")),
        name="moe",
    )(x, g, router, wg, wu, wd, fin)


def _row(v):
    return v.reshape(1, -1)


def kernel(x_prompt, x_sample, state_ret, state_mlstm_C, state_mlstm_n, state_mlstm_m, state_mlstm_conv, state_rwkv, state_rwkv_shift, norm_mix, w_in, ret_gn, ml_conv_w, ml_conv_b, ml_wq, ml_wk, ml_bi, ml_bf, ml_gn, ml_skip, rw_mu, rw_w0, rw_w2, rw_a0, rw_a2, rw_g2, rw_kk, rw_ka, rw_rk, rw_gn_w, rw_gn_b, w_branch, w_out, norm_ffn, ffn_w_gate, ffn_w_up, ffn_w_down, moe_router, moe_w_gate, moe_w_up, moe_w_down, final_norm):
    nc_p = SEQ // CHUNK
    xs = jnp.pad(x_sample, ((0, 0), (0, S_PAD - DEC_SEQ), (0, 0)))
    x = jnp.concatenate([x_prompt.reshape(N_PROMPT, D_MODEL), xs.reshape(N_SAMPLE, D_MODEL)], axis=0)

    pos_p = jnp.arange(SEQ, dtype=jnp.int32)
    pos_s = PAST_LEN + jnp.arange(S_PAD, dtype=jnp.int32)
    ret_tab_p = _retention_tables(CHUNK, CHUNK, pos_p)
    ret_tab_s = _retention_tables(S_PAD, DEC_SEQ, pos_s)
    head_of = jnp.arange(RW_W) // RW_DH
    bd64 = (head_of[:, None] == head_of[None, :]).astype(BF16)
    m0_all = jnp.pad(state_mlstm_m, ((0, 0), (0, 0), (0, LANES - ML_HEADS))).reshape(DEPTH, DEC_BATCH, 1, LANES)

    o_rq = 0
    o_mx = o_rq + RET_COLS
    o_mi = o_mx + 2 * ML_W
    o_mo = o_mi + 2 * ML_HEADS
    o_rw = o_mo + ML_W
    o_gate = o_rw + RW_COLS

    new_p, new_s = [], []
    for l in range(DEPTH):
        w = w_in[l]
        w_ret = w[:, o_rq:o_mx].astype(BF16)
        w_ml = jnp.concatenate([w[:, o_mx:o_mi], w[:, o_mo:o_rw], w[:, o_mi:o_mo],
                                jnp.zeros((D_MODEL, ML_GATE_W - 2 * ML_HEADS), F32)], axis=1).astype(BF16)
        w_rw = w[:, o_rw:o_gate].astype(BF16)
        w_gate = w[:, o_gate:].astype(BF16)
        g_mix = _row(norm_mix[l])
        zret, = _norm_matmul(x, g_mix, w_ret, [(RET_COLS, BF16)])
        zml, zml_gates = _norm_matmul(x, g_mix, w_ml, [(3 * ML_W, BF16), (ML_GATE_W, F32)])
        zrw, = _norm_matmul(x, g_mix, w_rw, [(RW_COLS, BF16)])
        zgate, = _norm_matmul(x, g_mix, w_gate, [(N_BRANCH * D_MODEL, BF16)])

        gn = _row(ret_gn[l])
        yr_p, ret_p = _retention(zret, ret_tab_p, gn, None, l, B=BATCH, nc=nc_p, L=CHUNK, nb=RET_NB_PROMPT,
                                 split=True, row0=0)
        yr_s, ret_s = _retention(zret, ret_tab_s, gn, state_ret, l, B=DEC_BATCH, nc=1, L=S_PAD, nb=RET_NB_SAMPLE,
                                 split=False, row0=N_PROMPT // (RET_NB_SAMPLE * S_PAD))

        gate_bias = jnp.concatenate([ml_bi[l], ml_bf[l], jnp.zeros((ML_GATE_W - 2 * ML_HEADS,), F32)])
        ml_weights = (ml_conv_w[l], _row(ml_conv_b[l]), ml_wq[l].astype(BF16), ml_wk[l].astype(BF16),
                      _row(gate_bias), _row(ml_gn[l]), _row(ml_skip[l]))
        ym_p, c_p, n_p, m_p, buf_p = _mlstm(zml, zml_gates, ml_weights, None, l, B=BATCH, nc=nc_p, L=CHUNK, Lv=CHUNK,
                                            nb=ML_NB_PROMPT, split=True, row0=0)
        ym_s, c_s, n_s, m_s, buf_s = _mlstm(zml, zml_gates, ml_weights,
                                            (state_mlstm_C, state_mlstm_n, m0_all, state_mlstm_conv),
                                            l, B=DEC_BATCH, nc=1, L=S_PAD, Lv=DEC_SEQ, nb=ML_NB_SAMPLE, split=False,
                                            row0=N_PROMPT // (ML_NB_SAMPLE * S_PAD))

        w2a = jnp.zeros((RW_LORA_W + RW_LORA_A, 2 * RW_W), F32)
        w2a = w2a.at[:RW_LORA_W, :RW_W].set(rw_w2[l]).at[RW_LORA_W:, RW_W:].set(rw_a2[l])
        rw_weights = (_row(rw_mu[l]), _row(jnp.concatenate([rw_w0[l], rw_a0[l]])), w2a.astype(BF16),
                      rw_g2[l].astype(BF16), _row(rw_kk[l]), _row(rw_ka[l]), _row(rw_rk[l]), bd64,
                      _row(rw_gn_w[l]), _row(rw_gn_b[l]))
        yw_p, rws_p, shift_p = _rwkv(zrw, rw_weights, None, l, B=BATCH, T=SEQ, C=RW_C, CB=RW_CB_PROMPT, Lv=RW_C,
                                     row0=0)
        yw_s, rws_s, shift_s = _rwkv(zrw, rw_weights, (state_rwkv, state_rwkv_shift), l, B=DEC_BATCH, T=S_PAD,
                                     C=S_PAD, CB=RW_CB_SAMPLE, Lv=DEC_SEQ, row0=N_PROMPT // S_PAD)

        x = _merge(x, zgate, (yr_p, ym_p, yw_p), (yr_s, ym_s, yw_s), w_branch[l].astype(BF16), w_out[l].astype(BF16))

        i = l // 2
        g_ffn = _row(norm_ffn[l])
        fin = _row(final_norm)
        final = l == DEPTH - 1
        if l % 2 == 0:
            halves = D_FF // D_FF_EXPERT
            wg = ffn_w_gate[i].reshape(D_MODEL, halves, D_FF_EXPERT).transpose(1, 0, 2).astype(BF16)
            wu = ffn_w_up[i].reshape(D_MODEL, halves, D_FF_EXPERT).transpose(1, 0, 2).astype(BF16)
            wd = ffn_w_down[i].reshape(halves, D_FF_EXPERT, D_MODEL).astype(BF16)
            x = _ffn(x, g_ffn, wg, wu, wd, fin, final)
        else:
            router = jnp.pad(moe_router[i], ((0, 0), (0, LANES - N_EXPERTS)))
            x = _moe(x, g_ffn, moe_w_gate[i].astype(BF16), moe_w_up[i].astype(BF16), moe_w_down[i].astype(BF16),
                     router, fin, final)

        new_p.append((ret_p, c_p, n_p, m_p[:, 0, :ML_HEADS], buf_p, rws_p, shift_p))
        new_s.append((ret_s, c_s, n_s, m_s[:, 0, :ML_HEADS], buf_s, rws_s, shift_s))

    y_prompt = x[:N_PROMPT].reshape(BATCH, SEQ, D_MODEL)
    y_sample = x[N_PROMPT:].reshape(DEC_BATCH, S_PAD, D_MODEL)[:, :DEC_SEQ]
    st_p = tuple(jnp.stack([st[j] for st in new_p]) for j in range(7))
    st_s = tuple(jnp.stack([st[j] for st in new_s]) for j in range(7))
    return (y_prompt, y_sample) + st_p + st_s
```
